```python
import jax, jax.numpy as jnp
from jax import lax
import numpy as np

D_MODEL = 1024
BATCH = 2
SEQ = 16384
DEPTH = 4
DEC_BATCH = 8
DEC_SEQ = 32
PAST_LEN = 4096

CHUNK = 64
META_TOKENS = 16
EPS = 1e-6
M_HEADS = 4
M_W = 3 * D_MODEL // 8
M_DH = M_W // M_HEADS
CONV_W = 4
P_GROUPS = 4
P_W = D_MODEL // 4
P_GW = P_W // P_GROUPS
POOL_WINDOWS = (2, 4, 8, 16)
POOL_HIST = 15
H_HEADS = 4
H_W = 3 * D_MODEL // 8
H_DV = H_W // H_HEADS
H_DK = 128
H_KW = H_HEADS * H_DK
D_MIX = M_W + P_W + H_W
SPLIT_WIDTHS = (M_W, M_W, M_W, M_W, M_HEADS, M_HEADS, P_W, H_KW, H_KW, H_W, H_W)
D_IN = sum(SPLIT_WIDTHS)
D_FF = 7 * D_MODEL // 2
N_EXPERTS = 8
TOP_K = 2
N_DENSE = (DEPTH + 1) // 2
N_MOE = DEPTH // 2

kernel_name = "hybrid_mlstm_pool_hgrn2_stream_step"


def _split_points():
    pts, acc = [], 0
    for w in SPLIT_WIDTHS[:-1]:
        acc += w
        pts.append(acc)
    return pts


def _rmsnorm(x, g):
    xf = x.astype(jnp.float32)
    y = xf * lax.rsqrt(jnp.mean(xf * xf, axis=-1, keepdims=True) + EPS)
    return y.astype(x.dtype) * g


def _head_rmsnorm(y, n_heads, g):
    B, T, W = y.shape
    yh = y.reshape(B, T, n_heads, W // n_heads)
    yh = yh * lax.rsqrt(jnp.mean(yh * yh, axis=-1, keepdims=True) + EPS)
    return yh.reshape(B, T, W) * g


def _heads(a, n):
    B, T, W = a.shape
    return a.reshape(B, T, n, W // n).transpose(0, 2, 1, 3)


def _merge(a):
    B, n, T, d = a.shape
    return a.transpose(0, 2, 1, 3).reshape(B, T, n * d)


def _to_chunks(a, L):
    B, H, T = a.shape[:3]
    a = a.reshape((B, H, T // L, L) + a.shape[3:])
    return jnp.moveaxis(a, 2, 0)


def _from_chunks(a):
    a = jnp.moveaxis(a, 0, 2)
    return a.reshape(a.shape[:2] + (-1,) + a.shape[4:])


def _run_segments(block_fn, carry, xs, segments):
    outs = []
    for start, length, blk in segments:
        seg = tuple(_to_chunks(a[:, :, start:start + length], blk) for a in xs)
        carry, o = lax.scan(block_fn, carry, seg)
        outs.append(_from_chunks(o))
    return jnp.concatenate(outs, axis=2), carry


def _mlstm_block(carry, xs):
    C, n, m = carry
    q, k, v, ig, lf = xs
    L = q.shape[2]
    causal = jnp.tril(jnp.ones((L, L), dtype=bool))
    b = jnp.cumsum(lf, axis=-1)
    a = b + m[..., None]
    dmat = jnp.where(causal, b[..., :, None] - b[..., None, :] + ig[..., None, :], -jnp.inf)
    m_t = jnp.maximum(a, jnp.max(dmat, axis=-1))
    w = jnp.exp(dmat - m_t[..., None])
    w_inter = jnp.exp(a - m_t)
    s = jnp.einsum('bhtd,bhsd->bhts', q, k) * w
    num = w_inter[..., None] * jnp.einsum('bhtd,bhde->bhte', q, C) + jnp.einsum('bhts,bhse->bhte', s, v)
    den = w_inter * jnp.einsum('bhtd,bhd->bht', q, n) + jnp.sum(s, axis=-1)
    h = num / jnp.maximum(jnp.abs(den), jnp.exp(-m_t))[..., None]
    m_new = m_t[..., -1]
    wl = jnp.exp(b[..., -1:] - b + ig - m_new[..., None])
    decay = jnp.exp(a[..., -1] - m_new)
    C_new = decay[..., None, None] * C + jnp.einsum('bhs,bhsd,bhse->bhde', wl, k, v)
    n_new = decay[..., None] * n + jnp.einsum('bhs,bhsd->bhd', wl, k)
    return (C_new, n_new, m_new), h


def _hgrn_block(S, xs):
    q, k, i, lg = xs
    L = q.shape[2]
    causal = jnp.tril(jnp.ones((L, L), dtype=bool))
    g = jnp.cumsum(lg, axis=2)
    rel = jnp.where(causal[..., None], g[:, :, :, None, :] - g[:, :, None, :, :], -jnp.inf)
    att = jnp.einsum('bhtc,bhsc,bhtsc->bhts', q, k, jnp.exp(rel))
    o = jnp.einsum('bhtc,bhce->bhte', q * jnp.exp(g), S) + jnp.einsum('bhts,bhse->bhte', att, i)
    gl = g[:, :, -1]
    S_new = jnp.exp(gl)[..., None] * S + jnp.einsum('bhsc,bhse->bhce', k * jnp.exp(gl[:, :, None, :] - g), i)
    return S_new, o


def _causal_conv(u, prev, w):
    T = u.shape[1]
    full = jnp.concatenate([prev, u], axis=1)
    out = full[:, 0:T] * w[0]
    for j in range(1, CONV_W):
        out = out + full[:, j:j + T] * w[j]
    return out, full[:, -(CONV_W - 1):]


def _multiscale_pool(u, prev, pos0):
    B, T, _ = u.shape
    full = jnp.concatenate([prev, u], axis=1)
    cs = jnp.concatenate([jnp.zeros((B, 1, P_W), jnp.float32), jnp.cumsum(full, axis=1)], axis=1)
    P = POOL_HIST
    pos = (pos0 + jnp.arange(T)).astype(jnp.float32)
    end = cs[:, P + 1:P + 1 + T]
    parts = []
    for gi, w in enumerate(POOL_WINDOWS):
        sl = slice(gi * P_GW, (gi + 1) * P_GW)
        start = cs[:, P + 1 - w:P + 1 - w + T, sl]
        cnt = jnp.minimum(jnp.float32(w), pos + 1.0)
        parts.append((end[..., sl] - start) / cnt[None, :, None])
    mean = jnp.concatenate(parts, axis=-1)
    return mean - u, full[:, -POOL_HIST:]


def _token_mixer(hn, l, lb, c_C, c_n, c_m, c_conv, c_pool, c_hgrn, segments, pos0, p):
    f32 = jnp.float32
    B, T, _ = hn.shape
    proj = jnp.einsum('btd,de->bte', hn, p['w_in'][l]).astype(f32)
    mq, mk, mv, mo, mi, mf, pu, hq, hf, hi, hg = jnp.split(proj, _split_points(), axis=-1)
    qk, conv_new = _causal_conv(jnp.concatenate([mq, mk], axis=-1), c_conv.astype(f32), p['conv_w'][l].astype(f32))
    qk = jax.nn.silu(qk)
    q = _heads(qk[..., :M_W], M_HEADS)
    k = _heads(qk[..., M_W:], M_HEADS) * (M_DH ** -0.5)
    v = _heads(mv, M_HEADS)
    ig = (mi + p['b_igate'][l].astype(f32)).transpose(0, 2, 1)
    lf = jax.nn.log_sigmoid(mf + p['b_fgate'][l].astype(f32)).transpose(0, 2, 1)
    h_m, (C, n, m) = _run_segments(_mlstm_block, (c_C.astype(f32), c_n.astype(f32), c_m.astype(f32)),
                                   (q, k, v, ig, lf), segments)
    y_m = _head_rmsnorm(jax.nn.sigmoid(mo) * _merge(h_m), M_HEADS, p['mlstm_norm_g'][l].astype(f32))
    pooled, pool_new = _multiscale_pool(pu, c_pool.astype(f32), pos0)
    y_p = jnp.einsum('btgc,gce->btge', pooled.reshape(B, T, P_GROUPS, P_GW), p['pool_w'][l].astype(f32))
    y_p = y_p.reshape(B, T, P_W) * p['pool_scale'][l].astype(f32)
    fg = lb + (1.0 - lb) * jax.nn.sigmoid(hf)
    o, S = _run_segments(_hgrn_block, c_hgrn.astype(f32),
                         (_heads(jax.nn.silu(hq), H_HEADS), _heads(1.0 - fg, H_HEADS),
                          _heads(hi, H_HEADS), _heads(jnp.log(fg), H_HEADS)), segments)
    y_h = _head_rmsnorm(_merge(o), H_HEADS, p['hgrn_norm_g'][l].astype(f32)) * jax.nn.silu(hg)
    y = jnp.concatenate([y_m, y_p, y_h], axis=-1).astype(hn.dtype)
    out = jnp.einsum('bte,ed->btd', y, p['w_out'][l])
    return out, (C, n, m, conv_new, pool_new, S)


def _swiglu(x, w1, w3, w2):
    return jnp.einsum('btf,fd->btd', jax.nn.silu(jnp.einsum('btd,df->btf', x, w1)) * jnp.einsum('btd,df->btf', x, w3), w2)


def _moe(x, w_router, w1, w3, w2):
    logits = jnp.einsum('btd,de->bte', x, w_router).astype(jnp.float32)
    top_v, top_i = lax.top_k(logits, TOP_K)
    gates = jax.nn.softmax(top_v, axis=-1)
    comb = jnp.sum(jax.nn.one_hot(top_i, N_EXPERTS, dtype=jnp.float32) * gates[..., None], axis=-2)
    out = jnp.zeros(x.shape, jnp.float32)
    for e in range(N_EXPERTS):
        out = out + comb[..., e:e + 1] * _swiglu(x, w1[e], w3[e], w2[e]).astype(jnp.float32)
    return out.astype(x.dtype)


def _trunk(x, c_C, c_n, c_m, c_conv, c_pool, c_hgrn, segments, pos0, p):
    pl = jax.nn.softmax(p['lb_logits'].astype(jnp.float32), axis=0)
    lbs = jnp.cumsum(pl, axis=0) - pl[0]
    new = ([], [], [], [], [], [])
    for l in range(DEPTH):
        y, st = _token_mixer(_rmsnorm(x, p['norm1_g'][l]), l, lbs[l], c_C[l], c_n[l], c_m[l],
                             c_conv[l], c_pool[l], c_hgrn[l], segments, pos0, p)
        x = x + y
        hn = _rmsnorm(x, p['norm2_g'][l])
        if l % 2 == 0:
            i = l // 2
            x = x + _swiglu(hn, p['ffn_w1'][i], p['ffn_w3'][i], p['ffn_w2'][i])
        else:
            i = l // 2
            x = x + _moe(hn, p['router_w'][i], p['moe_w1'][i], p['moe_w3'][i], p['moe_w2'][i])
        for lst, s in zip(new, st):
            lst.append(s)
    x = _rmsnorm(x, p['final_g'])
    return (x,) + tuple(jnp.stack(s, axis=0) for s in new)


def setup_inputs(seed: int = 0) -> dict:
    key = jax.random.key(seed)
    ks = jax.random.split(key, 32)
    nrm = lambda k, s, sc: jax.random.normal(k, s, jnp.float32) * sc
    f_bias = jnp.linspace(3.0, 6.0, M_HEADS, dtype=jnp.float32)[None, :]
    return {
        "x_prompt": nrm(ks[0], (BATCH, SEQ, D_MODEL), 1.0),
        "x_sample": nrm(ks[1], (DEC_BATCH, DEC_SEQ, D_MODEL), 1.0),
        "state_mlstm_C": nrm(ks[2], (DEPTH, DEC_BATCH, M_HEADS, M_DH, M_DH), 0.3),
        "state_mlstm_n": nrm(ks[3], (DEPTH, DEC_BATCH, M_HEADS, M_DH), 0.3),
        "state_mlstm_m": nrm(ks[4], (DEPTH, DEC_BATCH, M_HEADS), 1.0),
        "state_mlstm_conv": nrm(ks[5], (DEPTH, DEC_BATCH, CONV_W - 1, 2 * M_W), 1.0),
        "state_pool": nrm(ks[6], (DEPTH, DEC_BATCH, POOL_HIST, P_W), 1.0),
        "state_hgrn": nrm(ks[7], (DEPTH, DEC_BATCH, H_HEADS, H_DK, H_DV), 0.3),
        "meta_tokens": nrm(ks[8], (META_TOKENS, D_MODEL), 1.0),
        "norm1_g": 1.0 + nrm(ks[9], (DEPTH, D_MODEL), 0.05),
        "norm2_g": 1.0 + nrm(ks[10], (DEPTH, D_MODEL), 0.05),
        "final_g": 1.0 + nrm(ks[11], (D_MODEL,), 0.05),
        "w_in": nrm(ks[12], (DEPTH, D_MODEL, D_IN), D_MODEL ** -0.5),
        "b_igate": nrm(ks[13], (DEPTH, M_HEADS), 0.1),
        "b_fgate": f_bias + nrm(ks[14], (DEPTH, M_HEADS), 0.1),
        "conv_w": nrm(ks[15], (DEPTH, CONV_W, 2 * M_W), CONV_W ** -0.5),
        "mlstm_norm_g": 1.0 + nrm(ks[16], (DEPTH, M_W), 0.05),
        "pool_w": nrm(ks[17], (DEPTH, P_GROUPS, P_GW, P_GW), P_GW ** -0.5),
        "pool_scale": 1.0 + nrm(ks[18], (DEPTH, P_W), 0.1),
        "lb_logits": nrm(ks[19], (DEPTH, H_KW), 0.5),
        "hgrn_norm_g": 1.0 + nrm(ks[20], (DEPTH, H_W), 0.05),
        "w_out": nrm(ks[21], (DEPTH, D_MIX, D_MODEL), D_MIX ** -0.5),
        "ffn_w1": nrm(ks[22], (N_DENSE, D_MODEL, D_FF), D_MODEL ** -0.5),
        "ffn_w3": nrm(ks[23], (N_DENSE, D_MODEL, D_FF), D_MODEL ** -0.5),
        "ffn_w2": nrm(ks[24], (N_DENSE, D_FF, D_MODEL), D_FF ** -0.5),
        "router_w": nrm(ks[25], (N_MOE, D_MODEL, N_EXPERTS), D_MODEL ** -0.5),
        "moe_w1": nrm(ks[26], (N_MOE, N_EXPERTS, D_MODEL, D_FF), D_MODEL ** -0.5),
        "moe_w3": nrm(ks[27], (N_MOE, N_EXPERTS, D_MODEL, D_FF), D_MODEL ** -0.5),
        "moe_w2": nrm(ks[28], (N_MOE, N_EXPERTS, D_FF, D_MODEL), D_FF ** -0.5),
    }


def reference(x_prompt, x_sample, state_mlstm_C, state_mlstm_n, state_mlstm_m, state_mlstm_conv,
              state_pool, state_hgrn, meta_tokens, norm1_g, norm2_g, final_g, w_in, b_igate, b_fgate,
              conv_w, mlstm_norm_g, pool_w, pool_scale, lb_logits, hgrn_norm_g, w_out,
              ffn_w1, ffn_w3, ffn_w2, router_w, moe_w1, moe_w3, moe_w2):
    p = {
        'norm1_g': norm1_g, 'norm2_g': norm2_g, 'final_g': final_g, 'w_in': w_in,
        'b_igate': b_igate, 'b_fgate': b_fgate, 'conv_w': conv_w, 'mlstm_norm_g': mlstm_norm_g,
        'pool_w': pool_w, 'pool_scale': pool_scale, 'lb_logits': lb_logits, 'hgrn_norm_g': hgrn_norm_g,
        'w_out': w_out, 'ffn_w1': ffn_w1, 'ffn_w3': ffn_w3, 'ffn_w2': ffn_w2, 'router_w': router_w,
        'moe_w1': moe_w1, 'moe_w3': moe_w3, 'moe_w2': moe_w2,
    }
    B, T_p, _ = x_prompt.shape
    meta = jnp.broadcast_to(meta_tokens[None].astype(x_prompt.dtype), (B, META_TOKENS, D_MODEL))
    h0 = jnp.concatenate([meta, x_prompt], axis=1)

    def zeros(*s):
        return jnp.zeros((DEPTH, B) + s, jnp.float32)

    seg_p = ((0, META_TOKENS, META_TOKENS), (META_TOKENS, T_p, CHUNK))
    yp, pC, pn, pm, pconv, ppool, phgrn = _trunk(
        h0, zeros(M_HEADS, M_DH, M_DH), zeros(M_HEADS, M_DH), zeros(M_HEADS),
        zeros(CONV_W - 1, 2 * M_W), zeros(POOL_HIST, P_W), zeros(H_HEADS, H_DK, H_DV),
        seg_p, 0, p)
    y_prompt = yp[:, META_TOKENS:]
    T_s = x_sample.shape[1]
    seg_s = ((0, T_s, T_s),)
    y_sample, sC, sn, sm, sconv, spool, shgrn = _trunk(
        x_sample, state_mlstm_C, state_mlstm_n, state_mlstm_m, state_mlstm_conv, state_pool, state_hgrn,
        seg_s, META_TOKENS + PAST_LEN, p)
    return (y_prompt, y_sample, pC, pn, pm, pconv, ppool, phgrn, sC, sn, sm, sconv, spool, shgrn)
```

```python
import functools

import jax
import jax.numpy as jnp
from jax import lax
from jax.experimental import pallas as pl
from jax.experimental.pallas import tpu as pltpu

F32 = jnp.float32
BF16 = jnp.bfloat16

D_MODEL = 1024
DEPTH = 4
META_TOKENS = 16
PAST_LEN = 4096
EPS = 1e-6
NH = 4
DH = 96
HP = 128
HW = NH * HP
P_W = 256
P_GW = 64
POOL_WINDOWS = (2, 4, 8, 16)
D_FF = 3584
N_EXPERTS = 8
NEG = -1e30

OFF_MQ, OFF_MK, OFF_MV, OFF_MO = 0, 512, 1024, 1536
OFF_G = 2048
OFF_PU = 2176
OFF_HQ, OFF_HF, OFF_HI, OFF_HG = 2432, 2944, 3456, 3968
D_INP = 4480
OFF_YM, OFF_YP, OFF_YH = 0, 512, 768
D_MIXP = 1280

M_CHUNK = 128
H_CHUNK = 64
H_SUB = 16
CONV_HDR = 8
POOL_HDR = 32

VMEM_LIMIT = 60 * 1024 * 1024


def _sigmoid(x):
    return 1.0 / (1.0 + jnp.exp(-x))


def _dot(a, b):
    return jnp.dot(a, b, preferred_element_type=F32)


def _dot_nt(a, b):
    return lax.dot_general(a, b, (((1,), (1,)), ((), ())), preferred_element_type=F32)


def _dot_tn(a, b):
    return lax.dot_general(a, b, (((0,), (0,)), ((), ())), preferred_element_type=F32)


def _cumsum_rows(tril_bf, x):
    n = x.shape[1]
    hi = x.astype(BF16)
    lo = (x - hi.astype(F32)).astype(BF16)
    both = _dot(tril_bf, jnp.concatenate([hi, lo], axis=1))
    return both[:, :n] + both[:, n:]


def _bcast_rows(row, n):
    return jnp.broadcast_to(row, (n, row.shape[1]))


def _mixer_kernel(x_ref, c0_ref, m0_ref, conv0_ref, pool0_ref, s0_ref,
                  g1_ref, win_ref, gb_ref, cw_ref, mng_ref, pw_ref, ps_ref, lbl_ref, hng_ref, wout_ref,
                  xo_ref, cf_ref, mf_ref, convf_ref, poolf_ref, sf_ref,
                  proj_s, qk_s, u_s, s2_s, s4_s, s8_s, hk_s, gate_s, y_s, c_s, m_s, st_s,
                  *, layer, tv, lb, pos0, n_t):
    pad = lb - tv
    t = pl.program_id(1)

    @pl.when(t == 0)
    def _init():
        c_s[...] = c0_ref[...]
        m_s[...] = m0_ref[...]
        st_s[...] = s0_ref[...]
        qk_s[...] = jnp.zeros(qk_s.shape, F32)
        qk_s[pad:pad + CONV_HDR, :] = conv0_ref[...]
        u_s[...] = jnp.zeros(u_s.shape, F32)
        u_s[pad + 16:pad + 32, :] = pool0_ref[...]
        s2_s[0:POOL_HDR, :] = jnp.zeros((POOL_HDR, P_W), F32)
        s4_s[0:POOL_HDR, :] = jnp.zeros((POOL_HDR, P_W), F32)
        s8_s[0:POOL_HDR, :] = jnp.zeros((POOL_HDR, P_W), F32)
        if pad:
            proj_s[0:pad, :] = jnp.zeros((pad, D_INP), F32)

    x = x_ref[...]
    ms = jnp.mean(x * x, axis=-1, keepdims=True)
    hn = (x * lax.rsqrt(ms + EPS) * g1_ref[...]).astype(BF16)
    qk_s[CONV_HDR + pad:CONV_HDR + lb, :] = _dot(hn, win_ref[:, 0:OFF_MV])
    proj_s[pad:lb, OFF_MV:OFF_PU] = _dot(hn, win_ref[:, OFF_MV:OFF_PU])
    u_s[POOL_HDR + pad:POOL_HDR + lb, :] = _dot(hn, win_ref[:, OFF_PU:OFF_HQ])
    proj_s[pad:lb, OFF_HQ:D_INP] = _dot(hn, win_ref[:, OFF_HQ:D_INP])

    row = lax.broadcasted_iota(jnp.int32, (lb, HP), 0)
    lane = lax.broadcasted_iota(jnp.int32, (lb, HP), 1)

    acc = qk_s[5:5 + lb, :] * cw_ref[0:1, :]
    for j in range(1, 4):
        acc = acc + qk_s[5 + j:5 + j + lb, :] * cw_ref[j:j + 1, :]
    qk = acc * _sigmoid(acc)
    proj_s[:, OFF_MQ:OFF_MK] = qk[:, 0:HW]
    proj_s[:, OFF_MK:OFF_MV] = qk[:, HW:2 * HW] * (DH ** -0.5)
    conv_tail = qk_s[lb:lb + CONV_HDR, :]
    qk_s[0:CONV_HDR, :] = conv_tail

    gpre = proj_s[:, OFF_G:OFF_G + HP] + gb_ref[...]
    lsig = jnp.minimum(gpre, 0.0) - jnp.log(1.0 + jnp.exp(-jnp.abs(gpre)))
    gates = jnp.where(lane < NH, gpre, jnp.where(lane < 2 * NH, lsig, 0.0))
    if pad:
        gates = jnp.where(row >= pad, gates, jnp.where(lane < NH, NEG, 0.0))
    gate_s[...] = gates

    rr = lax.broadcasted_iota(jnp.int32, (M_CHUNK, M_CHUNK), 0)
    cc = lax.broadcasted_iota(jnp.int32, (M_CHUNK, M_CHUNK), 1)
    causal = rr >= cc
    tril_m = jnp.where(causal, 1.0, 0.0).astype(BF16)
    lane_c = lax.broadcasted_iota(jnp.int32, (M_CHUNK, HP), 1)

    def m_chunk(c, carry):
        r0 = pl.multiple_of(c * M_CHUNK, M_CHUNK)
        rows = pl.ds(r0, M_CHUNK)
        gt = gate_s[rows, :]
        bcum = _cumsum_rows(tril_m, jnp.where(lane_c >= NH, gt, 0.0))
        gbt = jnp.where(lane_c < NH, gt, bcum)
        gbt_t = gbt.T
        for h in range(NH):
            q = proj_s[rows, OFF_MQ + h * HP:OFF_MQ + (h + 1) * HP]
            k = proj_s[rows, OFF_MK + h * HP:OFF_MK + (h + 1) * HP]
            v = proj_s[rows, OFF_MV + h * HP:OFF_MV + (h + 1) * HP]
            ig_col = gbt[:, h:h + 1]
            b_col = gbt[:, NH + h:NH + h + 1]
            ig_row = gbt_t[h:h + 1, :]
            b_row = gbt_t[NH + h:NH + h + 1, :]
            m_prev = m_s[h:h + 1, 0:1]
            a_col = b_col + m_prev
            dm = jnp.where(causal, b_col - b_row + ig_row, NEG)
            m_t = jnp.maximum(a_col, jnp.max(dm, axis=1, keepdims=True))
            w = jnp.exp(dm - m_t)
            w_int = jnp.exp(a_col - m_t)
            s = _dot_nt(q.astype(BF16), k.astype(BF16)) * w
            vaug = jnp.where(lane_c == DH, 1.0, v)
            caug = c_s[h]
            lhs = jnp.concatenate([s, q * w_int], axis=1).astype(BF16)
            rhs = jnp.concatenate([vaug, caug], axis=0).astype(BF16)
            nd = _dot(lhs, rhs)
            den = nd[:, DH:DH + 1]
            denom = jnp.maximum(jnp.abs(den), jnp.exp(-m_t))
            hh = jnp.where(lane_c < DH, nd / denom, 0.0)
            m_new = m_t[M_CHUNK - 1:M_CHUNK, :]
            wl = jnp.exp(b_col[M_CHUNK - 1:M_CHUNK, :] - b_col + ig_col - m_new)
            decay = jnp.exp(a_col[M_CHUNK - 1:M_CHUNK, :] - m_new)
            c_s[h] = decay * caug + _dot_tn((k * wl).astype(BF16), vaug.astype(BF16))
            m_s[h:h + 1, :] = jnp.broadcast_to(m_new, (1, HP))
            mo = proj_s[rows, OFF_MO + h * HP:OFF_MO + (h + 1) * HP]
            yh = hh * _sigmoid(mo)
            msq = jnp.sum(yh * yh, axis=1, keepdims=True) * (1.0 / DH)
            yn = yh * lax.rsqrt(msq + EPS) * mng_ref[:, h * HP:(h + 1) * HP]
            y_s[rows, OFF_YM + h * HP:OFF_YM + (h + 1) * HP] = yn.astype(BF16)
        return carry

    lax.fori_loop(0, lb // M_CHUNK, m_chunk, 0)

    n_ext = lb + 16
    s2_s[16:16 + n_ext, :] = u_s[16:16 + n_ext, :] + u_s[15:15 + n_ext, :]
    s4_s[16:16 + n_ext, :] = s2_s[16:16 + n_ext, :] + s2_s[14:14 + n_ext, :]
    s8_s[16:16 + n_ext, :] = s4_s[16:16 + n_ext, :] + s4_s[12:12 + n_ext, :]
    u_cur = u_s[POOL_HDR:POOL_HDR + lb, :]
    w2 = s2_s[POOL_HDR:POOL_HDR + lb, :]
    w4 = s4_s[POOL_HDR:POOL_HDR + lb, :]
    w8 = s8_s[POOL_HDR:POOL_HDR + lb, :]
    w16 = w8 + s8_s[POOL_HDR - 8:POOL_HDR - 8 + lb, :]
    lane_p = lax.broadcasted_iota(jnp.int32, (lb, P_W), 1)
    wsum = jnp.where(lane_p < P_GW, w2, jnp.where(lane_p < 2 * P_GW, w4, jnp.where(lane_p < 3 * P_GW, w8, w16)))
    if pos0 >= POOL_WINDOWS[-1] - 1:
        inv = jnp.where(lane_p < P_GW, 0.5, jnp.where(lane_p < 2 * P_GW, 0.25,
                                                       jnp.where(lane_p < 3 * P_GW, 0.125, 0.0625)))
        mean = wsum * inv
    else:
        row_p = lax.broadcasted_iota(jnp.int32, (lb, P_W), 0)
        posn = (row_p + (pos0 + 1 - pad + t * tv)).astype(F32)
        wlen = jnp.where(lane_p < P_GW, 2.0, jnp.where(lane_p < 2 * P_GW, 4.0,
                                                        jnp.where(lane_p < 3 * P_GW, 8.0, 16.0)))
        mean = wsum / jnp.maximum(jnp.minimum(wlen, posn), 1.0)
    pooled = (mean - u_cur).astype(BF16)
    y_s[:, OFF_YP:OFF_YH] = (_dot(pooled, pw_ref[...]) * ps_ref[...]).astype(BF16)
    pool_tail = u_s[lb:lb + POOL_HDR, :]
    u_s[0:POOL_HDR, :] = pool_tail

    lbl = lbl_ref[...]
    e = jnp.exp(lbl - jnp.max(lbl, axis=0, keepdims=True))
    p = e / jnp.sum(e, axis=0, keepdims=True)
    lbv = jnp.sum(p[0:layer + 1, :], axis=0, keepdims=True) - p[0:1, :]
    hq = proj_s[:, OFF_HQ:OFF_HF]
    hf = proj_s[:, OFF_HF:OFF_HI]
    fg = lbv + (1.0 - lbv) * _sigmoid(hf)
    kh = 1.0 - fg
    lg = jnp.log(fg)
    if pad:
        row_h = lax.broadcasted_iota(jnp.int32, (lb, HW), 0)
        kh = jnp.where(row_h >= pad, kh, 0.0)
        lg = jnp.where(row_h >= pad, lg, 0.0)
    proj_s[:, OFF_HQ:OFF_HF] = hq * _sigmoid(hq)
    proj_s[:, OFF_HF:OFF_HI] = lg
    hk_s[...] = kh

    r64 = lax.broadcasted_iota(jnp.int32, (H_CHUNK, H_CHUNK), 0)
    c64 = lax.broadcasted_iota(jnp.int32, (H_CHUNK, H_CHUNK), 1)
    tril_h = jnp.where(r64 >= c64, 1.0, 0.0).astype(BF16)
    bdiff = r64 // H_SUB - c64 // H_SUB
    mask_intra = (bdiff == 0) & (r64 >= c64)
    n_sub = H_CHUNK // H_SUB

    def h_chunk(c, carry):
        r0 = pl.multiple_of(c * H_CHUNK, H_CHUNK)
        rows = pl.ds(r0, H_CHUNK)
        g = _cumsum_rows(tril_h, proj_s[rows, OFF_HF:OFF_HI])
        qh = proj_s[rows, OFF_HQ:OFF_HF]
        khc = hk_s[rows, :]
        bnd = [jnp.zeros((1, HW), F32)] + [g[(j + 1) * H_SUB - 1:(j + 1) * H_SUB, :] for j in range(n_sub)]
        g_start = jnp.concatenate([_bcast_rows(bnd[j], H_SUB) for j in range(n_sub)], axis=0)
        g_end = jnp.concatenate([_bcast_rows(bnd[j + 1], H_SUB) for j in range(n_sub)], axis=0)
        g_last = bnd[n_sub]
        qt = qh * jnp.exp(g - g_start)
        khat = khc * jnp.exp(g_end - g)
        kbar = khc * jnp.exp(g_start - g)
        qg = qh * jnp.exp(g)
        kend = khc * jnp.exp(g_last - g)
        dsub = [jnp.exp(bnd[j + 1] - bnd[j]) for j in range(n_sub - 1)]
        ones = jnp.ones((H_SUB, HW), F32)
        qlev = [qt]
        for d in range(1, n_sub - 1):
            fac = jnp.concatenate([ones] * d + [_bcast_rows(dsub[j - d], H_SUB) for j in range(d, n_sub)], axis=0)
            qlev.append(qlev[-1] * fac)
        for h in range(NH):
            sl = slice(h * HP, (h + 1) * HP)
            lhs = jnp.concatenate([ql[:, sl] for ql in qlev], axis=0).astype(BF16)
            inter = _dot_nt(lhs, khat[:, sl].astype(BF16))
            intra = _dot_nt(qt[:, sl].astype(BF16), kbar[:, sl].astype(BF16))
            att = jnp.where(mask_intra, intra, 0.0)
            for d in range(n_sub - 1):
                att = att + jnp.where(bdiff == d + 1, inter[d * H_CHUNK:(d + 1) * H_CHUNK, :], 0.0)
            iv = proj_s[rows, OFF_HI + h * HP:OFF_HI + (h + 1) * HP].astype(BF16)
            st = st_s[h]
            o = _dot(att.astype(BF16), iv) + _dot_nt(qg[:, sl].astype(BF16), st.astype(BF16))
            st_s[h] = st * jnp.exp(g_last[:, sl]) + _dot_tn(iv, kend[:, sl].astype(BF16))
            msq = jnp.sum(o * o, axis=1, keepdims=True) * (1.0 / DH)
            on = o * lax.rsqrt(msq + EPS) * hng_ref[:, sl]
            hg = proj_s[rows, OFF_HG + h * HP:OFF_HG + (h + 1) * HP]
            y_s[rows, OFF_YH + h * HP:OFF_YH + (h + 1) * HP] = (on * (hg * _sigmoid(hg))).astype(BF16)
        return carry

    lax.fori_loop(0, lb // H_CHUNK, h_chunk, 0)

    out = _dot(y_s[...], wout_ref[...])
    xo_ref[...] = x + out[pad:lb, :]

    @pl.when(t == n_t - 1)
    def _final():
        cf_ref[...] = c_s[...]
        mf_ref[...] = m_s[...]
        convf_ref[...] = conv_tail
        poolf_ref[...] = pool_tail[16:32, :]
        sf_ref[...] = st_s[...]


def _mixer_call(xall, states, shared_init, w, layer, *, row_off, nb, seq, tv, pos0, name):
    lb = max(tv, M_CHUNK)
    n_t = seq // tv
    blk0 = row_off // tv
    c0, m0, conv0, pool0, s0 = states

    def st_map(b, t):
        return (0 if shared_init else b, 0, 0, 0)

    def st_map3(b, t):
        return (0 if shared_init else b, 0, 0)

    def const2(b, t):
        return (0, 0)

    in_specs = [
        pl.BlockSpec((tv, D_MODEL), lambda b, t: (blk0 + b * n_t + t, 0)),
        pl.BlockSpec((None, NH, HP, HP), st_map),
        pl.BlockSpec((None, 8, HP), st_map3),
        pl.BlockSpec((None, CONV_HDR, 2 * HW), st_map3),
        pl.BlockSpec((None, 16, P_W), st_map3),
        pl.BlockSpec((None, NH, HP, HP), st_map),
        pl.BlockSpec((1, D_MODEL), const2),
        pl.BlockSpec((D_MODEL, D_INP), const2),
        pl.BlockSpec((1, HP), const2),
        pl.BlockSpec((4, 2 * HW), const2),
        pl.BlockSpec((1, HW), const2),
        pl.BlockSpec((P_W, P_W), const2),
        pl.BlockSpec((1, P_W), const2),
        pl.BlockSpec((DEPTH, HW), const2),
        pl.BlockSpec((1, HW), const2),
        pl.BlockSpec((D_MIXP, D_MODEL), const2),
    ]
    out_specs = [
        pl.BlockSpec((tv, D_MODEL), lambda b, t: (blk0 + b * n_t + t, 0)),
        pl.BlockSpec((None, NH, HP, HP), lambda b, t: (b, 0, 0, 0)),
        pl.BlockSpec((None, 8, HP), lambda b, t: (b, 0, 0)),
        pl.BlockSpec((None, CONV_HDR, 2 * HW), lambda b, t: (b, 0, 0)),
        pl.BlockSpec((None, 16, P_W), lambda b, t: (b, 0, 0)),
        pl.BlockSpec((None, NH, HP, HP), lambda b, t: (b, 0, 0, 0)),
    ]
    out_shape = [
        jax.ShapeDtypeStruct(xall.shape, F32),
        jax.ShapeDtypeStruct((nb, NH, HP, HP), F32),
        jax.ShapeDtypeStruct((nb, 8, HP), F32),
        jax.ShapeDtypeStruct((nb, CONV_HDR, 2 * HW), F32),
        jax.ShapeDtypeStruct((nb, 16, P_W), F32),
        jax.ShapeDtypeStruct((nb, NH, HP, HP), F32),
    ]
    scratch = [
        pltpu.VMEM((lb, D_INP), F32),
        pltpu.VMEM((CONV_HDR + lb, 2 * HW), F32),
        pltpu.VMEM((POOL_HDR + lb, P_W), F32),
        pltpu.VMEM((POOL_HDR + lb, P_W), F32),
        pltpu.VMEM((POOL_HDR + lb, P_W), F32),
        pltpu.VMEM((POOL_HDR + lb, P_W), F32),
        pltpu.VMEM((lb, HW), F32),
        pltpu.VMEM((lb, HP), F32),
        pltpu.VMEM((lb, D_MIXP), BF16),
        pltpu.VMEM((NH, HP, HP), F32),
        pltpu.VMEM((8, HP), F32),
        pltpu.VMEM((NH, HP, HP), F32),
    ]
    kern = functools.partial(_mixer_kernel, layer=layer, tv=tv, lb=lb, pos0=pos0, n_t=n_t)
    outs = pl.pallas_call(
        kern,
        grid=(nb, n_t),
        in_specs=in_specs,
        out_specs=out_specs,
        out_shape=out_shape,
        scratch_shapes=scratch,
        input_output_aliases={0: 0},
        compiler_params=pltpu.CompilerParams(dimension_semantics=("arbitrary", "arbitrary"),
                                             vmem_limit_bytes=VMEM_LIMIT),
        name=name,
    )(xall, c0, m0, conv0, pool0, s0,
      w["g1"][layer], w["w_in"][layer], w["gbias"][layer], w["conv_w"][layer], w["mnorm"][layer],
      w["pool_w"][layer], w["pool_scale"][layer], w["lb_logits"], w["hnorm"][layer], w["w_out"][layer])
    return outs[0], tuple(outs[1:])


def _ffn_kernel(x_ref, g_ref, w1_ref, w3_ref, w2_ref, o_ref, hn_s, *, n_f):
    f = pl.program_id(1)

    @pl.when(f == 0)
    def _start():
        x = x_ref[...]
        ms = jnp.mean(x * x, axis=-1, keepdims=True)
        hn_s[...] = (x * lax.rsqrt(ms + EPS) * g_ref[...]).astype(BF16)
        o_ref[...] = x

    hn = hn_s[...]
    h1 = _dot(hn, w1_ref[...])
    h3 = _dot(hn, w3_ref[...])
    a = (h1 * _sigmoid(h1) * h3).astype(BF16)
    o_ref[...] += _dot(a, w2_ref[...])


def _ffn_call(xall, g, w1, w3, w2, *, tm, tf, name):
    n = xall.shape[0]
    n_f = D_FF // tf
    return pl.pallas_call(
        functools.partial(_ffn_kernel, n_f=n_f),
        grid=(n // tm, n_f),
        in_specs=[
            pl.BlockSpec((tm, D_MODEL), lambda i, f: (i, 0)),
            pl.BlockSpec((1, D_MODEL), lambda i, f: (0, 0)),
            pl.BlockSpec((D_MODEL, tf), lambda i, f: (0, f)),
            pl.BlockSpec((D_MODEL, tf), lambda i, f: (0, f)),
            pl.BlockSpec((tf, D_MODEL), lambda i, f: (f, 0)),
        ],
        out_specs=pl.BlockSpec((tm, D_MODEL), lambda i, f: (i, 0)),
        out_shape=jax.ShapeDtypeStruct(xall.shape, F32),
        scratch_shapes=[pltpu.VMEM((tm, D_MODEL), BF16)],
        input_output_aliases={0: 0},
        compiler_params=pltpu.CompilerParams(dimension_semantics=("arbitrary", "arbitrary"),
                                             vmem_limit_bytes=VMEM_LIMIT),
        name=name,
    )(xall, g, w1, w3, w2)


ROW_TILE = 256


def _moe_kernel(x_ref, g_ref, wr_ref, w1_ref, w3_ref, w2_ref, fg_ref, o_ref,
                hn_s, xe_s, ye_s, rank_s, comb_s, rank_t_s, cnt_s, *, tb, n_f, final_norm):
    e = pl.program_id(1)
    f = pl.program_id(2)

    @pl.when((e == 0) & (f == 0))
    def _route():
        x = x_ref[...]
        ms = jnp.mean(x * x, axis=-1, keepdims=True)
        hn = x * lax.rsqrt(ms + EPS) * g_ref[...]
        hn_s[...] = hn.astype(BF16)
        o_ref[...] = x
        logits = jnp.dot(hn, wr_ref[...], preferred_element_type=F32, precision=lax.Precision.HIGHEST)
        lane = lax.broadcasted_iota(jnp.int32, (tb, HP), 1).astype(F32)
        lg = jnp.where(lane < N_EXPERTS, logits, NEG)
        v1 = jnp.max(lg, axis=1, keepdims=True)
        i1 = jnp.min(jnp.where(lg == v1, lane, float(HP)), axis=1, keepdims=True)
        mask1 = lane == i1
        lg2 = jnp.where(mask1, NEG, lg)
        v2 = jnp.max(lg2, axis=1, keepdims=True)
        i2 = jnp.min(jnp.where(lg2 == v2, lane, float(HP)), axis=1, keepdims=True)
        mask2 = lane == i2
        ex = jnp.exp(v2 - v1)
        ga = 1.0 / (1.0 + ex)
        comb = jnp.where(mask1, ga, 0.0) + jnp.where(mask2, ex * ga, 0.0)
        sel = mask1 | mask2
        rr = lax.broadcasted_iota(jnp.int32, (tb, tb), 0)
        cc = lax.broadcasted_iota(jnp.int32, (tb, tb), 1)
        tril_strict = jnp.where(rr > cc, 1.0, 0.0).astype(BF16)
        selb = jnp.where(sel, 1.0, 0.0)
        rank = jnp.where(sel, _dot(tril_strict, selb.astype(BF16)), -1.0)
        comb_s[...] = comb
        rank_s[...] = rank
        rank_t_s[...] = rank.T[0:N_EXPERTS, :]
        cnt = jnp.sum(selb, axis=0, keepdims=True).astype(jnp.int32)
        for j in range(N_EXPERTS):
            cnt_s[j] = cnt[0, j]

    n_rows = cnt_s[e]
    n_tiles = (n_rows + ROW_TILE - 1) // ROW_TILE

    @pl.when(f == 0)
    def _gather():
        rrow = rank_t_s[pl.ds(e, 1), :]

        def body(i, carry):
            r0 = pl.multiple_of(i * ROW_TILE, ROW_TILE)
            slot = (lax.broadcasted_iota(jnp.int32, (ROW_TILE, tb), 0) + r0).astype(F32)
            onehot = jnp.where(rrow == slot, 1.0, 0.0).astype(BF16)
            xe_s[pl.ds(r0, ROW_TILE), :] = _dot(onehot, hn_s[...]).astype(BF16)
            return carry

        lax.fori_loop(0, n_tiles, body, 0)

    def ffn_body(i, carry):
        r0 = pl.multiple_of(i * ROW_TILE, ROW_TILE)
        rows = pl.ds(r0, ROW_TILE)
        xe = xe_s[rows, :]
        h1 = _dot(xe, w1_ref[...])
        h3 = _dot(xe, w3_ref[...])
        a = (h1 * _sigmoid(h1) * h3).astype(BF16)
        part = _dot(a, w2_ref[...])

        @pl.when(f == 0)
        def _set():
            ye_s[rows, :] = part

        @pl.when(f != 0)
        def _add():
            ye_s[rows, :] += part

        return carry

    lax.fori_loop(0, n_tiles, ffn_body, 0)

    @pl.when(f == n_f - 1)
    def _scatter():
        lane = lax.broadcasted_iota(jnp.int32, (tb, HP), 1)
        pick = lane == e
        rcol = jnp.sum(jnp.where(pick, rank_s[...], 0.0), axis=1, keepdims=True)
        gcol = jnp.sum(jnp.where(pick, comb_s[...], 0.0), axis=1, keepdims=True)

        def body(i, carry):
            r0 = pl.multiple_of(i * ROW_TILE, ROW_TILE)
            slot = (lax.broadcasted_iota(jnp.int32, (tb, ROW_TILE), 1) + r0).astype(F32)
            onehot = jnp.where(rcol == slot, 1.0, 0.0).astype(BF16)
            o_ref[...] += gcol * _dot(onehot, ye_s[pl.ds(r0, ROW_TILE), :].astype(BF16))
            return carry

        lax.fori_loop(0, n_tiles, body, 0)

    if final_norm:
        @pl.when((e == N_EXPERTS - 1) & (f == n_f - 1))
        def _norm():
            y = o_ref[...]
            ms = jnp.mean(y * y, axis=-1, keepdims=True)
            o_ref[...] = y * lax.rsqrt(ms + EPS) * fg_ref[...]


def _moe_call(xall, g, wr, w1, w3, w2, fg, *, tb, tf, final_norm, name):
    n = xall.shape[0]
    n_f = D_FF // tf
    return pl.pallas_call(
        functools.partial(_moe_kernel, tb=tb, n_f=n_f, final_norm=final_norm),
        grid=(n // tb, N_EXPERTS, n_f),
        in_specs=[
            pl.BlockSpec((tb, D_MODEL), lambda i, e, f: (i, 0)),
            pl.BlockSpec((1, D_MODEL), lambda i, e, f: (0, 0)),
            pl.BlockSpec((D_MODEL, HP), lambda i, e, f: (0, 0)),
            pl.BlockSpec((None, D_MODEL, tf), lambda i, e, f: (e, 0, f)),
            pl.BlockSpec((None, D_MODEL, tf), lambda i, e, f: (e, 0, f)),
            pl.BlockSpec((None, tf, D_MODEL), lambda i, e, f: (e, f, 0)),
            pl.BlockSpec((1, D_MODEL), lambda i, e, f: (0, 0)),
        ],
        out_specs=pl.BlockSpec((tb, D_MODEL), lambda i, e, f: (i, 0)),
        out_shape=jax.ShapeDtypeStruct(xall.shape, F32),
        scratch_shapes=[
            pltpu.VMEM((tb, D_MODEL), BF16),
            pltpu.VMEM((tb, D_MODEL), BF16),
            pltpu.VMEM((tb, D_MODEL), F32),
            pltpu.VMEM((tb, HP), F32),
            pltpu.VMEM((tb, HP), F32),
            pltpu.VMEM((N_EXPERTS, tb), F32),
            pltpu.SMEM((N_EXPERTS,), jnp.int32),
        ],
        input_output_aliases={0: 0},
        compiler_params=pltpu.CompilerParams(dimension_semantics=("arbitrary", "arbitrary", "arbitrary"),
                                             vmem_limit_bytes=VMEM_LIMIT),
        name=name,
    )(xall, g, wr, w1, w3, w2, fg)


def _pad_heads(a, axis):
    axis = axis % a.ndim
    shp = a.shape
    a = a.reshape(shp[:axis] + (NH, DH) + shp[axis + 1:])
    padw = [(0, 0)] * a.ndim
    padw[axis + 1] = (0, HP - DH)
    a = jnp.pad(a, padw)
    return a.reshape(shp[:axis] + (HW,) + shp[axis + 1:])


def _unpad_heads(a, axis):
    axis = axis % a.ndim
    shp = a.shape
    a = a.reshape(shp[:axis] + (NH, HP) + shp[axis + 1:])
    a = lax.slice_in_dim(a, 0, DH, axis=axis + 1)
    return a.reshape(shp[:axis] + (NH * DH,) + shp[axis + 1:])


def _prep_weights(norm1_g, w_in, b_igate, b_fgate, conv_w, mlstm_norm_g, pool_w, pool_scale, lb_logits,
                  hgrn_norm_g, w_out):
    m_w, h_kw = NH * DH, HW
    widths = (m_w, m_w, m_w, m_w, NH, NH, P_W, h_kw, h_kw, m_w, m_w)
    pts, acc = [], 0
    for wd in widths[:-1]:
        acc += wd
        pts.append(acc)
    mq, mk, mv, mo, mi, mf, pu, hq, hf, hi, hg = jnp.split(w_in, pts, axis=-1)
    gates = jnp.pad(jnp.concatenate([mi, mf], axis=-1), ((0, 0), (0, 0), (0, HP - 2 * NH)))
    w_in_p = jnp.concatenate([_pad_heads(mq, -1), _pad_heads(mk, -1), _pad_heads(mv, -1), _pad_heads(mo, -1),
                              gates, pu, hq, hf, _pad_heads(hi, -1), _pad_heads(hg, -1)], axis=-1).astype(BF16)
    w_out_p = jnp.concatenate([_pad_heads(w_out[:, 0:m_w], 1), w_out[:, m_w:m_w + P_W],
                               _pad_heads(w_out[:, m_w + P_W:], 1)], axis=1).astype(BF16)
    gbias = jnp.pad(jnp.concatenate([b_igate, b_fgate], axis=-1), ((0, 0), (0, HP - 2 * NH)))[:, None, :]
    conv_p = jnp.concatenate([_pad_heads(conv_w[..., 0:m_w], -1), _pad_heads(conv_w[..., m_w:], -1)], axis=-1)
    eye = jnp.eye(len(POOL_WINDOWS), dtype=F32)
    pool_bd = jnp.einsum('lgce,gh->lgche', pool_w, eye).reshape(DEPTH, P_W, P_W).astype(BF16)
    return {
        "g1": norm1_g[:, None, :], "w_in": w_in_p, "gbias": gbias, "conv_w": conv_p,
        "mnorm": _pad_heads(mlstm_norm_g, -1)[:, None, :], "pool_w": pool_bd,
        "pool_scale": pool_scale[:, None, :], "lb_logits": lb_logits,
        "hnorm": _pad_heads(hgrn_norm_g, -1)[:, None, :], "w_out": w_out_p,
    }


def _states_to_kernel(C, n, m, conv, pool, S):
    nb = C.shape[0]
    caug = jnp.concatenate([C, n[..., None]], axis=-1)
    caug = jnp.pad(caug, ((0, 0), (0, 0), (0, HP - DH), (0, HP - DH - 1)))
    mk = jnp.pad(jnp.broadcast_to(m[:, :, None], (nb, NH, HP)), ((0, 0), (0, 8 - NH), (0, 0)))
    m_w = NH * DH
    convk = jnp.concatenate([_pad_heads(conv[..., 0:m_w], -1), _pad_heads(conv[..., m_w:], -1)], axis=-1)
    convk = jnp.pad(convk, ((0, 0), (CONV_HDR - 3, 0), (0, 0)))
    poolk = jnp.pad(pool, ((0, 0), (1, 0), (0, 0)))
    sk = jnp.pad(jnp.swapaxes(S, -1, -2), ((0, 0), (0, 0), (0, HP - DH), (0, 0)))
    return caug, mk, convk, poolk, sk


def _states_from_kernel(st):
    caug, mk, convk, poolk, sk = st
    m_w = NH * DH
    C = caug[:, :, 0:DH, 0:DH]
    n = caug[:, :, 0:DH, DH]
    m = mk[:, 0:NH, 0]
    conv = convk[:, CONV_HDR - 3:, :]
    conv = jnp.concatenate([_unpad_heads(conv[..., 0:HW], -1), _unpad_heads(conv[..., HW:], -1)], axis=-1)
    pool = poolk[:, 1:, :]
    S = jnp.swapaxes(sk[:, :, 0:DH, :], -1, -2)
    return C, n, m, conv, pool, S


def _pick_tile(n, candidates):
    for c in candidates:
        if n % c == 0:
            return c
    raise ValueError(f"no row tile for {n}")


def kernel(x_prompt, x_sample, state_mlstm_C, state_mlstm_n, state_mlstm_m, state_mlstm_conv, state_pool,
           state_hgrn, meta_tokens, norm1_g, norm2_g, final_g, w_in, b_igate, b_fgate, conv_w, mlstm_norm_g,
           pool_w, pool_scale, lb_logits, hgrn_norm_g, w_out, ffn_w1, ffn_w3, ffn_w2, router_w, moe_w1,
           moe_w3, moe_w2):
    B, T, _ = x_prompt.shape
    SB, ST, _ = x_sample.shape
    w = _prep_weights(norm1_g, w_in, b_igate, b_fgate, conv_w, mlstm_norm_g, pool_w, pool_scale, lb_logits,
                      hgrn_norm_g, w_out)
    ffn_w1b, ffn_w3b, ffn_w2b = ffn_w1.astype(BF16), ffn_w3.astype(BF16), ffn_w2.astype(BF16)
    moe_w1b, moe_w3b, moe_w2b = moe_w1.astype(BF16), moe_w3.astype(BF16), moe_w2.astype(BF16)
    router_p = jnp.pad(router_w, ((0, 0), (0, 0), (0, HP - N_EXPERTS)))

    n_main = B * T
    off_s = n_main
    off_m = off_s + SB * ST
    n_tok = off_m + META_TOKENS
    tile = 1280 if n_main >= 16384 else 256
    n_pad = -(-n_tok // tile) * tile
    xall = jnp.concatenate([x_prompt.reshape(n_main, D_MODEL), x_sample.reshape(SB * ST, D_MODEL),
                            meta_tokens.astype(F32), jnp.zeros((n_pad - n_tok, D_MODEL), F32)], axis=0)
    tv_main = _pick_tile(T, (256, 128))

    zero_states = (jnp.zeros((1, NH, HP, HP), F32), jnp.zeros((1, 8, HP), F32),
                   jnp.zeros((1, CONV_HDR, 2 * HW), F32), jnp.zeros((1, 16, P_W), F32),
                   jnp.zeros((1, NH, HP, HP), F32))
    p_states, s_states = [], []
    for l in range(DEPTH):
        xall, st_meta = _mixer_call(xall, zero_states, True, w, l, row_off=off_m, nb=1, seq=META_TOKENS,
                                    tv=META_TOKENS, pos0=0, name=f"mixer_meta_{l}")
        xall, st_p = _mixer_call(xall, st_meta, True, w, l, row_off=0, nb=B, seq=T, tv=tv_main,
                                 pos0=META_TOKENS, name=f"mixer_prompt_{l}")
        st_in = _states_to_kernel(state_mlstm_C[l], state_mlstm_n[l], state_mlstm_m[l], state_mlstm_conv[l],
                                  state_pool[l], state_hgrn[l])
        xall, st_s = _mixer_call(xall, st_in, False, w, l, row_off=off_s, nb=SB, seq=ST, tv=ST,
                                 pos0=META_TOKENS + PAST_LEN, name=f"mixer_sample_{l}")
        p_states.append(_states_from_kernel(st_p))
        s_states.append(_states_from_kernel(st_s))
        i = l // 2
        if l % 2 == 0:
            xall = _ffn_call(xall, norm2_g[l][None, :], ffn_w1b[i], ffn_w3b[i], ffn_w2b[i],
                             tm=tile, tf=512, name=f"ffn_{l}")
        else:
            xall = _moe_call(xall, norm2_g[l][None, :], router_p[i], moe_w1b[i], moe_w3b[i], moe_w2b[i],
                             final_g[None, :], tb=tile, tf=512, final_norm=(l == DEPTH - 1), name=f"moe_{l}")
    y_prompt = xall[0:n_main].reshape(B, T, D_MODEL)
    y_sample = xall[off_s:off_m].reshape(SB, ST, D_MODEL)
    p_out = tuple(jnp.stack([s[j] for s in p_states], axis=0) for j in range(6))
    s_out = tuple(jnp.stack([s[j] for s in s_states], axis=0) for j in range(6))
    return (y_prompt, y_sample) + p_out + s_out
```

```python
import functools

import jax
import jax.numpy as jnp
from jax import lax
from jax.experimental import pallas as pl
from jax.experimental.pallas import tpu as pltpu

F32 = jnp.float32
BF16 = jnp.bfloat16

D_MODEL = 1024
DEPTH = 4
META_TOKENS = 16
PAST_LEN = 4096
EPS = 1e-6
NH = 4
DH = 96
HP = 128
HW = NH * HP
P_W = 256
P_GW = 64
POOL_WINDOWS = (2, 4, 8, 16)
D_FF = 3584
N_EXPERTS = 8
NEG = -1e30

OFF_MQ, OFF_MK, OFF_MV, OFF_MO = 0, 512, 1024, 1536
OFF_G = 2048
OFF_PU = 2176
OFF_HQ, OFF_HF, OFF_HI, OFF_HG = 2432, 2944, 3456, 3968
D_INP = 4480
OFF_YM, OFF_YP, OFF_YH = 0, 512, 768
D_MIXP = 1280

M_CHUNK = 128
H_CHUNK = 64
H_SUB = 16
CONV_HDR = 8
POOL_HDR = 32

TAIL = 256
PRECISE_LAYERS = 2

VMEM_LIMIT = 60 * 1024 * 1024


def _sigmoid(x):
    return 1.0 / (1.0 + jnp.exp(-x))


def _dot(a, b, precision=None):
    return jnp.dot(a, b, preferred_element_type=F32, precision=precision)


def _dot_nt(a, b, precision=None):
    return lax.dot_general(a, b, (((1,), (1,)), ((), ())), preferred_element_type=F32, precision=precision)


def _dot_tn(a, b, precision=None):
    return lax.dot_general(a, b, (((0,), (0,)), ((), ())), preferred_element_type=F32, precision=precision)


def _cumsum_rows(tril_bf, x, precise=False):
    if precise:
        return _dot(tril_bf.astype(F32), x, precision=lax.Precision.HIGHEST)
    n = x.shape[1]
    hi = x.astype(BF16)
    lo = (x - hi.astype(F32)).astype(BF16)
    both = _dot(tril_bf, jnp.concatenate([hi, lo], axis=1))
    return both[:, :n] + both[:, n:]


def _bcast_rows(row, n):
    return jnp.broadcast_to(row, (n, row.shape[1]))


def _mixer_kernel(x_ref, c0_ref, m0_ref, conv0_ref, pool0_ref, s0_ref,
                  g1_ref, win_ref, gb_ref, cw_ref, mng_ref, pw_ref, ps_ref, lbl_ref, hng_ref, wout_ref,
                  xo_ref, cf_ref, mf_ref, convf_ref, poolf_ref, sf_ref,
                  proj_s, qk_s, u_s, s2_s, s4_s, s8_s, hk_s, gate_s, y_s, c_s, m_s, st_s,
                  *, layer, tv, lb, pos0, n_t, precise):
    pad = lb - tv
    t = pl.program_id(1)
    prec = lax.Precision.HIGHEST if precise else None
    dot = functools.partial(_dot, precision=prec)
    dot_nt = functools.partial(_dot_nt, precision=prec)
    dot_tn = functools.partial(_dot_tn, precision=prec)

    def mm(a):
        return a if precise else a.astype(BF16)

    @pl.when(t == 0)
    def _init():
        c_s[...] = c0_ref[...]
        m_s[...] = m0_ref[...]
        st_s[...] = s0_ref[...]
        qk_s[...] = jnp.zeros(qk_s.shape, F32)
        qk_s[pad:pad + CONV_HDR, :] = conv0_ref[...]
        u_s[...] = jnp.zeros(u_s.shape, F32)
        u_s[pad + 16:pad + 32, :] = pool0_ref[...]
        s2_s[0:POOL_HDR, :] = jnp.zeros((POOL_HDR, P_W), F32)
        s4_s[0:POOL_HDR, :] = jnp.zeros((POOL_HDR, P_W), F32)
        s8_s[0:POOL_HDR, :] = jnp.zeros((POOL_HDR, P_W), F32)
        if pad:
            proj_s[0:pad, :] = jnp.zeros((pad, D_INP), F32)

    x = x_ref[...]
    ms = jnp.mean(x * x, axis=-1, keepdims=True)
    hn = mm(x * lax.rsqrt(ms + EPS) * g1_ref[...])
    qk_s[CONV_HDR + pad:CONV_HDR + lb, :] = dot(hn, win_ref[:, 0:OFF_MV])
    proj_s[pad:lb, OFF_MV:OFF_PU] = dot(hn, win_ref[:, OFF_MV:OFF_PU])
    u_s[POOL_HDR + pad:POOL_HDR + lb, :] = dot(hn, win_ref[:, OFF_PU:OFF_HQ])
    proj_s[pad:lb, OFF_HQ:D_INP] = dot(hn, win_ref[:, OFF_HQ:D_INP])

    row = lax.broadcasted_iota(jnp.int32, (lb, HP), 0)
    lane = lax.broadcasted_iota(jnp.int32, (lb, HP), 1)

    acc = qk_s[5:5 + lb, :] * cw_ref[0:1, :]
    for j in range(1, 4):
        acc = acc + qk_s[5 + j:5 + j + lb, :] * cw_ref[j:j + 1, :]
    qk = acc * _sigmoid(acc)
    proj_s[:, OFF_MQ:OFF_MK] = qk[:, 0:HW]
    proj_s[:, OFF_MK:OFF_MV] = qk[:, HW:2 * HW] * (DH ** -0.5)
    conv_tail = qk_s[lb:lb + CONV_HDR, :]
    qk_s[0:CONV_HDR, :] = conv_tail

    gpre = proj_s[:, OFF_G:OFF_G + HP] + gb_ref[...]
    lsig = jnp.minimum(gpre, 0.0) - jnp.log(1.0 + jnp.exp(-jnp.abs(gpre)))
    gates = jnp.where(lane < NH, gpre, jnp.where(lane < 2 * NH, lsig, 0.0))
    if pad:
        gates = jnp.where(row >= pad, gates, jnp.where(lane < NH, NEG, 0.0))
    gate_s[...] = gates

    rr = lax.broadcasted_iota(jnp.int32, (M_CHUNK, M_CHUNK), 0)
    cc = lax.broadcasted_iota(jnp.int32, (M_CHUNK, M_CHUNK), 1)
    causal = rr >= cc
    tril_m = jnp.where(causal, 1.0, 0.0).astype(BF16)
    lane_c = lax.broadcasted_iota(jnp.int32, (M_CHUNK, HP), 1)

    for c in range(lb // M_CHUNK):
        rows = slice(c * M_CHUNK, (c + 1) * M_CHUNK)
        gt = gate_s[rows, :]
        bcum = _cumsum_rows(tril_m, jnp.where(lane_c >= NH, gt, 0.0), precise)
        cg = gt - pltpu.roll(bcum, HP - NH, axis=1)
        cg_t = cg.T
        for h in range(NH):
            q = proj_s[rows, OFF_MQ + h * HP:OFF_MQ + (h + 1) * HP]
            k = proj_s[rows, OFF_MK + h * HP:OFF_MK + (h + 1) * HP]
            v = proj_s[rows, OFF_MV + h * HP:OFF_MV + (h + 1) * HP]
            c_row = cg_t[h:h + 1, :]
            c_col = cg[:, h:h + 1]
            b_col = bcum[:, NH + h:NH + h + 1]
            m_prev = m_s[h:h + 1, 0:1]
            mx = jnp.maximum(jnp.max(jnp.where(causal, c_row, NEG), axis=1, keepdims=True), m_prev)
            w = jnp.exp(jnp.where(causal, c_row - mx, NEG))
            w_int = jnp.exp(m_prev - mx)
            mx_last = mx[M_CHUNK - 1:M_CHUNK, :]
            s = dot_nt(mm(q), mm(k)) * w
            vaug = jnp.where(lane_c == DH, 1.0, v)
            caug = c_s[h]
            lhs = mm(jnp.concatenate([s, q * w_int], axis=1))
            rhs = mm(jnp.concatenate([vaug, caug], axis=0))
            nd = dot(lhs, rhs)
            den = nd[:, DH:DH + 1]
            rden = 1.0 / jnp.maximum(jnp.abs(den), jnp.exp(-(b_col + mx)))
            wl = jnp.exp(c_col - mx_last)
            decay = jnp.exp(m_prev - mx_last)
            c_s[h] = decay * caug + dot_tn(mm(k * wl), mm(vaug))
            m_s[h:h + 1, :] = jnp.broadcast_to(b_col[M_CHUNK - 1:M_CHUNK, :] + mx_last, (1, HP))
            mo = proj_s[rows, OFF_MO + h * HP:OFF_MO + (h + 1) * HP]
            z = jnp.where(lane_c < DH, nd * _sigmoid(mo), 0.0)
            ssq = jnp.sum(z * z, axis=1, keepdims=True) * (1.0 / DH)
            fac = rden * lax.rsqrt(rden * rden * ssq + EPS)
            y_s[rows, OFF_YM + h * HP:OFF_YM + (h + 1) * HP] = mm(z * fac * mng_ref[:, h * HP:(h + 1) * HP])

    n_ext = lb + 16
    s2_s[16:16 + n_ext, :] = u_s[16:16 + n_ext, :] + u_s[15:15 + n_ext, :]
    s4_s[16:16 + n_ext, :] = s2_s[16:16 + n_ext, :] + s2_s[14:14 + n_ext, :]
    s8_s[16:16 + n_ext, :] = s4_s[16:16 + n_ext, :] + s4_s[12:12 + n_ext, :]
    u_cur = u_s[POOL_HDR:POOL_HDR + lb, :]
    w2 = s2_s[POOL_HDR:POOL_HDR + lb, :]
    w4 = s4_s[POOL_HDR:POOL_HDR + lb, :]
    w8 = s8_s[POOL_HDR:POOL_HDR + lb, :]
    w16 = w8 + s8_s[POOL_HDR - 8:POOL_HDR - 8 + lb, :]
    lane_p = lax.broadcasted_iota(jnp.int32, (lb, P_W), 1)
    wsum = jnp.where(lane_p < P_GW, w2, jnp.where(lane_p < 2 * P_GW, w4, jnp.where(lane_p < 3 * P_GW, w8, w16)))
    if pos0 >= POOL_WINDOWS[-1] - 1:
        inv = jnp.where(lane_p < P_GW, 0.5, jnp.where(lane_p < 2 * P_GW, 0.25,
                                                       jnp.where(lane_p < 3 * P_GW, 0.125, 0.0625)))
        mean = wsum * inv
    else:
        row_p = lax.broadcasted_iota(jnp.int32, (lb, P_W), 0)
        posn = (row_p + (pos0 + 1 - pad + t * tv)).astype(F32)
        wlen = jnp.where(lane_p < P_GW, 2.0, jnp.where(lane_p < 2 * P_GW, 4.0,
                                                        jnp.where(lane_p < 3 * P_GW, 8.0, 16.0)))
        mean = wsum / jnp.maximum(jnp.minimum(wlen, posn), 1.0)
    pooled = mm(mean - u_cur)
    y_s[:, OFF_YP:OFF_YH] = mm(dot(pooled, pw_ref[...]) * ps_ref[...])
    pool_tail = u_s[lb:lb + POOL_HDR, :]
    u_s[0:POOL_HDR, :] = pool_tail

    lbl = lbl_ref[...]
    e = jnp.exp(lbl - jnp.max(lbl, axis=0, keepdims=True))
    p = e / jnp.sum(e, axis=0, keepdims=True)
    lbv = jnp.sum(p[0:layer + 1, :], axis=0, keepdims=True) - p[0:1, :]
    hq = proj_s[:, OFF_HQ:OFF_HF]
    hf = proj_s[:, OFF_HF:OFF_HI]
    fg = lbv + (1.0 - lbv) * _sigmoid(hf)
    kh = 1.0 - fg
    lg = jnp.log(fg)
    if pad:
        row_h = lax.broadcasted_iota(jnp.int32, (lb, HW), 0)
        kh = jnp.where(row_h >= pad, kh, 0.0)
        lg = jnp.where(row_h >= pad, lg, 0.0)
    proj_s[:, OFF_HQ:OFF_HF] = hq * _sigmoid(hq)
    proj_s[:, OFF_HF:OFF_HI] = lg
    hk_s[...] = kh

    r64 = lax.broadcasted_iota(jnp.int32, (H_CHUNK, H_CHUNK), 0)
    c64 = lax.broadcasted_iota(jnp.int32, (H_CHUNK, H_CHUNK), 1)
    tril_h = jnp.where(r64 >= c64, 1.0, 0.0).astype(BF16)
    bdiff = r64 // H_SUB - c64 // H_SUB
    mask_intra = (bdiff == 0) & (r64 >= c64)
    n_sub = H_CHUNK // H_SUB

    for c in range(lb // H_CHUNK):
        rows = slice(c * H_CHUNK, (c + 1) * H_CHUNK)
        g = _cumsum_rows(tril_h, proj_s[rows, OFF_HF:OFF_HI], precise)
        qh = proj_s[rows, OFF_HQ:OFF_HF]
        khc = hk_s[rows, :]
        bnd = [jnp.zeros((1, HW), F32)] + [g[(j + 1) * H_SUB - 1:(j + 1) * H_SUB, :] for j in range(n_sub)]
        g_start = jnp.concatenate([_bcast_rows(bnd[j], H_SUB) for j in range(n_sub)], axis=0)
        g_end = jnp.concatenate([_bcast_rows(bnd[j + 1], H_SUB) for j in range(n_sub)], axis=0)
        g_last = bnd[n_sub]
        qt = qh * jnp.exp(g - g_start)
        khat = khc * jnp.exp(g_end - g)
        kbar = khc * jnp.exp(g_start - g)
        qg = qh * jnp.exp(g)
        kend = khc * jnp.exp(g_last - g)
        dsub = [jnp.exp(bnd[j + 1] - bnd[j]) for j in range(n_sub - 1)]
        ones = jnp.ones((H_SUB, HW), F32)
        qlev = [qt]
        for d in range(1, n_sub - 1):
            fac = jnp.concatenate([ones] * d + [_bcast_rows(dsub[j - d], H_SUB) for j in range(d, n_sub)], axis=0)
            qlev.append(qlev[-1] * fac)
        for h in range(NH):
            sl = slice(h * HP, (h + 1) * HP)
            lhs = mm(jnp.concatenate([ql[:, sl] for ql in qlev], axis=0))
            inter = dot_nt(lhs, mm(khat[:, sl]))
            intra = dot_nt(mm(qt[:, sl]), mm(kbar[:, sl]))
            att = jnp.where(mask_intra, intra, 0.0)
            for d in range(n_sub - 1):
                att = att + jnp.where(bdiff == d + 1, inter[d * H_CHUNK:(d + 1) * H_CHUNK, :], 0.0)
            iv = mm(proj_s[rows, OFF_HI + h * HP:OFF_HI + (h + 1) * HP])
            st = st_s[h]
            o = dot(mm(att), iv) + dot_nt(mm(qg[:, sl]), mm(st))
            st_s[h] = st * jnp.exp(g_last[:, sl]) + dot_tn(iv, mm(kend[:, sl]))
            msq = jnp.sum(o * o, axis=1, keepdims=True) * (1.0 / DH)
            on = o * lax.rsqrt(msq + EPS) * hng_ref[:, sl]
            hg = proj_s[rows, OFF_HG + h * HP:OFF_HG + (h + 1) * HP]
            y_s[rows, OFF_YH + h * HP:OFF_YH + (h + 1) * HP] = mm(on * (hg * _sigmoid(hg)))

    out = dot(y_s[...], wout_ref[...])
    xo_ref[...] = x + out[pad:lb, :]

    @pl.when(t == n_t - 1)
    def _final():
        cf_ref[...] = c_s[...]
        mf_ref[...] = m_s[...]
        convf_ref[...] = conv_tail
        poolf_ref[...] = pool_tail[16:32, :]
        sf_ref[...] = st_s[...]


def _mixer_call(xall, states, shared_init, w, layer, *, row_off, seq_stride, nb, seq, tv, pos0, precise, name):
    lb = max(tv, M_CHUNK)
    n_t = seq // tv
    assert row_off % tv == 0 and seq_stride % tv == 0 and seq % tv == 0
    blk0, blk_stride = row_off // tv, seq_stride // tv
    c0, m0, conv0, pool0, s0 = states

    def x_map(b, t):
        return (blk0 + b * blk_stride + t, 0)

    def st_map(b, t):
        return (0 if shared_init else b, 0, 0, 0)

    def st_map3(b, t):
        return (0 if shared_init else b, 0, 0)

    def const_spec(shape):
        return pl.BlockSpec(shape, lambda b, t: (0,) * len(shape), pipeline_mode=pl.Buffered(1))

    in_specs = [
        pl.BlockSpec((tv, D_MODEL), x_map),
        pl.BlockSpec((None, NH, HP, HP), st_map),
        pl.BlockSpec((None, 8, HP), st_map3),
        pl.BlockSpec((None, CONV_HDR, 2 * HW), st_map3),
        pl.BlockSpec((None, 16, P_W), st_map3),
        pl.BlockSpec((None, NH, HP, HP), st_map),
        const_spec((1, D_MODEL)),
        const_spec((D_MODEL, D_INP)),
        const_spec((1, HP)),
        const_spec((4, 2 * HW)),
        const_spec((1, HW)),
        const_spec((P_W, P_W)),
        const_spec((1, P_W)),
        const_spec((DEPTH, HW)),
        const_spec((1, HW)),
        const_spec((D_MIXP, D_MODEL)),
    ]
    out_specs = [
        pl.BlockSpec((tv, D_MODEL), x_map),
        pl.BlockSpec((None, NH, HP, HP), lambda b, t: (b, 0, 0, 0)),
        pl.BlockSpec((None, 8, HP), lambda b, t: (b, 0, 0)),
        pl.BlockSpec((None, CONV_HDR, 2 * HW), lambda b, t: (b, 0, 0)),
        pl.BlockSpec((None, 16, P_W), lambda b, t: (b, 0, 0)),
        pl.BlockSpec((None, NH, HP, HP), lambda b, t: (b, 0, 0, 0)),
    ]
    out_shape = [
        jax.ShapeDtypeStruct(xall.shape, F32),
        jax.ShapeDtypeStruct((nb, NH, HP, HP), F32),
        jax.ShapeDtypeStruct((nb, 8, HP), F32),
        jax.ShapeDtypeStruct((nb, CONV_HDR, 2 * HW), F32),
        jax.ShapeDtypeStruct((nb, 16, P_W), F32),
        jax.ShapeDtypeStruct((nb, NH, HP, HP), F32),
    ]
    scratch = [
        pltpu.VMEM((lb, D_INP), F32),
        pltpu.VMEM((CONV_HDR + lb, 2 * HW), F32),
        pltpu.VMEM((POOL_HDR + lb, P_W), F32),
        pltpu.VMEM((POOL_HDR + lb, P_W), F32),
        pltpu.VMEM((POOL_HDR + lb, P_W), F32),
        pltpu.VMEM((POOL_HDR + lb, P_W), F32),
        pltpu.VMEM((lb, HW), F32),
        pltpu.VMEM((lb, HP), F32),
        pltpu.VMEM((lb, D_MIXP), F32 if precise else BF16),
        pltpu.VMEM((NH, HP, HP), F32),
        pltpu.VMEM((8, HP), F32),
        pltpu.VMEM((NH, HP, HP), F32),
    ]
    sfx = "_f32" if precise else ""
    kern = functools.partial(_mixer_kernel, layer=layer, tv=tv, lb=lb, pos0=pos0, n_t=n_t, precise=precise)
    outs = pl.pallas_call(
        kern,
        grid=(nb, n_t),
        in_specs=in_specs,
        out_specs=out_specs,
        out_shape=out_shape,
        scratch_shapes=scratch,
        input_output_aliases={0: 0},
        compiler_params=pltpu.CompilerParams(dimension_semantics=("arbitrary", "arbitrary"),
                                             vmem_limit_bytes=VMEM_LIMIT),
        name=name,
    )(xall, c0, m0, conv0, pool0, s0,
      w["g1"][layer], w["w_in" + sfx][layer], w["gbias"][layer], w["conv_w"][layer], w["mnorm"][layer],
      w["pool_w" + sfx][layer], w["pool_scale"][layer], w["lb_logits"], w["hnorm"][layer],
      w["w_out" + sfx][layer])
    return outs[0], tuple(outs[1:])


def _ffn_kernel(x_ref, g_ref, w1_ref, w3_ref, w2_ref, o_ref, hn_s, *, precise):
    f = pl.program_id(1)
    prec = lax.Precision.HIGHEST if precise else None

    @pl.when(f == 0)
    def _start():
        x = x_ref[...]
        ms = jnp.mean(x * x, axis=-1, keepdims=True)
        hn_s[...] = (x * lax.rsqrt(ms + EPS) * g_ref[...]).astype(hn_s.dtype)
        o_ref[...] = x

    hn = hn_s[...]
    h1 = _dot(hn, w1_ref[...], prec)
    h3 = _dot(hn, w3_ref[...], prec)
    a = (h1 * _sigmoid(h1) * h3).astype(hn_s.dtype)
    o_ref[...] += _dot(a, w2_ref[...], prec)


def _ffn_rows_f32_call(xall, blocks, g, w1, w3, w2, *, tm, tf, name):
    n_f = D_FF // tf

    def x_map(i, f):
        idx = blocks[-1]
        for j in range(len(blocks) - 2, -1, -1):
            idx = jnp.where(i == j, blocks[j], idx)
        return (idx, 0)

    return pl.pallas_call(
        functools.partial(_ffn_kernel, precise=True),
        grid=(len(blocks), n_f),
        in_specs=[
            pl.BlockSpec((tm, D_MODEL), x_map),
            pl.BlockSpec((1, D_MODEL), lambda i, f: (0, 0)),
            pl.BlockSpec((D_MODEL, tf), lambda i, f: (0, f)),
            pl.BlockSpec((D_MODEL, tf), lambda i, f: (0, f)),
            pl.BlockSpec((tf, D_MODEL), lambda i, f: (f, 0)),
        ],
        out_specs=pl.BlockSpec((tm, D_MODEL), lambda i, f: (i, 0)),
        out_shape=jax.ShapeDtypeStruct((len(blocks) * tm, D_MODEL), F32),
        scratch_shapes=[pltpu.VMEM((tm, D_MODEL), F32)],
        compiler_params=pltpu.CompilerParams(dimension_semantics=("arbitrary", "arbitrary"),
                                             vmem_limit_bytes=VMEM_LIMIT),
        name=name,
    )(xall, g, w1, w3, w2)


def _ffn_call(xall, g, w1, w3, w2, *, tm, tf, name):
    n = xall.shape[0]
    n_f = D_FF // tf
    return pl.pallas_call(
        functools.partial(_ffn_kernel, precise=False),
        grid=(n // tm, n_f),
        in_specs=[
            pl.BlockSpec((tm, D_MODEL), lambda i, f: (i, 0)),
            pl.BlockSpec((1, D_MODEL), lambda i, f: (0, 0)),
            pl.BlockSpec((D_MODEL, tf), lambda i, f: (0, f)),
            pl.BlockSpec((D_MODEL, tf), lambda i, f: (0, f)),
            pl.BlockSpec((tf, D_MODEL), lambda i, f: (f, 0)),
        ],
        out_specs=pl.BlockSpec((tm, D_MODEL), lambda i, f: (i, 0)),
        out_shape=jax.ShapeDtypeStruct(xall.shape, F32),
        scratch_shapes=[pltpu.VMEM((tm, D_MODEL), BF16)],
        input_output_aliases={0: 0},
        compiler_params=pltpu.CompilerParams(dimension_semantics=("arbitrary", "arbitrary"),
                                             vmem_limit_bytes=VMEM_LIMIT),
        name=name,
    )(xall, g, w1, w3, w2)


ROW_TILE = 256


def _moe_kernel(x_ref, g_ref, wr_ref, w1_ref, w3_ref, w2_ref, fg_ref, o_ref,
                hn_s, xe_s, ye_s, rank_s, comb_s, rank_t_s, cnt_s, *, tb, n_f, final_norm):
    e = pl.program_id(1)
    f = pl.program_id(2)

    @pl.when((e == 0) & (f == 0))
    def _route():
        x = x_ref[...]
        ms = jnp.mean(x * x, axis=-1, keepdims=True)
        hn = x * lax.rsqrt(ms + EPS) * g_ref[...]
        hn_s[...] = hn.astype(BF16)
        o_ref[...] = x
        logits = jnp.dot(hn, wr_ref[...], preferred_element_type=F32, precision=lax.Precision.HIGHEST)
        lane = lax.broadcasted_iota(jnp.int32, (tb, HP), 1).astype(F32)
        lg = jnp.where(lane < N_EXPERTS, logits, NEG)
        v1 = jnp.max(lg, axis=1, keepdims=True)
        i1 = jnp.min(jnp.where(lg == v1, lane, float(HP)), axis=1, keepdims=True)
        mask1 = lane == i1
        lg2 = jnp.where(mask1, NEG, lg)
        v2 = jnp.max(lg2, axis=1, keepdims=True)
        i2 = jnp.min(jnp.where(lg2 == v2, lane, float(HP)), axis=1, keepdims=True)
        mask2 = lane == i2
        ex = jnp.exp(v2 - v1)
        ga = 1.0 / (1.0 + ex)
        comb = jnp.where(mask1, ga, 0.0) + jnp.where(mask2, ex * ga, 0.0)
        sel = mask1 | mask2
        rr = lax.broadcasted_iota(jnp.int32, (tb, tb), 0)
        cc = lax.broadcasted_iota(jnp.int32, (tb, tb), 1)
        tril_strict = jnp.where(rr > cc, 1.0, 0.0).astype(BF16)
        selb = jnp.where(sel, 1.0, 0.0)
        rank = jnp.where(sel, _dot(tril_strict, selb.astype(BF16)), -1.0)
        comb_s[...] = comb
        rank_s[...] = rank
        rank_t_s[...] = rank.T[0:N_EXPERTS, :]
        cnt = jnp.sum(selb, axis=0, keepdims=True).astype(jnp.int32)
        for j in range(N_EXPERTS):
            cnt_s[j] = cnt[0, j]

    n_rows = cnt_s[e]
    n_tiles = (n_rows + ROW_TILE - 1) // ROW_TILE

    @pl.when(f == 0)
    def _gather():
        rrow = rank_t_s[pl.ds(e, 1), :]

        def body(i, carry):
            r0 = pl.multiple_of(i * ROW_TILE, ROW_TILE)
            slot = (lax.broadcasted_iota(jnp.int32, (ROW_TILE, tb), 0) + r0).astype(F32)
            onehot = jnp.where(rrow == slot, 1.0, 0.0).astype(BF16)
            xe_s[pl.ds(r0, ROW_TILE), :] = _dot(onehot, hn_s[...]).astype(BF16)
            return carry

        lax.fori_loop(0, n_tiles, body, 0)

    def ffn_body(i, carry):
        r0 = pl.multiple_of(i * ROW_TILE, ROW_TILE)
        rows = pl.ds(r0, ROW_TILE)
        xe = xe_s[rows, :]
        h1 = _dot(xe, w1_ref[...])
        h3 = _dot(xe, w3_ref[...])
        a = (h1 * _sigmoid(h1) * h3).astype(BF16)
        part = _dot(a, w2_ref[...])

        @pl.when(f == 0)
        def _set():
            ye_s[rows, :] = part

        @pl.when(f != 0)
        def _add():
            ye_s[rows, :] += part

        return carry

    lax.fori_loop(0, n_tiles, ffn_body, 0)

    @pl.when(f == n_f - 1)
    def _scatter():
        lane = lax.broadcasted_iota(jnp.int32, (tb, HP), 1)
        pick = lane == e
        rcol = jnp.sum(jnp.where(pick, rank_s[...], 0.0), axis=1, keepdims=True)
        gcol = jnp.sum(jnp.where(pick, comb_s[...], 0.0), axis=1, keepdims=True)

        def body(i, carry):
            r0 = pl.multiple_of(i * ROW_TILE, ROW_TILE)
            slot = (lax.broadcasted_iota(jnp.int32, (tb, ROW_TILE), 1) + r0).astype(F32)
            onehot = jnp.where(rcol == slot, 1.0, 0.0).astype(BF16)
            o_ref[...] += gcol * _dot(onehot, ye_s[pl.ds(r0, ROW_TILE), :].astype(BF16))
            return carry

        lax.fori_loop(0, n_tiles, body, 0)

    if final_norm:
        @pl.when((e == N_EXPERTS - 1) & (f == n_f - 1))
        def _norm():
            y = o_ref[...]
            ms = jnp.mean(y * y, axis=-1, keepdims=True)
            o_ref[...] = y * lax.rsqrt(ms + EPS) * fg_ref[...]


def _moe_call(xall, g, wr, w1, w3, w2, fg, *, tb, tf, final_norm, name):
    n = xall.shape[0]
    n_f = D_FF // tf
    return pl.pallas_call(
        functools.partial(_moe_kernel, tb=tb, n_f=n_f, final_norm=final_norm),
        grid=(n // tb, N_EXPERTS, n_f),
        in_specs=[
            pl.BlockSpec((tb, D_MODEL), lambda i, e, f: (i, 0)),
            pl.BlockSpec((1, D_MODEL), lambda i, e, f: (0, 0)),
            pl.BlockSpec((D_MODEL, HP), lambda i, e, f: (0, 0)),
            pl.BlockSpec((None, D_MODEL, tf), lambda i, e, f: (e, 0, f)),
            pl.BlockSpec((None, D_MODEL, tf), lambda i, e, f: (e, 0, f)),
            pl.BlockSpec((None, tf, D_MODEL), lambda i, e, f: (e, f, 0)),
            pl.BlockSpec((1, D_MODEL), lambda i, e, f: (0, 0)),
        ],
        out_specs=pl.BlockSpec((tb, D_MODEL), lambda i, e, f: (i, 0)),
        out_shape=jax.ShapeDtypeStruct(xall.shape, F32),
        scratch_shapes=[
            pltpu.VMEM((tb, D_MODEL), BF16),
            pltpu.VMEM((tb, D_MODEL), BF16),
            pltpu.VMEM((tb, D_MODEL), F32),
            pltpu.VMEM((tb, HP), F32),
            pltpu.VMEM((tb, HP), F32),
            pltpu.VMEM((N_EXPERTS, tb), F32),
            pltpu.SMEM((N_EXPERTS,), jnp.int32),
        ],
        input_output_aliases={0: 0},
        compiler_params=pltpu.CompilerParams(dimension_semantics=("arbitrary", "arbitrary", "arbitrary"),
                                             vmem_limit_bytes=VMEM_LIMIT),
        name=name,
    )(xall, g, wr, w1, w3, w2, fg)


def _pad_heads(a, axis):
    axis = axis % a.ndim
    shp = a.shape
    a = a.reshape(shp[:axis] + (NH, DH) + shp[axis + 1:])
    padw = [(0, 0)] * a.ndim
    padw[axis + 1] = (0, HP - DH)
    a = jnp.pad(a, padw)
    return a.reshape(shp[:axis] + (HW,) + shp[axis + 1:])


def _unpad_heads(a, axis):
    axis = axis % a.ndim
    shp = a.shape
    a = a.reshape(shp[:axis] + (NH, HP) + shp[axis + 1:])
    a = lax.slice_in_dim(a, 0, DH, axis=axis + 1)
    return a.reshape(shp[:axis] + (NH * DH,) + shp[axis + 1:])


def _prep_weights(norm1_g, w_in, b_igate, b_fgate, conv_w, mlstm_norm_g, pool_w, pool_scale, lb_logits,
                  hgrn_norm_g, w_out):
    m_w, h_kw = NH * DH, HW
    widths = (m_w, m_w, m_w, m_w, NH, NH, P_W, h_kw, h_kw, m_w, m_w)
    pts, acc = [], 0
    for wd in widths[:-1]:
        acc += wd
        pts.append(acc)
    mq, mk, mv, mo, mi, mf, pu, hq, hf, hi, hg = jnp.split(w_in, pts, axis=-1)
    gates = jnp.pad(jnp.concatenate([mi, mf], axis=-1), ((0, 0), (0, 0), (0, HP - 2 * NH)))
    w_in_p = jnp.concatenate([_pad_heads(mq, -1), _pad_heads(mk, -1), _pad_heads(mv, -1), _pad_heads(mo, -1),
                              gates, pu, hq, hf, _pad_heads(hi, -1), _pad_heads(hg, -1)], axis=-1)
    w_out_p = jnp.concatenate([_pad_heads(w_out[:, 0:m_w], 1), w_out[:, m_w:m_w + P_W],
                               _pad_heads(w_out[:, m_w + P_W:], 1)], axis=1)
    gbias = jnp.pad(jnp.concatenate([b_igate, b_fgate], axis=-1), ((0, 0), (0, HP - 2 * NH)))[:, None, :]
    conv_p = jnp.concatenate([_pad_heads(conv_w[..., 0:m_w], -1), _pad_heads(conv_w[..., m_w:], -1)], axis=-1)
    eye = jnp.eye(len(POOL_WINDOWS), dtype=F32)
    pool_bd = jnp.einsum('lgce,gh->lgche', pool_w, eye).reshape(DEPTH, P_W, P_W)
    return {
        "g1": norm1_g[:, None, :], "w_in": w_in_p.astype(BF16), "w_in_f32": w_in_p, "gbias": gbias,
        "conv_w": conv_p, "mnorm": _pad_heads(mlstm_norm_g, -1)[:, None, :],
        "pool_w": pool_bd.astype(BF16), "pool_w_f32": pool_bd,
        "pool_scale": pool_scale[:, None, :], "lb_logits": lb_logits,
        "hnorm": _pad_heads(hgrn_norm_g, -1)[:, None, :], "w_out": w_out_p.astype(BF16), "w_out_f32": w_out_p,
    }


def _states_to_kernel(C, n, m, conv, pool, S):
    nb = C.shape[0]
    caug = jnp.concatenate([C, n[..., None]], axis=-1)
    caug = jnp.pad(caug, ((0, 0), (0, 0), (0, HP - DH), (0, HP - DH - 1)))
    mk = jnp.pad(jnp.broadcast_to(m[:, :, None], (nb, NH, HP)), ((0, 0), (0, 8 - NH), (0, 0)))
    m_w = NH * DH
    convk = jnp.concatenate([_pad_heads(conv[..., 0:m_w], -1), _pad_heads(conv[..., m_w:], -1)], axis=-1)
    convk = jnp.pad(convk, ((0, 0), (CONV_HDR - 3, 0), (0, 0)))
    poolk = jnp.pad(pool, ((0, 0), (1, 0), (0, 0)))
    sk = jnp.pad(jnp.swapaxes(S, -1, -2), ((0, 0), (0, 0), (0, HP - DH), (0, 0)))
    return caug, mk, convk, poolk, sk


def _states_from_kernel(st):
    caug, mk, convk, poolk, sk = st
    m_w = NH * DH
    C = caug[:, :, 0:DH, 0:DH]
    n = caug[:, :, 0:DH, DH]
    m = mk[:, 0:NH, 0]
    conv = convk[:, CONV_HDR - 3:, :]
    conv = jnp.concatenate([_unpad_heads(conv[..., 0:HW], -1), _unpad_heads(conv[..., HW:], -1)], axis=-1)
    pool = poolk[:, 1:, :]
    S = jnp.swapaxes(sk[:, :, 0:DH, :], -1, -2)
    return C, n, m, conv, pool, S


def _pick_tile(n, candidates):
    for c in candidates:
        if n % c == 0:
            return c
    raise ValueError(f"no row tile for {n}")


def kernel(x_prompt, x_sample, state_mlstm_C, state_mlstm_n, state_mlstm_m, state_mlstm_conv, state_pool,
           state_hgrn, meta_tokens, norm1_g, norm2_g, final_g, w_in, b_igate, b_fgate, conv_w, mlstm_norm_g,
           pool_w, pool_scale, lb_logits, hgrn_norm_g, w_out, ffn_w1, ffn_w3, ffn_w2, router_w, moe_w1,
           moe_w3, moe_w2):
    B, T, _ = x_prompt.shape
    SB, ST, _ = x_sample.shape
    w = _prep_weights(norm1_g, w_in, b_igate, b_fgate, conv_w, mlstm_norm_g, pool_w, pool_scale, lb_logits,
                      hgrn_norm_g, w_out)
    ffn_w1b, ffn_w3b, ffn_w2b = ffn_w1.astype(BF16), ffn_w3.astype(BF16), ffn_w2.astype(BF16)
    moe_w1b, moe_w3b, moe_w2b = moe_w1.astype(BF16), moe_w3.astype(BF16), moe_w2.astype(BF16)
    router_p = jnp.pad(router_w, ((0, 0), (0, 0), (0, HP - N_EXPERTS)))

    n_main = B * T
    off_s = n_main
    off_m = off_s + SB * ST
    assert SB * ST == TAIL and n_main % TAIL == 0
    n_tok = off_m + META_TOKENS
    tile = 1280 if n_main >= 16384 else 256
    n_pad = -(-n_tok // tile) * tile
    xall = jnp.concatenate([x_prompt.reshape(n_main, D_MODEL), x_sample.reshape(SB * ST, D_MODEL),
                            meta_tokens.astype(F32), jnp.zeros((n_pad - n_tok, D_MODEL), F32)], axis=0)
    tv_main = _pick_tile(T, (256, 128))

    zero_states = (jnp.zeros((1, NH, HP, HP), F32), jnp.zeros((1, 8, HP), F32),
                   jnp.zeros((1, CONV_HDR, 2 * HW), F32), jnp.zeros((1, 16, P_W), F32),
                   jnp.zeros((1, NH, HP, HP), F32))
    p_states, s_states = [], []
    tail_blocks = tuple((b * T + T - TAIL) // TAIL for b in range(B)) + (off_s // TAIL,)
    for l in range(DEPTH):
        precise = l < PRECISE_LAYERS
        xall, st_meta = _mixer_call(xall, zero_states, True, w, l, row_off=off_m, seq_stride=META_TOKENS, nb=1,
                                    seq=META_TOKENS, tv=META_TOKENS, pos0=0, precise=False,
                                    name=f"mixer_meta_{l}")
        if precise:
            xall, st_p = _mixer_call(xall, st_meta, True, w, l, row_off=0, seq_stride=T, nb=B, seq=T - TAIL,
                                     tv=tv_main, pos0=META_TOKENS, precise=False, name=f"mixer_prompt_{l}")
            xall, st_p = _mixer_call(xall, st_p, False, w, l, row_off=T - TAIL, seq_stride=T, nb=B, seq=TAIL,
                                     tv=TAIL, pos0=META_TOKENS + T - TAIL, precise=True,
                                     name=f"mixer_prompt_tail_{l}")
        else:
            xall, st_p = _mixer_call(xall, st_meta, True, w, l, row_off=0, seq_stride=T, nb=B, seq=T,
                                     tv=tv_main, pos0=META_TOKENS, precise=False, name=f"mixer_prompt_{l}")
        st_in = _states_to_kernel(state_mlstm_C[l], state_mlstm_n[l], state_mlstm_m[l], state_mlstm_conv[l],
                                  state_pool[l], state_hgrn[l])
        xall, st_s = _mixer_call(xall, st_in, False, w, l, row_off=off_s, seq_stride=ST, nb=SB, seq=ST, tv=ST,
                                 pos0=META_TOKENS + PAST_LEN, precise=True, name=f"mixer_sample_{l}")
        p_states.append(_states_from_kernel(st_p))
        s_states.append(_states_from_kernel(st_s))
        i = l // 2
        if l % 2 == 0:
            f32_blocks = tail_blocks if precise else tail_blocks[-1:]
            rows = _ffn_rows_f32_call(xall, f32_blocks, norm2_g[l][None, :], ffn_w1[i], ffn_w3[i], ffn_w2[i],
                                      tm=TAIL, tf=512, name=f"ffn_rows_f32_{l}")
            xall = _ffn_call(xall, norm2_g[l][None, :], ffn_w1b[i], ffn_w3b[i], ffn_w2b[i],
                             tm=tile, tf=512, name=f"ffn_{l}")
            for j, blk in enumerate(f32_blocks):
                xall = lax.dynamic_update_slice(xall, rows[j * TAIL:(j + 1) * TAIL], (blk * TAIL, 0))
        else:
            xall = _moe_call(xall, norm2_g[l][None, :], router_p[i], moe_w1b[i], moe_w3b[i], moe_w2b[i],
                             final_g[None, :], tb=tile, tf=512, final_norm=(l == DEPTH - 1), name=f"moe_{l}")
    y_prompt = xall[0:n_main].reshape(B, T, D_MODEL)
    y_sample = xall[off_s:off_m].reshape(SB, ST, D_MODEL)
    p_out = tuple(jnp.stack([s[j] for s in p_states], axis=0) for j in range(6))
    s_out = tuple(jnp.stack([s[j] for s in s_states], axis=0) for j in range(6))
    return (y_prompt, y_sample) + p_out + s_out
```

```python
import functools

import jax
import jax.numpy as jnp
from jax import lax
from jax.experimental import pallas as pl
from jax.experimental.pallas import tpu as pltpu

F32 = jnp.float32
BF16 = jnp.bfloat16

D_MODEL = 1024
DEPTH = 4
META_TOKENS = 16
PAST_LEN = 4096
EPS = 1e-6
NH = 4
DH = 96
HP = 128
HW = NH * HP
P_W = 256
P_GW = 64
POOL_WINDOWS = (2, 4, 8, 16)
D_FF = 3584
N_EXPERTS = 8
NEG = -1e30

OFF_MQ, OFF_MK, OFF_MV, OFF_MO = 0, 512, 1024, 1536
OFF_G = 2048
OFF_PU = 2176
OFF_HQ, OFF_HF, OFF_HI, OFF_HG = 2432, 2944, 3456, 3968
D_INP = 4480
OFF_YM, OFF_YP, OFF_YH = 0, 512, 768
D_MIXP = 1280

M_CHUNK = 128
H_CHUNK = 64
H_SUB = 16
CONV_HDR = 8
POOL_HDR = 32

TAIL = 256
PRECISE_LAYERS = 2

VMEM_LIMIT = 60 * 1024 * 1024


def _sigmoid(x):
    return 1.0 / (1.0 + jnp.exp(-x))


def _dot(a, b, precision=None):
    return jnp.dot(a, b, preferred_element_type=F32, precision=precision)


def _dot_nt(a, b, precision=None):
    return lax.dot_general(a, b, (((1,), (1,)), ((), ())), preferred_element_type=F32, precision=precision)


def _dot_tn(a, b, precision=None):
    return lax.dot_general(a, b, (((0,), (0,)), ((), ())), preferred_element_type=F32, precision=precision)


def _cumsum_rows(tril_bf, x, precise=False):
    if precise:
        return _dot(tril_bf.astype(F32), x, precision=lax.Precision.HIGHEST)
    n = x.shape[1]
    hi = x.astype(BF16)
    lo = (x - hi.astype(F32)).astype(BF16)
    both = _dot(tril_bf, jnp.concatenate([hi, lo], axis=1))
    return both[:, :n] + both[:, n:]


def _bcast_rows(row, n):
    return jnp.broadcast_to(row, (n, row.shape[1]))


def _mixer_kernel(x_ref, c0_ref, m0_ref, conv0_ref, pool0_ref, s0_ref,
                  g1_ref, win_ref, gb_ref, cw_ref, mng_ref, pw_ref, ps_ref, lbl_ref, hng_ref, wout_ref,
                  xo_ref, cf_ref, mf_ref, convf_ref, poolf_ref, sf_ref,
                  proj_s, qk_s, u_s, s2_s, s4_s, s8_s, hk_s, gate_s, y_s, c_s, m_s, st_s,
                  *, layer, tv, lb, pos0, n_t, precise):
    pad = lb - tv
    t = pl.program_id(1)
    prec = lax.Precision.HIGHEST if precise else None
    dot = functools.partial(_dot, precision=prec)
    dot_nt = functools.partial(_dot_nt, precision=prec)
    dot_tn = functools.partial(_dot_tn, precision=prec)

    def mm(a):
        return a if precise else a.astype(BF16)

    @pl.when(t == 0)
    def _init():
        c_s[...] = c0_ref[...]
        m_s[...] = m0_ref[...]
        st_s[...] = s0_ref[...]
        qk_s[...] = jnp.zeros(qk_s.shape, F32)
        qk_s[pad:pad + CONV_HDR, :] = conv0_ref[...]
        u_s[...] = jnp.zeros(u_s.shape, F32)
        u_s[pad + 16:pad + 32, :] = pool0_ref[...]
        s2_s[0:POOL_HDR, :] = jnp.zeros((POOL_HDR, P_W), F32)
        s4_s[0:POOL_HDR, :] = jnp.zeros((POOL_HDR, P_W), F32)
        s8_s[0:POOL_HDR, :] = jnp.zeros((POOL_HDR, P_W), F32)
        if pad:
            proj_s[0:pad, :] = jnp.zeros((pad, D_INP), F32)

    x = x_ref[...]
    ms = jnp.mean(x * x, axis=-1, keepdims=True)
    hn = mm(x * lax.rsqrt(ms + EPS) * g1_ref[...])
    qk_s[CONV_HDR + pad:CONV_HDR + lb, :] = dot(hn, win_ref[:, 0:OFF_MV])
    proj_s[pad:lb, OFF_MV:OFF_PU] = dot(hn, win_ref[:, OFF_MV:OFF_PU])
    u_s[POOL_HDR + pad:POOL_HDR + lb, :] = dot(hn, win_ref[:, OFF_PU:OFF_HQ])
    proj_s[pad:lb, OFF_HQ:D_INP] = dot(hn, win_ref[:, OFF_HQ:D_INP])

    row = lax.broadcasted_iota(jnp.int32, (lb, HP), 0)
    lane = lax.broadcasted_iota(jnp.int32, (lb, HP), 1)

    acc = qk_s[5:5 + lb, :] * cw_ref[0:1, :]
    for j in range(1, 4):
        acc = acc + qk_s[5 + j:5 + j + lb, :] * cw_ref[j:j + 1, :]
    qk = acc * _sigmoid(acc)
    proj_s[:, OFF_MQ:OFF_MK] = qk[:, 0:HW]
    proj_s[:, OFF_MK:OFF_MV] = qk[:, HW:2 * HW] * (DH ** -0.5)
    conv_tail = qk_s[lb:lb + CONV_HDR, :]
    qk_s[0:CONV_HDR, :] = conv_tail

    gpre = proj_s[:, OFF_G:OFF_G + HP] + gb_ref[...]
    lsig = jnp.minimum(gpre, 0.0) - jnp.log(1.0 + jnp.exp(-jnp.abs(gpre)))
    gates = jnp.where(lane < NH, gpre, jnp.where(lane < 2 * NH, lsig, 0.0))
    if pad:
        gates = jnp.where(row >= pad, gates, jnp.where(lane < NH, NEG, 0.0))
    gate_s[...] = gates

    rr = lax.broadcasted_iota(jnp.int32, (M_CHUNK, M_CHUNK), 0)
    cc = lax.broadcasted_iota(jnp.int32, (M_CHUNK, M_CHUNK), 1)
    causal = rr >= cc
    tril_m = jnp.where(causal, 1.0, 0.0).astype(BF16)
    lane_c = lax.broadcasted_iota(jnp.int32, (M_CHUNK, HP), 1)

    for c in range(lb // M_CHUNK):
        rows = slice(c * M_CHUNK, (c + 1) * M_CHUNK)
        gt = gate_s[rows, :]
        bcum = _cumsum_rows(tril_m, jnp.where(lane_c >= NH, gt, 0.0), precise)
        cg = gt - pltpu.roll(bcum, HP - NH, axis=1)
        cg_t = cg.T
        for h in range(NH):
            q = proj_s[rows, OFF_MQ + h * HP:OFF_MQ + (h + 1) * HP]
            k = proj_s[rows, OFF_MK + h * HP:OFF_MK + (h + 1) * HP]
            v = proj_s[rows, OFF_MV + h * HP:OFF_MV + (h + 1) * HP]
            c_row = cg_t[h:h + 1, :]
            c_col = cg[:, h:h + 1]
            b_col = bcum[:, NH + h:NH + h + 1]
            m_prev = m_s[h:h + 1, 0:1]
            mx = jnp.maximum(jnp.max(jnp.where(causal, c_row, NEG), axis=1, keepdims=True), m_prev)
            w = jnp.exp(jnp.where(causal, c_row - mx, NEG))
            w_int = jnp.exp(m_prev - mx)
            mx_last = mx[M_CHUNK - 1:M_CHUNK, :]
            s = dot_nt(mm(q), mm(k)) * w
            vaug = jnp.where(lane_c == DH, 1.0, v)
            caug = c_s[h]
            lhs = mm(jnp.concatenate([s, q * w_int], axis=1))
            rhs = mm(jnp.concatenate([vaug, caug], axis=0))
            nd = dot(lhs, rhs)
            den = nd[:, DH:DH + 1]
            rden = 1.0 / jnp.maximum(jnp.abs(den), jnp.exp(-(b_col + mx)))
            wl = jnp.exp(c_col - mx_last)
            decay = jnp.exp(m_prev - mx_last)
            c_s[h] = decay * caug + dot_tn(mm(k * wl), mm(vaug))
            m_s[h:h + 1, :] = jnp.broadcast_to(b_col[M_CHUNK - 1:M_CHUNK, :] + mx_last, (1, HP))
            mo = proj_s[rows, OFF_MO + h * HP:OFF_MO + (h + 1) * HP]
            z = jnp.where(lane_c < DH, nd * _sigmoid(mo), 0.0)
            ssq = jnp.sum(z * z, axis=1, keepdims=True) * (1.0 / DH)
            fac = rden * lax.rsqrt(rden * rden * ssq + EPS)
            y_s[rows, OFF_YM + h * HP:OFF_YM + (h + 1) * HP] = mm(z * fac * mng_ref[:, h * HP:(h + 1) * HP])

    n_ext = lb + 16
    s2_s[16:16 + n_ext, :] = u_s[16:16 + n_ext, :] + u_s[15:15 + n_ext, :]
    s4_s[16:16 + n_ext, :] = s2_s[16:16 + n_ext, :] + s2_s[14:14 + n_ext, :]
    s8_s[16:16 + n_ext, :] = s4_s[16:16 + n_ext, :] + s4_s[12:12 + n_ext, :]
    u_cur = u_s[POOL_HDR:POOL_HDR + lb, :]
    w2 = s2_s[POOL_HDR:POOL_HDR + lb, :]
    w4 = s4_s[POOL_HDR:POOL_HDR + lb, :]
    w8 = s8_s[POOL_HDR:POOL_HDR + lb, :]
    w16 = w8 + s8_s[POOL_HDR - 8:POOL_HDR - 8 + lb, :]
    lane_p = lax.broadcasted_iota(jnp.int32, (lb, P_W), 1)
    wsum = jnp.where(lane_p < P_GW, w2, jnp.where(lane_p < 2 * P_GW, w4, jnp.where(lane_p < 3 * P_GW, w8, w16)))
    if pos0 >= POOL_WINDOWS[-1] - 1:
        inv = jnp.where(lane_p < P_GW, 0.5, jnp.where(lane_p < 2 * P_GW, 0.25,
                                                       jnp.where(lane_p < 3 * P_GW, 0.125, 0.0625)))
        mean = wsum * inv
    else:
        row_p = lax.broadcasted_iota(jnp.int32, (lb, P_W), 0)
        posn = (row_p + (pos0 + 1 - pad + t * tv)).astype(F32)
        wlen = jnp.where(lane_p < P_GW, 2.0, jnp.where(lane_p < 2 * P_GW, 4.0,
                                                        jnp.where(lane_p < 3 * P_GW, 8.0, 16.0)))
        mean = wsum / jnp.maximum(jnp.minimum(wlen, posn), 1.0)
    pooled = mm(mean - u_cur)
    y_s[:, OFF_YP:OFF_YH] = mm(dot(pooled, pw_ref[...]) * ps_ref[...])
    pool_tail = u_s[lb:lb + POOL_HDR, :]
    u_s[0:POOL_HDR, :] = pool_tail

    lbl = lbl_ref[...]
    e = jnp.exp(lbl - jnp.max(lbl, axis=0, keepdims=True))
    p = e / jnp.sum(e, axis=0, keepdims=True)
    lbv = jnp.sum(p[0:layer + 1, :], axis=0, keepdims=True) - p[0:1, :]
    hq = proj_s[:, OFF_HQ:OFF_HF]
    hf = proj_s[:, OFF_HF:OFF_HI]
    fg = lbv + (1.0 - lbv) * _sigmoid(hf)
    kh = 1.0 - fg
    lg = jnp.log(fg)
    if pad:
        row_h = lax.broadcasted_iota(jnp.int32, (lb, HW), 0)
        kh = jnp.where(row_h >= pad, kh, 0.0)
        lg = jnp.where(row_h >= pad, lg, 0.0)
    proj_s[:, OFF_HQ:OFF_HF] = hq * _sigmoid(hq)
    proj_s[:, OFF_HF:OFF_HI] = lg
    hk_s[...] = kh

    r64 = lax.broadcasted_iota(jnp.int32, (H_CHUNK, H_CHUNK), 0)
    c64 = lax.broadcasted_iota(jnp.int32, (H_CHUNK, H_CHUNK), 1)
    tril_h = jnp.where(r64 >= c64, 1.0, 0.0).astype(BF16)
    bdiff = r64 // H_SUB - c64 // H_SUB
    mask_intra = (bdiff == 0) & (r64 >= c64)
    n_sub = H_CHUNK // H_SUB

    for c in range(lb // H_CHUNK):
        rows = slice(c * H_CHUNK, (c + 1) * H_CHUNK)
        g = _cumsum_rows(tril_h, proj_s[rows, OFF_HF:OFF_HI], precise)
        qh = proj_s[rows, OFF_HQ:OFF_HF]
        khc = hk_s[rows, :]
        bnd = [jnp.zeros((1, HW), F32)] + [g[(j + 1) * H_SUB - 1:(j + 1) * H_SUB, :] for j in range(n_sub)]
        g_start = jnp.concatenate([_bcast_rows(bnd[j], H_SUB) for j in range(n_sub)], axis=0)
        g_end = jnp.concatenate([_bcast_rows(bnd[j + 1], H_SUB) for j in range(n_sub)], axis=0)
        g_last = bnd[n_sub]
        qt = qh * jnp.exp(g - g_start)
        khat = khc * jnp.exp(g_end - g)
        kbar = khc * jnp.exp(g_start - g)
        qg = qh * jnp.exp(g)
        kend = khc * jnp.exp(g_last - g)
        dsub = [jnp.exp(bnd[j + 1] - bnd[j]) for j in range(n_sub - 1)]
        ones = jnp.ones((H_SUB, HW), F32)
        qlev = [qt]
        for d in range(1, n_sub - 1):
            fac = jnp.concatenate([ones] * d + [_bcast_rows(dsub[j - d], H_SUB) for j in range(d, n_sub)], axis=0)
            qlev.append(qlev[-1] * fac)
        for h in range(NH):
            sl = slice(h * HP, (h + 1) * HP)
            lhs = mm(jnp.concatenate([ql[:, sl] for ql in qlev], axis=0))
            inter = dot_nt(lhs, mm(khat[:, sl]))
            intra = dot_nt(mm(qt[:, sl]), mm(kbar[:, sl]))
            att = jnp.where(mask_intra, intra, 0.0)
            for d in range(n_sub - 1):
                att = att + jnp.where(bdiff == d + 1, inter[d * H_CHUNK:(d + 1) * H_CHUNK, :], 0.0)
            iv = mm(proj_s[rows, OFF_HI + h * HP:OFF_HI + (h + 1) * HP])
            st = st_s[h]
            o = dot(mm(att), iv) + dot_nt(mm(qg[:, sl]), mm(st))
            st_s[h] = st * jnp.exp(g_last[:, sl]) + dot_tn(iv, mm(kend[:, sl]))
            msq = jnp.sum(o * o, axis=1, keepdims=True) * (1.0 / DH)
            on = o * lax.rsqrt(msq + EPS) * hng_ref[:, sl]
            hg = proj_s[rows, OFF_HG + h * HP:OFF_HG + (h + 1) * HP]
            y_s[rows, OFF_YH + h * HP:OFF_YH + (h + 1) * HP] = mm(on * (hg * _sigmoid(hg)))

    out = dot(y_s[...], wout_ref[...])
    xo_ref[...] = x + out[pad:lb, :]

    @pl.when(t == n_t - 1)
    def _final():
        cf_ref[...] = c_s[...]
        mf_ref[...] = m_s[...]
        convf_ref[...] = conv_tail
        poolf_ref[...] = pool_tail[16:32, :]
        sf_ref[...] = st_s[...]


def _mixer_call(xall, states, shared_init, w, layer, *, row_off, seq_stride, nb, seq, tv, pos0, precise, name):
    lb = max(tv, M_CHUNK)
    n_t = seq // tv
    assert row_off % tv == 0 and seq_stride % tv == 0 and seq % tv == 0
    blk0, blk_stride = row_off // tv, seq_stride // tv
    c0, m0, conv0, pool0, s0 = states

    def x_map(b, t):
        return (blk0 + b * blk_stride + t, 0)

    def st_map(b, t):
        return (0 if shared_init else b, 0, 0, 0)

    def st_map3(b, t):
        return (0 if shared_init else b, 0, 0)

    def const_spec(shape):
        return pl.BlockSpec(shape, lambda b, t: (0,) * len(shape), pipeline_mode=pl.Buffered(1))

    in_specs = [
        pl.BlockSpec((tv, D_MODEL), x_map),
        pl.BlockSpec((None, NH, HP, HP), st_map),
        pl.BlockSpec((None, 8, HP), st_map3),
        pl.BlockSpec((None, CONV_HDR, 2 * HW), st_map3),
        pl.BlockSpec((None, 16, P_W), st_map3),
        pl.BlockSpec((None, NH, HP, HP), st_map),
        const_spec((1, D_MODEL)),
        const_spec((D_MODEL, D_INP)),
        const_spec((1, HP)),
        const_spec((4, 2 * HW)),
        const_spec((1, HW)),
        const_spec((P_W, P_W)),
        const_spec((1, P_W)),
        const_spec((DEPTH, HW)),
        const_spec((1, HW)),
        const_spec((D_MIXP, D_MODEL)),
    ]
    out_specs = [
        pl.BlockSpec((tv, D_MODEL), x_map),
        pl.BlockSpec((None, NH, HP, HP), lambda b, t: (b, 0, 0, 0)),
        pl.BlockSpec((None, 8, HP), lambda b, t: (b, 0, 0)),
        pl.BlockSpec((None, CONV_HDR, 2 * HW), lambda b, t: (b, 0, 0)),
        pl.BlockSpec((None, 16, P_W), lambda b, t: (b, 0, 0)),
        pl.BlockSpec((None, NH, HP, HP), lambda b, t: (b, 0, 0, 0)),
    ]
    out_shape = [
        jax.ShapeDtypeStruct(xall.shape, F32),
        jax.ShapeDtypeStruct((nb, NH, HP, HP), F32),
        jax.ShapeDtypeStruct((nb, 8, HP), F32),
        jax.ShapeDtypeStruct((nb, CONV_HDR, 2 * HW), F32),
        jax.ShapeDtypeStruct((nb, 16, P_W), F32),
        jax.ShapeDtypeStruct((nb, NH, HP, HP), F32),
    ]
    scratch = [
        pltpu.VMEM((lb, D_INP), F32),
        pltpu.VMEM((CONV_HDR + lb, 2 * HW), F32),
        pltpu.VMEM((POOL_HDR + lb, P_W), F32),
        pltpu.VMEM((POOL_HDR + lb, P_W), F32),
        pltpu.VMEM((POOL_HDR + lb, P_W), F32),
        pltpu.VMEM((POOL_HDR + lb, P_W), F32),
        pltpu.VMEM((lb, HW), F32),
        pltpu.VMEM((lb, HP), F32),
        pltpu.VMEM((lb, D_MIXP), F32 if precise else BF16),
        pltpu.VMEM((NH, HP, HP), F32),
        pltpu.VMEM((8, HP), F32),
        pltpu.VMEM((NH, HP, HP), F32),
    ]
    sfx = "_f32" if precise else ""
    kern = functools.partial(_mixer_kernel, layer=layer, tv=tv, lb=lb, pos0=pos0, n_t=n_t, precise=precise)
    outs = pl.pallas_call(
        kern,
        grid=(nb, n_t),
        in_specs=in_specs,
        out_specs=out_specs,
        out_shape=out_shape,
        scratch_shapes=scratch,
        input_output_aliases={0: 0},
        compiler_params=pltpu.CompilerParams(dimension_semantics=("arbitrary", "arbitrary"),
                                             vmem_limit_bytes=VMEM_LIMIT),
        name=name,
    )(xall, c0, m0, conv0, pool0, s0,
      w["g1"][layer], w["w_in" + sfx][layer], w["gbias"][layer], w["conv_w"][layer], w["mnorm"][layer],
      w["pool_w" + sfx][layer], w["pool_scale"][layer], w["lb_logits"], w["hnorm"][layer],
      w["w_out" + sfx][layer])
    return outs[0], tuple(outs[1:])


def _ffn_kernel(x_ref, g_ref, w1_ref, w3_ref, w2_ref, o_ref, hn_s, *, precise):
    f = pl.program_id(1)
    prec = lax.Precision.HIGHEST if precise else None

    @pl.when(f == 0)
    def _start():
        x = x_ref[...]
        ms = jnp.mean(x * x, axis=-1, keepdims=True)
        hn_s[...] = (x * lax.rsqrt(ms + EPS) * g_ref[...]).astype(hn_s.dtype)
        o_ref[...] = x

    hn = hn_s[...]
    h1 = _dot(hn, w1_ref[...], prec)
    h3 = _dot(hn, w3_ref[...], prec)
    a = (h1 * _sigmoid(h1) * h3).astype(hn_s.dtype)
    o_ref[...] += _dot(a, w2_ref[...], prec)


def _ffn_rows_f32_call(xall, blocks, g, w1, w3, w2, *, tm, tf, name):
    n_f = D_FF // tf

    def x_map(i, f):
        idx = blocks[-1]
        for j in range(len(blocks) - 2, -1, -1):
            idx = jnp.where(i == j, blocks[j], idx)
        return (idx, 0)

    return pl.pallas_call(
        functools.partial(_ffn_kernel, precise=True),
        grid=(len(blocks), n_f),
        in_specs=[
            pl.BlockSpec((tm, D_MODEL), x_map),
            pl.BlockSpec((1, D_MODEL), lambda i, f: (0, 0)),
            pl.BlockSpec((D_MODEL, tf), lambda i, f: (0, f)),
            pl.BlockSpec((D_MODEL, tf), lambda i, f: (0, f)),
            pl.BlockSpec((tf, D_MODEL), lambda i, f: (f, 0)),
        ],
        out_specs=pl.BlockSpec((tm, D_MODEL), lambda i, f: (i, 0)),
        out_shape=jax.ShapeDtypeStruct((len(blocks) * tm, D_MODEL), F32),
        scratch_shapes=[pltpu.VMEM((tm, D_MODEL), F32)],
        compiler_params=pltpu.CompilerParams(dimension_semantics=("arbitrary", "arbitrary"),
                                             vmem_limit_bytes=VMEM_LIMIT),
        name=name,
    )(xall, g, w1, w3, w2)


def _ffn_call(xall, g, w1, w3, w2, *, tm, tf, name):
    n = xall.shape[0]
    n_f = D_FF // tf
    return pl.pallas_call(
        functools.partial(_ffn_kernel, precise=False),
        grid=(n // tm, n_f),
        in_specs=[
            pl.BlockSpec((tm, D_MODEL), lambda i, f: (i, 0)),
            pl.BlockSpec((1, D_MODEL), lambda i, f: (0, 0)),
            pl.BlockSpec((D_MODEL, tf), lambda i, f: (0, f)),
            pl.BlockSpec((D_MODEL, tf), lambda i, f: (0, f)),
            pl.BlockSpec((tf, D_MODEL), lambda i, f: (f, 0)),
        ],
        out_specs=pl.BlockSpec((tm, D_MODEL), lambda i, f: (i, 0)),
        out_shape=jax.ShapeDtypeStruct(xall.shape, F32),
        scratch_shapes=[pltpu.VMEM((tm, D_MODEL), BF16)],
        input_output_aliases={0: 0},
        compiler_params=pltpu.CompilerParams(dimension_semantics=("arbitrary", "arbitrary"),
                                             vmem_limit_bytes=VMEM_LIMIT),
        name=name,
    )(xall, g, w1, w3, w2)


ROW_TILE = 256


def _moe_kernel(x_ref, g_ref, wr_ref, w1_ref, w3_ref, w2_ref, fg_ref, o_ref,
                hn_s, xe_s, ye_s, rank_s, comb_s, rank_t_s, cnt_s, *, tb, n_f, final_norm):
    e = pl.program_id(1)
    f = pl.program_id(2)

    @pl.when((e == 0) & (f == 0))
    def _route():
        x = x_ref[...]
        ms = jnp.mean(x * x, axis=-1, keepdims=True)
        hn = x * lax.rsqrt(ms + EPS) * g_ref[...]
        hn_s[...] = hn.astype(BF16)
        o_ref[...] = x
        logits = jnp.dot(hn, wr_ref[...], preferred_element_type=F32, precision=lax.Precision.HIGHEST)
        lane = lax.broadcasted_iota(jnp.int32, (tb, HP), 1).astype(F32)
        lg = jnp.where(lane < N_EXPERTS, logits, NEG)
        v1 = jnp.max(lg, axis=1, keepdims=True)
        i1 = jnp.min(jnp.where(lg == v1, lane, float(HP)), axis=1, keepdims=True)
        mask1 = lane == i1
        lg2 = jnp.where(mask1, NEG, lg)
        v2 = jnp.max(lg2, axis=1, keepdims=True)
        i2 = jnp.min(jnp.where(lg2 == v2, lane, float(HP)), axis=1, keepdims=True)
        mask2 = lane == i2
        ex = jnp.exp(v2 - v1)
        ga = 1.0 / (1.0 + ex)
        comb = jnp.where(mask1, ga, 0.0) + jnp.where(mask2, ex * ga, 0.0)
        sel = mask1 | mask2
        rr = lax.broadcasted_iota(jnp.int32, (tb, tb), 0)
        cc = lax.broadcasted_iota(jnp.int32, (tb, tb), 1)
        tril_strict = jnp.where(rr > cc, 1.0, 0.0).astype(BF16)
        selb = jnp.where(sel, 1.0, 0.0)
        rank = jnp.where(sel, _dot(tril_strict, selb.astype(BF16)), -1.0)
        comb_s[...] = comb
        rank_s[...] = rank
        rank_t_s[...] = rank.T[0:N_EXPERTS, :]
        cnt = jnp.sum(selb, axis=0, keepdims=True).astype(jnp.int32)
        for j in range(N_EXPERTS):
            cnt_s[j] = cnt[0, j]

    n_rows = cnt_s[e]
    n_tiles = (n_rows + ROW_TILE - 1) // ROW_TILE

    @pl.when(f == 0)
    def _gather():
        rrow = rank_t_s[pl.ds(e, 1), :]

        def body(i, carry):
            r0 = pl.multiple_of(i * ROW_TILE, ROW_TILE)
            slot = (lax.broadcasted_iota(jnp.int32, (ROW_TILE, tb), 0) + r0).astype(F32)
            onehot = jnp.where(rrow == slot, 1.0, 0.0).astype(BF16)
            xe_s[pl.ds(r0, ROW_TILE), :] = _dot(onehot, hn_s[...]).astype(BF16)
            return carry

        lax.fori_loop(0, n_tiles, body, 0)

    def ffn_body(i, carry):
        r0 = pl.multiple_of(i * ROW_TILE, ROW_TILE)
        rows = pl.ds(r0, ROW_TILE)
        xe = xe_s[rows, :]
        h1 = _dot(xe, w1_ref[...])
        h3 = _dot(xe, w3_ref[...])
        a = (h1 * _sigmoid(h1) * h3).astype(BF16)
        part = _dot(a, w2_ref[...])

        @pl.when(f == 0)
        def _set():
            ye_s[rows, :] = part

        @pl.when(f != 0)
        def _add():
            ye_s[rows, :] += part

        return carry

    lax.fori_loop(0, n_tiles, ffn_body, 0)

    @pl.when(f == n_f - 1)
    def _scatter():
        lane = lax.broadcasted_iota(jnp.int32, (tb, HP), 1)
        pick = lane == e
        rcol = jnp.sum(jnp.where(pick, rank_s[...], 0.0), axis=1, keepdims=True)
        gcol = jnp.sum(jnp.where(pick, comb_s[...], 0.0), axis=1, keepdims=True)

        def body(i, carry):
            r0 = pl.multiple_of(i * ROW_TILE, ROW_TILE)
            slot = (lax.broadcasted_iota(jnp.int32, (tb, ROW_TILE), 1) + r0).astype(F32)
            onehot = jnp.where(rcol == slot, 1.0, 0.0).astype(BF16)
            o_ref[...] += gcol * _dot(onehot, ye_s[pl.ds(r0, ROW_TILE), :].astype(BF16))
            return carry

        lax.fori_loop(0, n_tiles, body, 0)

    if final_norm:
        @pl.when((e == N_EXPERTS - 1) & (f == n_f - 1))
        def _norm():
            y = o_ref[...]
            ms = jnp.mean(y * y, axis=-1, keepdims=True)
            o_ref[...] = y * lax.rsqrt(ms + EPS) * fg_ref[...]


def _moe_call(xall, g, wr, w1, w3, w2, fg, *, tb, tf, final_norm, name):
    n = xall.shape[0]
    n_f = D_FF // tf
    return pl.pallas_call(
        functools.partial(_moe_kernel, tb=tb, n_f=n_f, final_norm=final_norm),
        grid=(n // tb, N_EXPERTS, n_f),
        in_specs=[
            pl.BlockSpec((tb, D_MODEL), lambda i, e, f: (i, 0)),
            pl.BlockSpec((1, D_MODEL), lambda i, e, f: (0, 0)),
            pl.BlockSpec((D_MODEL, HP), lambda i, e, f: (0, 0)),
            pl.BlockSpec((None, D_MODEL, tf), lambda i, e, f: (e, 0, f)),
            pl.BlockSpec((None, D_MODEL, tf), lambda i, e, f: (e, 0, f)),
            pl.BlockSpec((None, tf, D_MODEL), lambda i, e, f: (e, f, 0)),
            pl.BlockSpec((1, D_MODEL), lambda i, e, f: (0, 0)),
        ],
        out_specs=pl.BlockSpec((tb, D_MODEL), lambda i, e, f: (i, 0)),
        out_shape=jax.ShapeDtypeStruct(xall.shape, F32),
        scratch_shapes=[
            pltpu.VMEM((tb, D_MODEL), BF16),
            pltpu.VMEM((tb, D_MODEL), BF16),
            pltpu.VMEM((tb, D_MODEL), F32),
            pltpu.VMEM((tb, HP), F32),
            pltpu.VMEM((tb, HP), F32),
            pltpu.VMEM((N_EXPERTS, tb), F32),
            pltpu.SMEM((N_EXPERTS,), jnp.int32),
        ],
        input_output_aliases={0: 0},
        compiler_params=pltpu.CompilerParams(dimension_semantics=("arbitrary", "arbitrary", "arbitrary"),
                                             vmem_limit_bytes=VMEM_LIMIT),
        name=name,
    )(xall, g, wr, w1, w3, w2, fg)


GATHER_TILE = 256
NO_SLOT = -1e9


def _route_kernel(x_ref, g_ref, wr_ref, hn_ref, rank_t_ref, rank_c_ref, comb_ref, cnt_ref, *, tb):
    x = x_ref[...]
    ms = jnp.mean(x * x, axis=-1, keepdims=True)
    hn = x * lax.rsqrt(ms + EPS) * g_ref[...]
    hn_ref[...] = hn.astype(BF16)
    logits = jnp.dot(hn, wr_ref[...], preferred_element_type=F32, precision=lax.Precision.HIGHEST)
    lane = lax.broadcasted_iota(jnp.int32, (tb, HP), 1).astype(F32)
    lg = jnp.where(lane < N_EXPERTS, logits, NEG)
    v1 = jnp.max(lg, axis=1, keepdims=True)
    i1 = jnp.min(jnp.where(lg == v1, lane, float(HP)), axis=1, keepdims=True)
    mask1 = lane == i1
    lg2 = jnp.where(mask1, NEG, lg)
    v2 = jnp.max(lg2, axis=1, keepdims=True)
    i2 = jnp.min(jnp.where(lg2 == v2, lane, float(HP)), axis=1, keepdims=True)
    mask2 = lane == i2
    ex = jnp.exp(v2 - v1)
    ga = 1.0 / (1.0 + ex)
    comb_ref[...] = jnp.where(mask1, ga, 0.0) + jnp.where(mask2, ex * ga, 0.0)
    sel = mask1 | mask2
    rr = lax.broadcasted_iota(jnp.int32, (tb, tb), 0)
    cc = lax.broadcasted_iota(jnp.int32, (tb, tb), 1)
    tril_strict = jnp.where(rr > cc, 1.0, 0.0).astype(BF16)
    selb = jnp.where(sel, 1.0, 0.0)
    rank = jnp.where(sel, _dot(tril_strict, selb.astype(BF16)), NO_SLOT)
    rank_c_ref[...] = rank
    rank_t_ref[...] = rank.T[0:N_EXPERTS, :]
    cnt_ref[...] = jnp.broadcast_to(jnp.sum(selb, axis=0, keepdims=True), (8, HP))


def _route_call(xall, g, wr, *, tb, name):
    n = xall.shape[0]
    nb = n // tb
    return pl.pallas_call(
        functools.partial(_route_kernel, tb=tb),
        grid=(nb,),
        in_specs=[
            pl.BlockSpec((tb, D_MODEL), lambda i: (i, 0)),
            pl.BlockSpec((1, D_MODEL), lambda i: (0, 0)),
            pl.BlockSpec((D_MODEL, HP), lambda i: (0, 0)),
        ],
        out_specs=[
            pl.BlockSpec((tb, D_MODEL), lambda i: (i, 0)),
            pl.BlockSpec((None, N_EXPERTS, tb), lambda i: (i, 0, 0)),
            pl.BlockSpec((tb, HP), lambda i: (i, 0)),
            pl.BlockSpec((tb, HP), lambda i: (i, 0)),
            pl.BlockSpec((None, 8, HP), lambda i: (i, 0, 0)),
        ],
        out_shape=[
            jax.ShapeDtypeStruct((n, D_MODEL), BF16),
            jax.ShapeDtypeStruct((nb, N_EXPERTS, tb), F32),
            jax.ShapeDtypeStruct((n, HP), F32),
            jax.ShapeDtypeStruct((n, HP), F32),
            jax.ShapeDtypeStruct((nb, 8, HP), F32),
        ],
        compiler_params=pltpu.CompilerParams(dimension_semantics=("arbitrary",), vmem_limit_bytes=VMEM_LIMIT),
        name=name,
    )(xall, g, wr)


def _gather_kernel(tile_ref, blk_ref, exp_ref, off_ref, first_ref, act_ref, rank_t_ref, hn_ref, init_ref, xs_ref,
                   *, tb):
    del tile_ref, blk_ref, init_ref
    p = pl.program_id(0)

    @pl.when(act_ref[p] == 1)
    def _():
        rrow = rank_t_ref[pl.ds(exp_ref[p], 1), :]
        slot = (lax.broadcasted_iota(jnp.int32, (GATHER_TILE, tb), 0) - off_ref[p]).astype(F32)
        onehot = jnp.where(rrow == slot, 1.0, 0.0).astype(BF16)
        val = _dot(onehot, hn_ref[...]).astype(BF16)

        @pl.when(first_ref[p] == 1)
        def _set():
            xs_ref[...] = val

        @pl.when(first_ref[p] == 0)
        def _add():
            xs_ref[...] += val


def _gather_call(pairs, rank_t, hn, n_rows, *, tb, name):
    n_pairs = pairs[0].shape[0]
    grid_spec = pltpu.PrefetchScalarGridSpec(
        num_scalar_prefetch=6,
        grid=(n_pairs,),
        in_specs=[
            pl.BlockSpec((None, N_EXPERTS, tb), lambda p, tile, blk, *_: (blk[p], 0, 0)),
            pl.BlockSpec((tb, D_MODEL), lambda p, tile, blk, *_: (blk[p], 0)),
            pl.BlockSpec(memory_space=pl.ANY),
        ],
        out_specs=pl.BlockSpec((GATHER_TILE, D_MODEL), lambda p, tile, *_: (tile[p], 0)),
    )
    return pl.pallas_call(
        functools.partial(_gather_kernel, tb=tb),
        grid_spec=grid_spec,
        out_shape=jax.ShapeDtypeStruct((n_rows, D_MODEL), BF16),
        input_output_aliases={8: 0},
        compiler_params=pltpu.CompilerParams(dimension_semantics=("arbitrary",), vmem_limit_bytes=VMEM_LIMIT),
        name=name,
    )(*pairs, rank_t, hn, jnp.zeros((n_rows, D_MODEL), BF16))


def _experts_kernel(exp_ref, act_ref, xs_ref, w1_ref, w3_ref, w2_ref, ys_ref, acc_s, *, n_f):
    del exp_ref
    i = pl.program_id(0)
    f = pl.program_id(1)

    @pl.when(act_ref[i] == 1)
    def _():
        xe = xs_ref[...]
        h1 = _dot(xe, w1_ref[...].astype(BF16))
        h3 = _dot(xe, w3_ref[...].astype(BF16))
        a = (h1 * _sigmoid(h1) * h3).astype(BF16)
        part = _dot(a, w2_ref[...].astype(BF16))

        @pl.when(f == 0)
        def _set():
            acc_s[...] = part

        @pl.when(f != 0)
        def _add():
            acc_s[...] += part

        @pl.when(f == n_f - 1)
        def _out():
            ys_ref[...] = acc_s[...].astype(BF16)

    @pl.when((act_ref[i] == 0) & (f == n_f - 1))
    def _zero():
        ys_ref[...] = jnp.zeros(ys_ref.shape, BF16)


def _experts_call(tile_exp, tile_act, xs, w1, w3, w2, *, tm, tf, name):
    n_rows = xs.shape[0]
    n_f = D_FF // tf
    grid_spec = pltpu.PrefetchScalarGridSpec(
        num_scalar_prefetch=2,
        grid=(n_rows // tm, n_f),
        in_specs=[
            pl.BlockSpec((tm, D_MODEL), lambda i, f, ex, act: (i, 0)),
            pl.BlockSpec((None, D_MODEL, tf), lambda i, f, ex, act: (ex[i], 0, f)),
            pl.BlockSpec((None, D_MODEL, tf), lambda i, f, ex, act: (ex[i], 0, f)),
            pl.BlockSpec((None, tf, D_MODEL), lambda i, f, ex, act: (ex[i], f, 0)),
        ],
        out_specs=pl.BlockSpec((tm, D_MODEL), lambda i, f, ex, act: (i, 0)),
        scratch_shapes=[pltpu.VMEM((tm, D_MODEL), F32)],
    )
    return pl.pallas_call(
        functools.partial(_experts_kernel, n_f=n_f),
        grid_spec=grid_spec,
        out_shape=jax.ShapeDtypeStruct((n_rows, D_MODEL), BF16),
        compiler_params=pltpu.CompilerParams(dimension_semantics=("arbitrary", "arbitrary"),
                                             vmem_limit_bytes=VMEM_LIMIT),
        name=name,
    )(tile_exp, tile_act, xs, w1, w3, w2)


def _combine_kernel(blk_ref, tile_ref, exp_ref, off_ref, first_ref, last_ref, act_ref,
                    x_ref, rank_c_ref, comb_ref, ys_ref, fg_ref, o_ref, *, tb, final_norm):
    del blk_ref, tile_ref
    p = pl.program_id(0)

    @pl.when(first_ref[p] == 1)
    def _start():
        o_ref[...] = x_ref[...]

    @pl.when(act_ref[p] == 1)
    def _():
        pick = lax.broadcasted_iota(jnp.int32, (tb, HP), 1) == exp_ref[p]
        rcol = jnp.sum(jnp.where(pick, rank_c_ref[...], 0.0), axis=1, keepdims=True)
        gcol = jnp.sum(jnp.where(pick, comb_ref[...], 0.0), axis=1, keepdims=True)
        slot = (lax.broadcasted_iota(jnp.int32, (tb, GATHER_TILE), 1) - off_ref[p]).astype(F32)
        onehot = jnp.where(rcol == slot, 1.0, 0.0).astype(BF16)
        o_ref[...] += gcol * _dot(onehot, ys_ref[...])

    if final_norm:
        @pl.when(last_ref[p] == 1)
        def _norm():
            y = o_ref[...]
            ms = jnp.mean(y * y, axis=-1, keepdims=True)
            o_ref[...] = y * lax.rsqrt(ms + EPS) * fg_ref[...]


def _combine_call(pairs, xall, rank_c, comb, ys, fg, *, tb, final_norm, name):
    n_pairs = pairs[0].shape[0]
    grid_spec = pltpu.PrefetchScalarGridSpec(
        num_scalar_prefetch=7,
        grid=(n_pairs,),
        in_specs=[
            pl.BlockSpec((tb, D_MODEL), lambda p, blk, *_: (blk[p], 0)),
            pl.BlockSpec((tb, HP), lambda p, blk, *_: (blk[p], 0)),
            pl.BlockSpec((tb, HP), lambda p, blk, *_: (blk[p], 0)),
            pl.BlockSpec((GATHER_TILE, D_MODEL), lambda p, blk, tile, *_: (tile[p], 0)),
            pl.BlockSpec((1, D_MODEL), lambda p, *_: (0, 0)),
        ],
        out_specs=pl.BlockSpec((tb, D_MODEL), lambda p, blk, *_: (blk[p], 0)),
    )
    return pl.pallas_call(
        functools.partial(_combine_kernel, tb=tb, final_norm=final_norm),
        grid_spec=grid_spec,
        out_shape=jax.ShapeDtypeStruct(xall.shape, F32),
        input_output_aliases={7: 0},
        compiler_params=pltpu.CompilerParams(dimension_semantics=("arbitrary",), vmem_limit_bytes=VMEM_LIMIT),
        name=name,
    )(*pairs, xall, rank_c, comb, ys, fg)


def _pair_lists(cnt, *, tm, n_rows):
    nb = cnt.shape[0]
    i32 = jnp.int32
    tot = jnp.sum(cnt, axis=0)
    grp_rows = (tot + tm - 1) // tm * tm
    grp_end = jnp.cumsum(grp_rows)
    grp_start = grp_end - grp_rows
    base = grp_start[None, :] + jnp.cumsum(cnt, axis=0) - cnt
    lo = base // GATHER_TILE
    hi = (base + cnt - 1) // GATHER_TILE
    npair = jnp.where(cnt > 0, hi - lo + 1, 0)
    n_pairs = nb * N_EXPERTS + n_rows // GATHER_TILE
    blk_id = jnp.broadcast_to(jnp.arange(nb, dtype=i32)[:, None], (nb, N_EXPERTS))
    exp_id = jnp.broadcast_to(jnp.arange(N_EXPERTS, dtype=i32)[None, :], (nb, N_EXPERTS))

    def expand(order):
        np_k, lo_k, base_k = npair.reshape(-1)[order], lo.reshape(-1)[order], base.reshape(-1)[order]
        blk_k, exp_k = blk_id.reshape(-1)[order], exp_id.reshape(-1)[order]
        cum = jnp.cumsum(np_k)
        total = cum[-1]
        p = jnp.arange(n_pairs, dtype=i32)
        pc = jnp.minimum(p, total - 1)
        k = jnp.searchsorted(cum, pc, side='right').astype(i32)
        tile = lo_k[k] + pc - (cum[k] - np_k[k])
        act = (p < total).astype(i32)
        return tile.astype(i32), blk_k[k], exp_k[k], (base_k[k] - tile * GATHER_TILE).astype(i32), act

    keys = jnp.arange(nb * N_EXPERTS, dtype=i32).reshape(nb, N_EXPERTS)
    g_tile, g_blk, g_exp, g_off, g_act = expand(keys.T.reshape(-1))
    g_first = jnp.concatenate([jnp.ones((1,), i32), (g_tile[1:] != g_tile[:-1]).astype(i32)])
    c_tile, c_blk, c_exp, c_off, c_act = expand(keys.reshape(-1))
    c_first = jnp.concatenate([jnp.ones((1,), i32), (c_blk[1:] != c_blk[:-1]).astype(i32)])
    c_last = jnp.concatenate([(c_blk[1:] != c_blk[:-1]).astype(i32), jnp.ones((1,), i32)])
    c_last = jnp.where(jnp.arange(n_pairs) == jnp.sum(c_act) - 1, 1, c_last) * c_act
    n_tiles = n_rows // tm
    t0 = jnp.arange(n_tiles, dtype=i32) * tm
    t_act = (t0 < grp_end[-1]).astype(i32)
    t_exp = jnp.searchsorted(grp_end, jnp.minimum(t0, grp_end[-1] - 1), side='right').astype(i32)
    return ((g_tile, g_blk, g_exp, g_off, g_first, g_act),
            (c_blk, c_tile, c_exp, c_off, c_first, c_last, c_act), (t_exp, t_act))


def _moe_layer(xall, g, wr, w1, w3, w2, fg, *, tb, tm, tf, final_norm, tag):
    n = xall.shape[0]
    n_rows = 2 * n + N_EXPERTS * tm
    hn, rank_t, rank_c, comb, cnt = _route_call(xall, g, wr, tb=tb, name=f"moe_route_{tag}")
    cnt = cnt[:, 0, 0:N_EXPERTS].astype(jnp.int32)
    g_pairs, c_pairs, (t_exp, t_act) = _pair_lists(cnt, tm=tm, n_rows=n_rows)
    xs = _gather_call(g_pairs, rank_t, hn, n_rows, tb=tb, name=f"moe_gather_{tag}")
    ys = _experts_call(t_exp, t_act, xs, w1, w3, w2, tm=tm, tf=tf, name=f"moe_experts_{tag}")
    return _combine_call(c_pairs, xall, rank_c, comb, ys, fg, tb=tb, final_norm=final_norm,
                         name=f"moe_combine_{tag}")


def _pad_heads(a, axis):
    axis = axis % a.ndim
    shp = a.shape
    a = a.reshape(shp[:axis] + (NH, DH) + shp[axis + 1:])
    padw = [(0, 0)] * a.ndim
    padw[axis + 1] = (0, HP - DH)
    a = jnp.pad(a, padw)
    return a.reshape(shp[:axis] + (HW,) + shp[axis + 1:])


def _unpad_heads(a, axis):
    axis = axis % a.ndim
    shp = a.shape
    a = a.reshape(shp[:axis] + (NH, HP) + shp[axis + 1:])
    a = lax.slice_in_dim(a, 0, DH, axis=axis + 1)
    return a.reshape(shp[:axis] + (NH * DH,) + shp[axis + 1:])


def _prep_weights(norm1_g, w_in, b_igate, b_fgate, conv_w, mlstm_norm_g, pool_w, pool_scale, lb_logits,
                  hgrn_norm_g, w_out):
    m_w, h_kw = NH * DH, HW
    widths = (m_w, m_w, m_w, m_w, NH, NH, P_W, h_kw, h_kw, m_w, m_w)
    pts, acc = [], 0
    for wd in widths[:-1]:
        acc += wd
        pts.append(acc)
    mq, mk, mv, mo, mi, mf, pu, hq, hf, hi, hg = jnp.split(w_in, pts, axis=-1)
    gates = jnp.pad(jnp.concatenate([mi, mf], axis=-1), ((0, 0), (0, 0), (0, HP - 2 * NH)))
    w_in_p = jnp.concatenate([_pad_heads(mq, -1), _pad_heads(mk, -1), _pad_heads(mv, -1), _pad_heads(mo, -1),
                              gates, pu, hq, hf, _pad_heads(hi, -1), _pad_heads(hg, -1)], axis=-1)
    w_out_p = jnp.concatenate([_pad_heads(w_out[:, 0:m_w], 1), w_out[:, m_w:m_w + P_W],
                               _pad_heads(w_out[:, m_w + P_W:], 1)], axis=1)
    gbias = jnp.pad(jnp.concatenate([b_igate, b_fgate], axis=-1), ((0, 0), (0, HP - 2 * NH)))[:, None, :]
    conv_p = jnp.concatenate([_pad_heads(conv_w[..., 0:m_w], -1), _pad_heads(conv_w[..., m_w:], -1)], axis=-1)
    eye = jnp.eye(len(POOL_WINDOWS), dtype=F32)
    pool_bd = jnp.einsum('lgce,gh->lgche', pool_w, eye).reshape(DEPTH, P_W, P_W)
    return {
        "g1": norm1_g[:, None, :], "w_in": w_in_p.astype(BF16), "w_in_f32": w_in_p, "gbias": gbias,
        "conv_w": conv_p, "mnorm": _pad_heads(mlstm_norm_g, -1)[:, None, :],
        "pool_w": pool_bd.astype(BF16), "pool_w_f32": pool_bd,
        "pool_scale": pool_scale[:, None, :], "lb_logits": lb_logits,
        "hnorm": _pad_heads(hgrn_norm_g, -1)[:, None, :], "w_out": w_out_p.astype(BF16), "w_out_f32": w_out_p,
    }


def _states_to_kernel(C, n, m, conv, pool, S):
    nb = C.shape[0]
    caug = jnp.concatenate([C, n[..., None]], axis=-1)
    caug = jnp.pad(caug, ((0, 0), (0, 0), (0, HP - DH), (0, HP - DH - 1)))
    mk = jnp.pad(jnp.broadcast_to(m[:, :, None], (nb, NH, HP)), ((0, 0), (0, 8 - NH), (0, 0)))
    m_w = NH * DH
    convk = jnp.concatenate([_pad_heads(conv[..., 0:m_w], -1), _pad_heads(conv[..., m_w:], -1)], axis=-1)
    convk = jnp.pad(convk, ((0, 0), (CONV_HDR - 3, 0), (0, 0)))
    poolk = jnp.pad(pool, ((0, 0), (1, 0), (0, 0)))
    sk = jnp.pad(jnp.swapaxes(S, -1, -2), ((0, 0), (0, 0), (0, HP - DH), (0, 0)))
    return caug, mk, convk, poolk, sk


def _states_from_kernel(st):
    caug, mk, convk, poolk, sk = st
    m_w = NH * DH
    C = caug[:, :, 0:DH, 0:DH]
    n = caug[:, :, 0:DH, DH]
    m = mk[:, 0:NH, 0]
    conv = convk[:, CONV_HDR - 3:, :]
    conv = jnp.concatenate([_unpad_heads(conv[..., 0:HW], -1), _unpad_heads(conv[..., HW:], -1)], axis=-1)
    pool = poolk[:, 1:, :]
    S = jnp.swapaxes(sk[:, :, 0:DH, :], -1, -2)
    return C, n, m, conv, pool, S


def _pick_tile(n, candidates):
    for c in candidates:
        if n % c == 0:
            return c
    raise ValueError(f"no row tile for {n}")


def kernel(x_prompt, x_sample, state_mlstm_C, state_mlstm_n, state_mlstm_m, state_mlstm_conv, state_pool,
           state_hgrn, meta_tokens, norm1_g, norm2_g, final_g, w_in, b_igate, b_fgate, conv_w, mlstm_norm_g,
           pool_w, pool_scale, lb_logits, hgrn_norm_g, w_out, ffn_w1, ffn_w3, ffn_w2, router_w, moe_w1,
           moe_w3, moe_w2):
    B, T, _ = x_prompt.shape
    SB, ST, _ = x_sample.shape
    w = _prep_weights(norm1_g, w_in, b_igate, b_fgate, conv_w, mlstm_norm_g, pool_w, pool_scale, lb_logits,
                      hgrn_norm_g, w_out)
    ffn_w1b, ffn_w3b, ffn_w2b = ffn_w1.astype(BF16), ffn_w3.astype(BF16), ffn_w2.astype(BF16)
    router_p = jnp.pad(router_w, ((0, 0), (0, 0), (0, HP - N_EXPERTS)))

    n_main = B * T
    off_s = n_main
    off_m = off_s + SB * ST
    assert SB * ST == TAIL and n_main % TAIL == 0
    n_tok = off_m + META_TOKENS
    tile = 1280 if n_main >= 16384 else 256
    expert_tile = 1024 if n_main >= 16384 else 256
    n_pad = -(-n_tok // tile) * tile
    xall = jnp.concatenate([x_prompt.reshape(n_main, D_MODEL), x_sample.reshape(SB * ST, D_MODEL),
                            meta_tokens.astype(F32), jnp.zeros((n_pad - n_tok, D_MODEL), F32)], axis=0)
    tv_main = _pick_tile(T, (256, 128))

    zero_states = (jnp.zeros((1, NH, HP, HP), F32), jnp.zeros((1, 8, HP), F32),
                   jnp.zeros((1, CONV_HDR, 2 * HW), F32), jnp.zeros((1, 16, P_W), F32),
                   jnp.zeros((1, NH, HP, HP), F32))
    p_states, s_states = [], []
    tail_blocks = tuple((b * T + T - TAIL) // TAIL for b in range(B)) + (off_s // TAIL,)
    for l in range(DEPTH):
        precise = l < PRECISE_LAYERS
        xall, st_meta = _mixer_call(xall, zero_states, True, w, l, row_off=off_m, seq_stride=META_TOKENS, nb=1,
                                    seq=META_TOKENS, tv=META_TOKENS, pos0=0, precise=False,
                                    name=f"mixer_meta_{l}")
        if precise:
            xall, st_p = _mixer_call(xall, st_meta, True, w, l, row_off=0, seq_stride=T, nb=B, seq=T - TAIL,
                                     tv=tv_main, pos0=META_TOKENS, precise=False, name=f"mixer_prompt_{l}")
            xall, st_p = _mixer_call(xall, st_p, False, w, l, row_off=T - TAIL, seq_stride=T, nb=B, seq=TAIL,
                                     tv=TAIL, pos0=META_TOKENS + T - TAIL, precise=True,
                                     name=f"mixer_prompt_tail_{l}")
        else:
            xall, st_p = _mixer_call(xall, st_meta, True, w, l, row_off=0, seq_stride=T, nb=B, seq=T,
                                     tv=tv_main, pos0=META_TOKENS, precise=False, name=f"mixer_prompt_{l}")
        st_in = _states_to_kernel(state_mlstm_C[l], state_mlstm_n[l], state_mlstm_m[l], state_mlstm_conv[l],
                                  state_pool[l], state_hgrn[l])
        xall, st_s = _mixer_call(xall, st_in, False, w, l, row_off=off_s, seq_stride=ST, nb=SB, seq=ST, tv=ST,
                                 pos0=META_TOKENS + PAST_LEN, precise=True, name=f"mixer_sample_{l}")
        p_states.append(_states_from_kernel(st_p))
        s_states.append(_states_from_kernel(st_s))
        i = l // 2
        if l % 2 == 0:
            f32_blocks = tail_blocks if precise else tail_blocks[-1:]
            rows = _ffn_rows_f32_call(xall, f32_blocks, norm2_g[l][None, :], ffn_w1[i], ffn_w3[i], ffn_w2[i],
                                      tm=TAIL, tf=512, name=f"ffn_rows_f32_{l}")
            xall = _ffn_call(xall, norm2_g[l][None, :], ffn_w1b[i], ffn_w3b[i], ffn_w2b[i],
                             tm=tile, tf=512, name=f"ffn_{l}")
            for j, blk in enumerate(f32_blocks):
                xall = lax.dynamic_update_slice(xall, rows[j * TAIL:(j + 1) * TAIL], (blk * TAIL, 0))
        else:
            xall = _moe_layer(xall, norm2_g[l][None, :], router_p[i], moe_w1[i], moe_w3[i], moe_w2[i],
                              final_g[None, :], tb=tile, tm=expert_tile, tf=512, final_norm=(l == DEPTH - 1),
                              tag=str(l))
    y_prompt = xall[0:n_main].reshape(B, T, D_MODEL)
    y_sample = xall[off_s:off_m].reshape(SB, ST, D_MODEL)
    p_out = tuple(jnp.stack([s[j] for s in p_states], axis=0) for j in range(6))
    s_out = tuple(jnp.stack([s[j] for s in s_states], axis=0) for j in range(6))
    return (y_prompt, y_sample) + p_out + s_out
```

```python
import functools

import jax
import jax.numpy as jnp
from jax import lax
from jax.experimental import pallas as pl
from jax.experimental.pallas import tpu as pltpu

F32 = jnp.float32
BF16 = jnp.bfloat16

D_MODEL = 1024
DEPTH = 4
META_TOKENS = 16
PAST_LEN = 4096
EPS = 1e-6
NH = 4
DH = 96
HP = 128
HW = NH * HP
P_W = 256
P_GW = 64
POOL_WINDOWS = (2, 4, 8, 16)
D_FF = 3584
N_EXPERTS = 8
NEG = -1e30

OFF_MQ, OFF_MK, OFF_MV, OFF_MO = 0, 512, 1024, 1536
OFF_G = 2048
OFF_PU = 2176
OFF_HQ, OFF_HF, OFF_HI, OFF_HG = 2432, 2944, 3456, 3968
D_INP = 4480
OFF_YM, OFF_YP, OFF_YH = 0, 512, 768
D_MIXP = 1280

M_CHUNK = 128
H_CHUNK = 64
H_SUB = 16
CONV_HDR = 8
POOL_HDR = 32

TAIL = 256
PRECISE_LAYERS = 2

VMEM_LIMIT = 60 * 1024 * 1024


def _sigmoid(x):
    return 1.0 / (1.0 + jnp.exp(-x))


def _dot(a, b, precision=None):
    return jnp.dot(a, b, preferred_element_type=F32, precision=precision)


def _dot_nt(a, b, precision=None):
    return lax.dot_general(a, b, (((1,), (1,)), ((), ())), preferred_element_type=F32, precision=precision)


def _dot_tn(a, b, precision=None):
    return lax.dot_general(a, b, (((0,), (0,)), ((), ())), preferred_element_type=F32, precision=precision)


def _cumsum_rows(tril_bf, x, precise=False):
    if precise:
        return _dot(tril_bf.astype(F32), x, precision=lax.Precision.HIGHEST)
    n = x.shape[1]
    hi = x.astype(BF16)
    lo = (x - hi.astype(F32)).astype(BF16)
    both = _dot(tril_bf, jnp.concatenate([hi, lo], axis=1))
    return both[:, :n] + both[:, n:]


def _bcast_rows(row, n):
    return jnp.broadcast_to(row, (n, row.shape[1]))


def _mixer_kernel(x_ref, c0_ref, m0_ref, conv0_ref, pool0_ref, s0_ref,
                  g1_ref, win_ref, gb_ref, cw_ref, mng_ref, pw_ref, ps_ref, lbl_ref, hng_ref, wout_ref,
                  xo_ref, cf_ref, mf_ref, convf_ref, poolf_ref, sf_ref,
                  proj_s, qk_s, u_s, s2_s, s4_s, s8_s, hk_s, gate_s, y_s, c_s, m_s, st_s,
                  *, layer, tv, lb, pos0, n_t, precise):
    pad = lb - tv
    t = pl.program_id(1)
    prec = lax.Precision.HIGHEST if precise else None
    dot = functools.partial(_dot, precision=prec)
    dot_nt = functools.partial(_dot_nt, precision=prec)
    dot_tn = functools.partial(_dot_tn, precision=prec)

    def mm(a):
        return a if precise else a.astype(BF16)

    @pl.when(t == 0)
    def _init():
        c_s[...] = c0_ref[...]
        m_s[...] = m0_ref[...]
        st_s[...] = s0_ref[...]
        qk_s[...] = jnp.zeros(qk_s.shape, F32)
        qk_s[pad:pad + CONV_HDR, :] = conv0_ref[...]
        u_s[...] = jnp.zeros(u_s.shape, F32)
        u_s[pad + 16:pad + 32, :] = pool0_ref[...]
        s2_s[0:POOL_HDR, :] = jnp.zeros((POOL_HDR, P_W), F32)
        s4_s[0:POOL_HDR, :] = jnp.zeros((POOL_HDR, P_W), F32)
        s8_s[0:POOL_HDR, :] = jnp.zeros((POOL_HDR, P_W), F32)
        if pad:
            proj_s[0:pad, :] = jnp.zeros((pad, D_INP), F32)

    x = x_ref[...]
    ms = jnp.mean(x * x, axis=-1, keepdims=True)
    hn = mm(x * lax.rsqrt(ms + EPS) * g1_ref[...])
    qk_s[CONV_HDR + pad:CONV_HDR + lb, :] = dot(hn, win_ref[:, 0:OFF_MV])
    proj_s[pad:lb, OFF_MV:OFF_PU] = dot(hn, win_ref[:, OFF_MV:OFF_PU])
    u_s[POOL_HDR + pad:POOL_HDR + lb, :] = dot(hn, win_ref[:, OFF_PU:OFF_HQ])
    proj_s[pad:lb, OFF_HQ:D_INP] = dot(hn, win_ref[:, OFF_HQ:D_INP])

    row = lax.broadcasted_iota(jnp.int32, (lb, HP), 0)
    lane = lax.broadcasted_iota(jnp.int32, (lb, HP), 1)

    acc = qk_s[5:5 + lb, :] * cw_ref[0:1, :]
    for j in range(1, 4):
        acc = acc + qk_s[5 + j:5 + j + lb, :] * cw_ref[j:j + 1, :]
    qk = acc * _sigmoid(acc)
    proj_s[:, OFF_MQ:OFF_MK] = qk[:, 0:HW]
    proj_s[:, OFF_MK:OFF_MV] = qk[:, HW:2 * HW] * (DH ** -0.5)
    conv_tail = qk_s[lb:lb + CONV_HDR, :]
    qk_s[0:CONV_HDR, :] = conv_tail

    gpre = proj_s[:, OFF_G:OFF_G + HP] + gb_ref[...]
    lsig = jnp.minimum(gpre, 0.0) - jnp.log(1.0 + jnp.exp(-jnp.abs(gpre)))
    gates = jnp.where(lane < NH, gpre, jnp.where(lane < 2 * NH, lsig, 0.0))
    if pad:
        gates = jnp.where(row >= pad, gates, jnp.where(lane < NH, NEG, 0.0))
    gate_s[...] = gates

    rr = lax.broadcasted_iota(jnp.int32, (M_CHUNK, M_CHUNK), 0)
    cc = lax.broadcasted_iota(jnp.int32, (M_CHUNK, M_CHUNK), 1)
    causal = rr >= cc
    tril_m = jnp.where(causal, 1.0, 0.0).astype(BF16)
    lane_c = lax.broadcasted_iota(jnp.int32, (M_CHUNK, HP), 1)

    for c in range(lb // M_CHUNK):
        rows = slice(c * M_CHUNK, (c + 1) * M_CHUNK)
        gt = gate_s[rows, :]
        bcum = _cumsum_rows(tril_m, jnp.where(lane_c >= NH, gt, 0.0), precise)
        cg = gt - pltpu.roll(bcum, HP - NH, axis=1)
        cg_t = cg.T
        for h in range(NH):
            q = proj_s[rows, OFF_MQ + h * HP:OFF_MQ + (h + 1) * HP]
            k = proj_s[rows, OFF_MK + h * HP:OFF_MK + (h + 1) * HP]
            v = proj_s[rows, OFF_MV + h * HP:OFF_MV + (h + 1) * HP]
            c_row = cg_t[h:h + 1, :]
            c_col = cg[:, h:h + 1]
            b_col = bcum[:, NH + h:NH + h + 1]
            m_prev = m_s[h:h + 1, 0:1]
            mx = jnp.maximum(jnp.max(jnp.where(causal, c_row, NEG), axis=1, keepdims=True), m_prev)
            w = jnp.exp(jnp.where(causal, c_row - mx, NEG))
            w_int = jnp.exp(m_prev - mx)
            mx_last = mx[M_CHUNK - 1:M_CHUNK, :]
            s = dot_nt(mm(q), mm(k)) * w
            vaug = jnp.where(lane_c == DH, 1.0, v)
            caug = c_s[h]
            lhs = mm(jnp.concatenate([s, q * w_int], axis=1))
            rhs = mm(jnp.concatenate([vaug, caug], axis=0))
            nd = dot(lhs, rhs)
            den = nd[:, DH:DH + 1]
            rden = 1.0 / jnp.maximum(jnp.abs(den), jnp.exp(-(b_col + mx)))
            wl = jnp.exp(c_col - mx_last)
            decay = jnp.exp(m_prev - mx_last)
            c_s[h] = decay * caug + dot_tn(mm(k * wl), mm(vaug))
            m_s[h:h + 1, :] = jnp.broadcast_to(b_col[M_CHUNK - 1:M_CHUNK, :] + mx_last, (1, HP))
            mo = proj_s[rows, OFF_MO + h * HP:OFF_MO + (h + 1) * HP]
            z = jnp.where(lane_c < DH, nd * _sigmoid(mo), 0.0)
            ssq = jnp.sum(z * z, axis=1, keepdims=True) * (1.0 / DH)
            fac = rden * lax.rsqrt(rden * rden * ssq + EPS)
            y_s[rows, OFF_YM + h * HP:OFF_YM + (h + 1) * HP] = mm(z * fac * mng_ref[:, h * HP:(h + 1) * HP])

    n_ext = lb + 16
    s2_s[16:16 + n_ext, :] = u_s[16:16 + n_ext, :] + u_s[15:15 + n_ext, :]
    s4_s[16:16 + n_ext, :] = s2_s[16:16 + n_ext, :] + s2_s[14:14 + n_ext, :]
    s8_s[16:16 + n_ext, :] = s4_s[16:16 + n_ext, :] + s4_s[12:12 + n_ext, :]
    u_cur = u_s[POOL_HDR:POOL_HDR + lb, :]
    w2 = s2_s[POOL_HDR:POOL_HDR + lb, :]
    w4 = s4_s[POOL_HDR:POOL_HDR + lb, :]
    w8 = s8_s[POOL_HDR:POOL_HDR + lb, :]
    w16 = w8 + s8_s[POOL_HDR - 8:POOL_HDR - 8 + lb, :]
    lane_p = lax.broadcasted_iota(jnp.int32, (lb, P_W), 1)
    wsum = jnp.where(lane_p < P_GW, w2, jnp.where(lane_p < 2 * P_GW, w4, jnp.where(lane_p < 3 * P_GW, w8, w16)))
    if pos0 >= POOL_WINDOWS[-1] - 1:
        inv = jnp.where(lane_p < P_GW, 0.5, jnp.where(lane_p < 2 * P_GW, 0.25,
                                                       jnp.where(lane_p < 3 * P_GW, 0.125, 0.0625)))
        mean = wsum * inv
    else:
        row_p = lax.broadcasted_iota(jnp.int32, (lb, P_W), 0)
        posn = (row_p + (pos0 + 1 - pad + t * tv)).astype(F32)
        wlen = jnp.where(lane_p < P_GW, 2.0, jnp.where(lane_p < 2 * P_GW, 4.0,
                                                        jnp.where(lane_p < 3 * P_GW, 8.0, 16.0)))
        mean = wsum / jnp.maximum(jnp.minimum(wlen, posn), 1.0)
    pooled = mm(mean - u_cur)
    y_s[:, OFF_YP:OFF_YH] = mm(dot(pooled, pw_ref[...]) * ps_ref[...])
    pool_tail = u_s[lb:lb + POOL_HDR, :]
    u_s[0:POOL_HDR, :] = pool_tail

    lbl = lbl_ref[...]
    e = jnp.exp(lbl - jnp.max(lbl, axis=0, keepdims=True))
    p = e / jnp.sum(e, axis=0, keepdims=True)
    lbv = jnp.sum(p[0:layer + 1, :], axis=0, keepdims=True) - p[0:1, :]
    hq = proj_s[:, OFF_HQ:OFF_HF]
    hf = proj_s[:, OFF_HF:OFF_HI]
    fg = lbv + (1.0 - lbv) * _sigmoid(hf)
    kh = 1.0 - fg
    lg = jnp.log(fg)
    if pad:
        row_h = lax.broadcasted_iota(jnp.int32, (lb, HW), 0)
        kh = jnp.where(row_h >= pad, kh, 0.0)
        lg = jnp.where(row_h >= pad, lg, 0.0)
    proj_s[:, OFF_HQ:OFF_HF] = hq * _sigmoid(hq)
    proj_s[:, OFF_HF:OFF_HI] = lg
    hk_s[...] = kh

    r64 = lax.broadcasted_iota(jnp.int32, (H_CHUNK, H_CHUNK), 0)
    c64 = lax.broadcasted_iota(jnp.int32, (H_CHUNK, H_CHUNK), 1)
    tril_h = jnp.where(r64 >= c64, 1.0, 0.0).astype(BF16)
    bdiff = r64 // H_SUB - c64 // H_SUB
    mask_intra = (bdiff == 0) & (r64 >= c64)
    n_sub = H_CHUNK // H_SUB

    for c in range(lb // H_CHUNK):
        rows = slice(c * H_CHUNK, (c + 1) * H_CHUNK)
        g = _cumsum_rows(tril_h, proj_s[rows, OFF_HF:OFF_HI], precise)
        qh = proj_s[rows, OFF_HQ:OFF_HF]
        khc = hk_s[rows, :]
        bnd = [jnp.zeros((1, HW), F32)] + [g[(j + 1) * H_SUB - 1:(j + 1) * H_SUB, :] for j in range(n_sub)]
        g_start = jnp.concatenate([_bcast_rows(bnd[j], H_SUB) for j in range(n_sub)], axis=0)
        g_end = jnp.concatenate([_bcast_rows(bnd[j + 1], H_SUB) for j in range(n_sub)], axis=0)
        g_last = bnd[n_sub]
        qt = qh * jnp.exp(g - g_start)
        khat = khc * jnp.exp(g_end - g)
        kbar = khc * jnp.exp(g_start - g)
        qg = qh * jnp.exp(g)
        kend = khc * jnp.exp(g_last - g)
        dsub = [jnp.exp(bnd[j + 1] - bnd[j]) for j in range(n_sub - 1)]
        ones = jnp.ones((H_SUB, HW), F32)
        qlev = [qt]
        for d in range(1, n_sub - 1):
            fac = jnp.concatenate([ones] * d + [_bcast_rows(dsub[j - d], H_SUB) for j in range(d, n_sub)], axis=0)
            qlev.append(qlev[-1] * fac)
        for h in range(NH):
            sl = slice(h * HP, (h + 1) * HP)
            lhs = mm(jnp.concatenate([ql[:, sl] for ql in qlev], axis=0))
            inter = dot_nt(lhs, mm(khat[:, sl]))
            intra = dot_nt(mm(qt[:, sl]), mm(kbar[:, sl]))
            att = jnp.where(mask_intra, intra, 0.0)
            for d in range(n_sub - 1):
                att = att + jnp.where(bdiff == d + 1, inter[d * H_CHUNK:(d + 1) * H_CHUNK, :], 0.0)
            iv = mm(proj_s[rows, OFF_HI + h * HP:OFF_HI + (h + 1) * HP])
            st = st_s[h]
            o = dot(mm(att), iv) + dot_nt(mm(qg[:, sl]), mm(st))
            st_s[h] = st * jnp.exp(g_last[:, sl]) + dot_tn(iv, mm(kend[:, sl]))
            msq = jnp.sum(o * o, axis=1, keepdims=True) * (1.0 / DH)
            on = o * lax.rsqrt(msq + EPS) * hng_ref[:, sl]
            hg = proj_s[rows, OFF_HG + h * HP:OFF_HG + (h + 1) * HP]
            y_s[rows, OFF_YH + h * HP:OFF_YH + (h + 1) * HP] = mm(on * (hg * _sigmoid(hg)))

    out = dot(y_s[...], wout_ref[...])
    xo_ref[...] = x + out[pad:lb, :]

    @pl.when(t == n_t - 1)
    def _final():
        cf_ref[...] = c_s[...]
        mf_ref[...] = m_s[...]
        convf_ref[...] = conv_tail
        poolf_ref[...] = pool_tail[16:32, :]
        sf_ref[...] = st_s[...]


def _mixer_call(xall, states, shared_init, w, layer, *, row_off, seq_stride, nb, seq, tv, pos0, precise, name,
                stacked_states=False):
    lb = max(tv, M_CHUNK)
    n_t = seq // tv
    assert row_off % tv == 0 and seq_stride % tv == 0 and seq % tv == 0
    blk0, blk_stride = row_off // tv, seq_stride // tv
    c0, m0, conv0, pool0, s0 = states
    lead_blk = (None,) if stacked_states else ()
    lead_idx = (layer,) if stacked_states else ()

    def x_map(b, t):
        return (blk0 + b * blk_stride + t, 0)

    def st_map(b, t):
        return lead_idx + (0 if shared_init else b, 0, 0, 0)

    def st_map3(b, t):
        return lead_idx + (0 if shared_init else b, 0, 0)

    def const_spec(shape):
        return pl.BlockSpec(shape, lambda b, t: (0,) * len(shape), pipeline_mode=pl.Buffered(1))

    in_specs = [
        pl.BlockSpec((tv, D_MODEL), x_map),
        pl.BlockSpec(lead_blk + (None, NH, HP, HP), st_map),
        pl.BlockSpec(lead_blk + (None, 8, HP), st_map3),
        pl.BlockSpec(lead_blk + (None, CONV_HDR, 2 * HW), st_map3),
        pl.BlockSpec(lead_blk + (None, 16, P_W), st_map3),
        pl.BlockSpec(lead_blk + (None, NH, HP, HP), st_map),
        const_spec((1, D_MODEL)),
        const_spec((D_MODEL, D_INP)),
        const_spec((1, HP)),
        const_spec((4, 2 * HW)),
        const_spec((1, HW)),
        const_spec((P_W, P_W)),
        const_spec((1, P_W)),
        const_spec((DEPTH, HW)),
        const_spec((1, HW)),
        const_spec((D_MIXP, D_MODEL)),
    ]
    out_specs = [
        pl.BlockSpec((tv, D_MODEL), x_map),
        pl.BlockSpec((None, NH, HP, HP), lambda b, t: (b, 0, 0, 0)),
        pl.BlockSpec((None, 8, HP), lambda b, t: (b, 0, 0)),
        pl.BlockSpec((None, CONV_HDR, 2 * HW), lambda b, t: (b, 0, 0)),
        pl.BlockSpec((None, 16, P_W), lambda b, t: (b, 0, 0)),
        pl.BlockSpec((None, NH, HP, HP), lambda b, t: (b, 0, 0, 0)),
    ]
    out_shape = [
        jax.ShapeDtypeStruct(xall.shape, F32),
        jax.ShapeDtypeStruct((nb, NH, HP, HP), F32),
        jax.ShapeDtypeStruct((nb, 8, HP), F32),
        jax.ShapeDtypeStruct((nb, CONV_HDR, 2 * HW), F32),
        jax.ShapeDtypeStruct((nb, 16, P_W), F32),
        jax.ShapeDtypeStruct((nb, NH, HP, HP), F32),
    ]
    scratch = [
        pltpu.VMEM((lb, D_INP), F32),
        pltpu.VMEM((CONV_HDR + lb, 2 * HW), F32),
        pltpu.VMEM((POOL_HDR + lb, P_W), F32),
        pltpu.VMEM((POOL_HDR + lb, P_W), F32),
        pltpu.VMEM((POOL_HDR + lb, P_W), F32),
        pltpu.VMEM((POOL_HDR + lb, P_W), F32),
        pltpu.VMEM((lb, HW), F32),
        pltpu.VMEM((lb, HP), F32),
        pltpu.VMEM((lb, D_MIXP), F32 if precise else BF16),
        pltpu.VMEM((NH, HP, HP), F32),
        pltpu.VMEM((8, HP), F32),
        pltpu.VMEM((NH, HP, HP), F32),
    ]
    sfx = "_f32" if precise else ""
    kern = functools.partial(_mixer_kernel, layer=layer, tv=tv, lb=lb, pos0=pos0, n_t=n_t, precise=precise)
    outs = pl.pallas_call(
        kern,
        grid=(nb, n_t),
        in_specs=in_specs,
        out_specs=out_specs,
        out_shape=out_shape,
        scratch_shapes=scratch,
        input_output_aliases={0: 0},
        compiler_params=pltpu.CompilerParams(dimension_semantics=("arbitrary", "arbitrary"),
                                             vmem_limit_bytes=VMEM_LIMIT),
        name=name,
    )(xall, c0, m0, conv0, pool0, s0,
      w["g1"][layer], w["w_in" + sfx][layer], w["gbias"][layer], w["conv_w"][layer], w["mnorm"][layer],
      w["pool_w" + sfx][layer], w["pool_scale"][layer], w["lb_logits"], w["hnorm"][layer],
      w["w_out" + sfx][layer])
    return outs[0], tuple(outs[1:])


def _ffn_kernel(x_ref, g_ref, w1_ref, w3_ref, w2_ref, o_ref, hn_s, *, precise):
    f = pl.program_id(1)
    prec = lax.Precision.HIGHEST if precise else None

    @pl.when(f == 0)
    def _start():
        x = x_ref[...]
        ms = jnp.mean(x * x, axis=-1, keepdims=True)
        hn_s[...] = (x * lax.rsqrt(ms + EPS) * g_ref[...]).astype(hn_s.dtype)
        o_ref[...] = x

    hn = hn_s[...]
    h1 = _dot(hn, w1_ref[...], prec)
    h3 = _dot(hn, w3_ref[...], prec)
    a = (h1 * _sigmoid(h1) * h3).astype(hn_s.dtype)
    o_ref[...] += _dot(a, w2_ref[...], prec)


def _ffn_rows_f32_call(xall, blocks, g, w1, w3, w2, *, tm, tf, name):
    n_f = D_FF // tf

    def x_map(i, f):
        idx = blocks[-1]
        for j in range(len(blocks) - 2, -1, -1):
            idx = jnp.where(i == j, blocks[j], idx)
        return (idx, 0)

    return pl.pallas_call(
        functools.partial(_ffn_kernel, precise=True),
        grid=(len(blocks), n_f),
        in_specs=[
            pl.BlockSpec((tm, D_MODEL), x_map),
            pl.BlockSpec((1, D_MODEL), lambda i, f: (0, 0)),
            pl.BlockSpec((D_MODEL, tf), lambda i, f: (0, f)),
            pl.BlockSpec((D_MODEL, tf), lambda i, f: (0, f)),
            pl.BlockSpec((tf, D_MODEL), lambda i, f: (f, 0)),
        ],
        out_specs=pl.BlockSpec((tm, D_MODEL), lambda i, f: (i, 0)),
        out_shape=jax.ShapeDtypeStruct((len(blocks) * tm, D_MODEL), F32),
        scratch_shapes=[pltpu.VMEM((tm, D_MODEL), F32)],
        compiler_params=pltpu.CompilerParams(dimension_semantics=("arbitrary", "arbitrary"),
                                             vmem_limit_bytes=VMEM_LIMIT),
        name=name,
    )(xall, g, w1, w3, w2)


def _ffn_call(xall, g, w1, w3, w2, *, tm, tf, name):
    n = xall.shape[0]
    n_f = D_FF // tf
    return pl.pallas_call(
        functools.partial(_ffn_kernel, precise=False),
        grid=(n // tm, n_f),
        in_specs=[
            pl.BlockSpec((tm, D_MODEL), lambda i, f: (i, 0)),
            pl.BlockSpec((1, D_MODEL), lambda i, f: (0, 0)),
            pl.BlockSpec((D_MODEL, tf), lambda i, f: (0, f)),
            pl.BlockSpec((D_MODEL, tf), lambda i, f: (0, f)),
            pl.BlockSpec((tf, D_MODEL), lambda i, f: (f, 0)),
        ],
        out_specs=pl.BlockSpec((tm, D_MODEL), lambda i, f: (i, 0)),
        out_shape=jax.ShapeDtypeStruct(xall.shape, F32),
        scratch_shapes=[pltpu.VMEM((tm, D_MODEL), BF16)],
        input_output_aliases={0: 0},
        compiler_params=pltpu.CompilerParams(dimension_semantics=("arbitrary", "arbitrary"),
                                             vmem_limit_bytes=VMEM_LIMIT),
        name=name,
    )(xall, g, w1, w3, w2)


ROW_TILE = 256


def _moe_kernel(x_ref, g_ref, wr_ref, w1_ref, w3_ref, w2_ref, fg_ref, o_ref,
                hn_s, xe_s, ye_s, rank_s, comb_s, rank_t_s, cnt_s, *, tb, n_f, final_norm):
    e = pl.program_id(1)
    f = pl.program_id(2)

    @pl.when((e == 0) & (f == 0))
    def _route():
        x = x_ref[...]
        ms = jnp.mean(x * x, axis=-1, keepdims=True)
        hn = x * lax.rsqrt(ms + EPS) * g_ref[...]
        hn_s[...] = hn.astype(BF16)
        o_ref[...] = x
        logits = jnp.dot(hn, wr_ref[...], preferred_element_type=F32, precision=lax.Precision.HIGHEST)
        lane = lax.broadcasted_iota(jnp.int32, (tb, HP), 1).astype(F32)
        lg = jnp.where(lane < N_EXPERTS, logits, NEG)
        v1 = jnp.max(lg, axis=1, keepdims=True)
        i1 = jnp.min(jnp.where(lg == v1, lane, float(HP)), axis=1, keepdims=True)
        mask1 = lane == i1
        lg2 = jnp.where(mask1, NEG, lg)
        v2 = jnp.max(lg2, axis=1, keepdims=True)
        i2 = jnp.min(jnp.where(lg2 == v2, lane, float(HP)), axis=1, keepdims=True)
        mask2 = lane == i2
        ex = jnp.exp(v2 - v1)
        ga = 1.0 / (1.0 + ex)
        comb = jnp.where(mask1, ga, 0.0) + jnp.where(mask2, ex * ga, 0.0)
        sel = mask1 | mask2
        rr = lax.broadcasted_iota(jnp.int32, (tb, tb), 0)
        cc = lax.broadcasted_iota(jnp.int32, (tb, tb), 1)
        tril_strict = jnp.where(rr > cc, 1.0, 0.0).astype(BF16)
        selb = jnp.where(sel, 1.0, 0.0)
        rank = jnp.where(sel, _dot(tril_strict, selb.astype(BF16)), -1.0)
        comb_s[...] = comb
        rank_s[...] = rank
        rank_t_s[...] = rank.T[0:N_EXPERTS, :]
        cnt = jnp.sum(selb, axis=0, keepdims=True).astype(jnp.int32)
        for j in range(N_EXPERTS):
            cnt_s[j] = cnt[0, j]

    n_rows = cnt_s[e]
    n_tiles = (n_rows + ROW_TILE - 1) // ROW_TILE

    @pl.when(f == 0)
    def _gather():
        rrow = rank_t_s[pl.ds(e, 1), :]

        def body(i, carry):
            r0 = pl.multiple_of(i * ROW_TILE, ROW_TILE)
            slot = (lax.broadcasted_iota(jnp.int32, (ROW_TILE, tb), 0) + r0).astype(F32)
            onehot = jnp.where(rrow == slot, 1.0, 0.0).astype(BF16)
            xe_s[pl.ds(r0, ROW_TILE), :] = _dot(onehot, hn_s[...]).astype(BF16)
            return carry

        lax.fori_loop(0, n_tiles, body, 0)

    def ffn_body(i, carry):
        r0 = pl.multiple_of(i * ROW_TILE, ROW_TILE)
        rows = pl.ds(r0, ROW_TILE)
        xe = xe_s[rows, :]
        h1 = _dot(xe, w1_ref[...])
        h3 = _dot(xe, w3_ref[...])
        a = (h1 * _sigmoid(h1) * h3).astype(BF16)
        part = _dot(a, w2_ref[...])

        @pl.when(f == 0)
        def _set():
            ye_s[rows, :] = part

        @pl.when(f != 0)
        def _add():
            ye_s[rows, :] += part

        return carry

    lax.fori_loop(0, n_tiles, ffn_body, 0)

    @pl.when(f == n_f - 1)
    def _scatter():
        lane = lax.broadcasted_iota(jnp.int32, (tb, HP), 1)
        pick = lane == e
        rcol = jnp.sum(jnp.where(pick, rank_s[...], 0.0), axis=1, keepdims=True)
        gcol = jnp.sum(jnp.where(pick, comb_s[...], 0.0), axis=1, keepdims=True)

        def body(i, carry):
            r0 = pl.multiple_of(i * ROW_TILE, ROW_TILE)
            slot = (lax.broadcasted_iota(jnp.int32, (tb, ROW_TILE), 1) + r0).astype(F32)
            onehot = jnp.where(rcol == slot, 1.0, 0.0).astype(BF16)
            o_ref[...] += gcol * _dot(onehot, ye_s[pl.ds(r0, ROW_TILE), :].astype(BF16))
            return carry

        lax.fori_loop(0, n_tiles, body, 0)

    if final_norm:
        @pl.when((e == N_EXPERTS - 1) & (f == n_f - 1))
        def _norm():
            y = o_ref[...]
            ms = jnp.mean(y * y, axis=-1, keepdims=True)
            o_ref[...] = y * lax.rsqrt(ms + EPS) * fg_ref[...]


def _moe_call(xall, g, wr, w1, w3, w2, fg, *, tb, tf, final_norm, name):
    n = xall.shape[0]
    n_f = D_FF // tf
    return pl.pallas_call(
        functools.partial(_moe_kernel, tb=tb, n_f=n_f, final_norm=final_norm),
        grid=(n // tb, N_EXPERTS, n_f),
        in_specs=[
            pl.BlockSpec((tb, D_MODEL), lambda i, e, f: (i, 0)),
            pl.BlockSpec((1, D_MODEL), lambda i, e, f: (0, 0)),
            pl.BlockSpec((D_MODEL, HP), lambda i, e, f: (0, 0)),
            pl.BlockSpec((None, D_MODEL, tf), lambda i, e, f: (e, 0, f)),
            pl.BlockSpec((None, D_MODEL, tf), lambda i, e, f: (e, 0, f)),
            pl.BlockSpec((None, tf, D_MODEL), lambda i, e, f: (e, f, 0)),
            pl.BlockSpec((1, D_MODEL), lambda i, e, f: (0, 0)),
        ],
        out_specs=pl.BlockSpec((tb, D_MODEL), lambda i, e, f: (i, 0)),
        out_shape=jax.ShapeDtypeStruct(xall.shape, F32),
        scratch_shapes=[
            pltpu.VMEM((tb, D_MODEL), BF16),
            pltpu.VMEM((tb, D_MODEL), BF16),
            pltpu.VMEM((tb, D_MODEL), F32),
            pltpu.VMEM((tb, HP), F32),
            pltpu.VMEM((tb, HP), F32),
            pltpu.VMEM((N_EXPERTS, tb), F32),
            pltpu.SMEM((N_EXPERTS,), jnp.int32),
        ],
        input_output_aliases={0: 0},
        compiler_params=pltpu.CompilerParams(dimension_semantics=("arbitrary", "arbitrary", "arbitrary"),
                                             vmem_limit_bytes=VMEM_LIMIT),
        name=name,
    )(xall, g, wr, w1, w3, w2, fg)


GATHER_TILE = 256
NO_SLOT = -1e9


def _route_kernel(x_ref, g_ref, wr_ref, hn_ref, rank_t_ref, rank_c_ref, comb_ref, cnt_ref, *, tb):
    x = x_ref[...]
    ms = jnp.mean(x * x, axis=-1, keepdims=True)
    hn = x * lax.rsqrt(ms + EPS) * g_ref[...]
    hn_ref[...] = hn.astype(BF16)
    logits = jnp.dot(hn, wr_ref[...], preferred_element_type=F32, precision=lax.Precision.HIGHEST)
    lane = lax.broadcasted_iota(jnp.int32, (tb, HP), 1).astype(F32)
    lg = jnp.where(lane < N_EXPERTS, logits, NEG)
    v1 = jnp.max(lg, axis=1, keepdims=True)
    i1 = jnp.min(jnp.where(lg == v1, lane, float(HP)), axis=1, keepdims=True)
    mask1 = lane == i1
    lg2 = jnp.where(mask1, NEG, lg)
    v2 = jnp.max(lg2, axis=1, keepdims=True)
    i2 = jnp.min(jnp.where(lg2 == v2, lane, float(HP)), axis=1, keepdims=True)
    mask2 = lane == i2
    ex = jnp.exp(v2 - v1)
    ga = 1.0 / (1.0 + ex)
    comb_ref[...] = jnp.where(mask1, ga, 0.0) + jnp.where(mask2, ex * ga, 0.0)
    sel = mask1 | mask2
    rr = lax.broadcasted_iota(jnp.int32, (tb, tb), 0)
    cc = lax.broadcasted_iota(jnp.int32, (tb, tb), 1)
    tril_strict = jnp.where(rr > cc, 1.0, 0.0).astype(BF16)
    selb = jnp.where(sel, 1.0, 0.0)
    rank = jnp.where(sel, _dot(tril_strict, selb.astype(BF16)), NO_SLOT)
    rank_c_ref[...] = rank
    rank_t_ref[...] = rank.T[0:N_EXPERTS, :]
    cnt_ref[...] = jnp.broadcast_to(jnp.sum(selb, axis=0, keepdims=True), (8, HP))


def _route_call(xall, g, wr, *, tb, name):
    n = xall.shape[0]
    nb = n // tb
    return pl.pallas_call(
        functools.partial(_route_kernel, tb=tb),
        grid=(nb,),
        in_specs=[
            pl.BlockSpec((tb, D_MODEL), lambda i: (i, 0)),
            pl.BlockSpec((1, D_MODEL), lambda i: (0, 0)),
            pl.BlockSpec((D_MODEL, HP), lambda i: (0, 0)),
        ],
        out_specs=[
            pl.BlockSpec((tb, D_MODEL), lambda i: (i, 0)),
            pl.BlockSpec((None, N_EXPERTS, tb), lambda i: (i, 0, 0)),
            pl.BlockSpec((tb, HP), lambda i: (i, 0)),
            pl.BlockSpec((tb, HP), lambda i: (i, 0)),
            pl.BlockSpec((None, 8, HP), lambda i: (i, 0, 0)),
        ],
        out_shape=[
            jax.ShapeDtypeStruct((n, D_MODEL), BF16),
            jax.ShapeDtypeStruct((nb, N_EXPERTS, tb), F32),
            jax.ShapeDtypeStruct((n, HP), F32),
            jax.ShapeDtypeStruct((n, HP), F32),
            jax.ShapeDtypeStruct((nb, 8, HP), F32),
        ],
        compiler_params=pltpu.CompilerParams(dimension_semantics=("arbitrary",), vmem_limit_bytes=VMEM_LIMIT),
        name=name,
    )(xall, g, wr)


def _gather_kernel(tile_ref, blk_ref, exp_ref, off_ref, first_ref, act_ref, rank_t_ref, hn_ref, init_ref, xs_ref,
                   *, tb):
    del tile_ref, blk_ref, init_ref
    p = pl.program_id(0)

    @pl.when(act_ref[p] == 1)
    def _():
        rrow = rank_t_ref[pl.ds(exp_ref[p], 1), :]
        slot = (lax.broadcasted_iota(jnp.int32, (GATHER_TILE, tb), 0) - off_ref[p]).astype(F32)
        onehot = jnp.where(rrow == slot, 1.0, 0.0).astype(BF16)
        val = _dot(onehot, hn_ref[...]).astype(BF16)

        @pl.when(first_ref[p] == 1)
        def _set():
            xs_ref[...] = val

        @pl.when(first_ref[p] == 0)
        def _add():
            xs_ref[...] += val


def _gather_call(pairs, rank_t, hn, n_rows, *, tb, name):
    n_pairs = pairs[0].shape[0]
    grid_spec = pltpu.PrefetchScalarGridSpec(
        num_scalar_prefetch=6,
        grid=(n_pairs,),
        in_specs=[
            pl.BlockSpec((None, N_EXPERTS, tb), lambda p, tile, blk, *_: (blk[p], 0, 0)),
            pl.BlockSpec((tb, D_MODEL), lambda p, tile, blk, *_: (blk[p], 0)),
            pl.BlockSpec(memory_space=pl.ANY),
        ],
        out_specs=pl.BlockSpec((GATHER_TILE, D_MODEL), lambda p, tile, *_: (tile[p], 0)),
    )
    return pl.pallas_call(
        functools.partial(_gather_kernel, tb=tb),
        grid_spec=grid_spec,
        out_shape=jax.ShapeDtypeStruct((n_rows, D_MODEL), BF16),
        input_output_aliases={8: 0},
        compiler_params=pltpu.CompilerParams(dimension_semantics=("arbitrary",), vmem_limit_bytes=VMEM_LIMIT),
        name=name,
    )(*pairs, rank_t, hn, jnp.zeros((n_rows, D_MODEL), BF16))


def _experts_kernel(exp_ref, act_ref, xs_ref, w1_ref, w3_ref, w2_ref, ys_ref, acc_s, *, n_f):
    del exp_ref
    i = pl.program_id(0)
    f = pl.program_id(1)

    @pl.when(f == 0)
    def _zero():
        acc_s[...] = jnp.zeros(acc_s.shape, F32)

    @pl.when(act_ref[i] == 1)
    def _():
        xe = xs_ref[...]
        h1 = _dot(xe, w1_ref[...].astype(BF16))
        h3 = _dot(xe, w3_ref[...].astype(BF16))
        a = (h1 * _sigmoid(h1) * h3).astype(BF16)
        acc_s[...] += _dot(a, w2_ref[...].astype(BF16))

    @pl.when(f == n_f - 1)
    def _out():
        ys_ref[...] = acc_s[...].astype(BF16)


def _experts_call(tile_exp, tile_act, xs, w1, w3, w2, *, tm, tf, name):
    n_rows = xs.shape[0]
    n_f = D_FF // tf
    grid_spec = pltpu.PrefetchScalarGridSpec(
        num_scalar_prefetch=2,
        grid=(n_rows // tm, n_f),
        in_specs=[
            pl.BlockSpec((tm, D_MODEL), lambda i, f, ex, act: (i, 0)),
            pl.BlockSpec((None, D_MODEL, tf), lambda i, f, ex, act: (ex[i], 0, f)),
            pl.BlockSpec((None, D_MODEL, tf), lambda i, f, ex, act: (ex[i], 0, f)),
            pl.BlockSpec((None, tf, D_MODEL), lambda i, f, ex, act: (ex[i], f, 0)),
        ],
        out_specs=pl.BlockSpec((tm, D_MODEL), lambda i, f, ex, act: (i, 0)),
        scratch_shapes=[pltpu.VMEM((tm, D_MODEL), F32)],
    )
    return pl.pallas_call(
        functools.partial(_experts_kernel, n_f=n_f),
        grid_spec=grid_spec,
        out_shape=jax.ShapeDtypeStruct((n_rows, D_MODEL), BF16),
        compiler_params=pltpu.CompilerParams(dimension_semantics=("arbitrary", "arbitrary"),
                                             vmem_limit_bytes=VMEM_LIMIT),
        name=name,
    )(tile_exp, tile_act, xs, w1, w3, w2)


def _combine_kernel(blk_ref, tile_ref, exp_ref, off_ref, first_ref, last_ref, act_ref,
                    x_ref, rank_c_ref, comb_ref, ys_ref, fg_ref, o_ref, *, tb, final_norm):
    del blk_ref, tile_ref
    p = pl.program_id(0)

    @pl.when(first_ref[p] == 1)
    def _start():
        o_ref[...] = x_ref[...]

    @pl.when(act_ref[p] == 1)
    def _():
        pick = lax.broadcasted_iota(jnp.int32, (tb, HP), 1) == exp_ref[p]
        rcol = jnp.sum(jnp.where(pick, rank_c_ref[...], 0.0), axis=1, keepdims=True)
        gcol = jnp.sum(jnp.where(pick, comb_ref[...], 0.0), axis=1, keepdims=True)
        slot = (lax.broadcasted_iota(jnp.int32, (tb, GATHER_TILE), 1) - off_ref[p]).astype(F32)
        onehot = jnp.where(rcol == slot, 1.0, 0.0).astype(BF16)
        o_ref[...] += gcol * _dot(onehot, ys_ref[...])

    if final_norm:
        @pl.when(last_ref[p] == 1)
        def _norm():
            y = o_ref[...]
            ms = jnp.mean(y * y, axis=-1, keepdims=True)
            o_ref[...] = y * lax.rsqrt(ms + EPS) * fg_ref[...]


def _combine_call(pairs, xall, rank_c, comb, ys, fg, *, tb, final_norm, name):
    n_pairs = pairs[0].shape[0]
    grid_spec = pltpu.PrefetchScalarGridSpec(
        num_scalar_prefetch=7,
        grid=(n_pairs,),
        in_specs=[
            pl.BlockSpec((tb, D_MODEL), lambda p, blk, *_: (blk[p], 0)),
            pl.BlockSpec((tb, HP), lambda p, blk, *_: (blk[p], 0)),
            pl.BlockSpec((tb, HP), lambda p, blk, *_: (blk[p], 0)),
            pl.BlockSpec((GATHER_TILE, D_MODEL), lambda p, blk, tile, *_: (tile[p], 0)),
            pl.BlockSpec((1, D_MODEL), lambda p, *_: (0, 0)),
        ],
        out_specs=pl.BlockSpec((tb, D_MODEL), lambda p, blk, *_: (blk[p], 0)),
    )
    return pl.pallas_call(
        functools.partial(_combine_kernel, tb=tb, final_norm=final_norm),
        grid_spec=grid_spec,
        out_shape=jax.ShapeDtypeStruct(xall.shape, F32),
        input_output_aliases={7: 0},
        compiler_params=pltpu.CompilerParams(dimension_semantics=("arbitrary",), vmem_limit_bytes=VMEM_LIMIT),
        name=name,
    )(*pairs, xall, rank_c, comb, ys, fg)


def _pair_lists(cnt, *, tm, n_rows):
    nb = cnt.shape[0]
    i32 = jnp.int32
    tot = jnp.sum(cnt, axis=0)
    grp_rows = (tot + tm - 1) // tm * tm
    grp_end = jnp.cumsum(grp_rows)
    grp_start = grp_end - grp_rows
    base = grp_start[None, :] + jnp.cumsum(cnt, axis=0) - cnt
    lo = base // GATHER_TILE
    hi = (base + cnt - 1) // GATHER_TILE
    npair = jnp.where(cnt > 0, hi - lo + 1, 0)
    n_pairs = nb * N_EXPERTS + n_rows // GATHER_TILE
    n_keys = nb * N_EXPERTS

    def expand(expert_major):
        def flat(a):
            return (a.T if expert_major else a).reshape(-1)
        np_k = flat(npair)
        cum = jnp.cumsum(np_k)
        total = cum[-1]
        p = jnp.arange(n_pairs, dtype=i32)
        pc = jnp.minimum(p, total - 1)
        k = jnp.sum((cum[None, :] <= pc[:, None]).astype(i32), axis=1)
        table = jnp.stack([flat(lo), flat(base), cum - np_k], axis=1)
        row = jnp.take(table, jnp.minimum(k, n_keys - 1), axis=0)
        tile = row[:, 0] + pc - row[:, 2]
        blk, exp = (k % nb, k // nb) if expert_major else (k // N_EXPERTS, k % N_EXPERTS)
        act = (p < total).astype(i32)
        return tile.astype(i32), blk.astype(i32), exp.astype(i32), (row[:, 1] - tile * GATHER_TILE).astype(i32), act

    g_tile, g_blk, g_exp, g_off, g_act = expand(True)
    g_first = jnp.concatenate([jnp.ones((1,), i32), (g_tile[1:] != g_tile[:-1]).astype(i32)])
    c_tile, c_blk, c_exp, c_off, c_act = expand(False)
    c_first = jnp.concatenate([jnp.ones((1,), i32), (c_blk[1:] != c_blk[:-1]).astype(i32)])
    c_last = jnp.concatenate([(c_blk[1:] != c_blk[:-1]).astype(i32), jnp.ones((1,), i32)])
    c_last = jnp.where(jnp.arange(n_pairs) == jnp.sum(c_act) - 1, 1, c_last) * c_act
    n_tiles = n_rows // tm
    t0 = jnp.arange(n_tiles, dtype=i32) * tm
    t_act = (t0 < grp_end[-1]).astype(i32)
    t_exp = jnp.sum((grp_end[None, :] <= jnp.minimum(t0, grp_end[-1] - 1)[:, None]).astype(i32), axis=1)
    return ((g_tile, g_blk, g_exp, g_off, g_first, g_act),
            (c_blk, c_tile, c_exp, c_off, c_first, c_last, c_act), (t_exp, t_act))


def _moe_layer(xall, g, wr, w1, w3, w2, fg, *, tb, tm, tf, final_norm, tag):
    n = xall.shape[0]
    n_rows = 2 * n + N_EXPERTS * tm
    hn, rank_t, rank_c, comb, cnt = _route_call(xall, g, wr, tb=tb, name=f"moe_route_{tag}")
    cnt = cnt[:, 0, 0:N_EXPERTS].astype(jnp.int32)
    g_pairs, c_pairs, (t_exp, t_act) = _pair_lists(cnt, tm=tm, n_rows=n_rows)
    xs = _gather_call(g_pairs, rank_t, hn, n_rows, tb=tb, name=f"moe_gather_{tag}")
    ys = _experts_call(t_exp, t_act, xs, w1, w3, w2, tm=tm, tf=tf, name=f"moe_experts_{tag}")
    return _combine_call(c_pairs, xall, rank_c, comb, ys, fg, tb=tb, final_norm=final_norm,
                         name=f"moe_combine_{tag}")


def _pad_heads(a, axis):
    axis = axis % a.ndim
    shp = a.shape
    a = a.reshape(shp[:axis] + (NH, DH) + shp[axis + 1:])
    padw = [(0, 0)] * a.ndim
    padw[axis + 1] = (0, HP - DH)
    a = jnp.pad(a, padw)
    return a.reshape(shp[:axis] + (HW,) + shp[axis + 1:])


def _unpad_heads(a, axis):
    axis = axis % a.ndim
    shp = a.shape
    a = a.reshape(shp[:axis] + (NH, HP) + shp[axis + 1:])
    a = lax.slice_in_dim(a, 0, DH, axis=axis + 1)
    return a.reshape(shp[:axis] + (NH * DH,) + shp[axis + 1:])


def _prep_weights(norm1_g, w_in, b_igate, b_fgate, conv_w, mlstm_norm_g, pool_w, pool_scale, lb_logits,
                  hgrn_norm_g, w_out):
    m_w, h_kw = NH * DH, HW
    widths = (m_w, m_w, m_w, m_w, NH, NH, P_W, h_kw, h_kw, m_w, m_w)
    pts, acc = [], 0
    for wd in widths[:-1]:
        acc += wd
        pts.append(acc)
    mq, mk, mv, mo, mi, mf, pu, hq, hf, hi, hg = jnp.split(w_in, pts, axis=-1)
    gates = jnp.pad(jnp.concatenate([mi, mf], axis=-1), ((0, 0), (0, 0), (0, HP - 2 * NH)))
    w_in_p = jnp.concatenate([_pad_heads(mq, -1), _pad_heads(mk, -1), _pad_heads(mv, -1), _pad_heads(mo, -1),
                              gates, pu, hq, hf, _pad_heads(hi, -1), _pad_heads(hg, -1)], axis=-1)
    w_out_p = jnp.concatenate([_pad_heads(w_out[:, 0:m_w], 1), w_out[:, m_w:m_w + P_W],
                               _pad_heads(w_out[:, m_w + P_W:], 1)], axis=1)
    gbias = jnp.pad(jnp.concatenate([b_igate, b_fgate], axis=-1), ((0, 0), (0, HP - 2 * NH)))[:, None, :]
    conv_p = jnp.concatenate([_pad_heads(conv_w[..., 0:m_w], -1), _pad_heads(conv_w[..., m_w:], -1)], axis=-1)
    eye = jnp.eye(len(POOL_WINDOWS), dtype=F32)
    pool_bd = jnp.einsum('lgce,gh->lgche', pool_w, eye).reshape(DEPTH, P_W, P_W)
    return {
        "g1": norm1_g[:, None, :], "w_in": w_in_p.astype(BF16), "w_in_f32": w_in_p, "gbias": gbias,
        "conv_w": conv_p, "mnorm": _pad_heads(mlstm_norm_g, -1)[:, None, :],
        "pool_w": pool_bd.astype(BF16), "pool_w_f32": pool_bd,
        "pool_scale": pool_scale[:, None, :], "lb_logits": lb_logits,
        "hnorm": _pad_heads(hgrn_norm_g, -1)[:, None, :], "w_out": w_out_p.astype(BF16), "w_out_f32": w_out_p,
    }


def _states_to_kernel(C, n, m, conv, pool, S):
    nb = C.shape[0]
    caug = jnp.concatenate([C, n[..., None]], axis=-1)
    caug = jnp.pad(caug, ((0, 0), (0, 0), (0, HP - DH), (0, HP - DH - 1)))
    mk = jnp.pad(jnp.broadcast_to(m[:, :, None], (nb, NH, HP)), ((0, 0), (0, 8 - NH), (0, 0)))
    m_w = NH * DH
    convk = jnp.concatenate([_pad_heads(conv[..., 0:m_w], -1), _pad_heads(conv[..., m_w:], -1)], axis=-1)
    convk = jnp.pad(convk, ((0, 0), (CONV_HDR - 3, 0), (0, 0)))
    poolk = jnp.pad(pool, ((0, 0), (1, 0), (0, 0)))
    sk = jnp.pad(jnp.swapaxes(S, -1, -2), ((0, 0), (0, 0), (0, HP - DH), (0, 0)))
    return caug, mk, convk, poolk, sk


def _states_from_kernel(st):
    caug, mk, convk, poolk, sk = st
    m_w = NH * DH
    C = caug[:, :, 0:DH, 0:DH]
    n = caug[:, :, 0:DH, DH]
    m = mk[:, 0:NH, 0]
    conv = convk[:, CONV_HDR - 3:, :]
    conv = jnp.concatenate([_unpad_heads(conv[..., 0:HW], -1), _unpad_heads(conv[..., HW:], -1)], axis=-1)
    pool = poolk[:, 1:, :]
    S = jnp.swapaxes(sk[:, :, 0:DH, :], -1, -2)
    return C, n, m, conv, pool, S


def _pick_tile(n, candidates):
    for c in candidates:
        if n % c == 0:
            return c
    raise ValueError(f"no row tile for {n}")


def kernel(x_prompt, x_sample, state_mlstm_C, state_mlstm_n, state_mlstm_m, state_mlstm_conv, state_pool,
           state_hgrn, meta_tokens, norm1_g, norm2_g, final_g, w_in, b_igate, b_fgate, conv_w, mlstm_norm_g,
           pool_w, pool_scale, lb_logits, hgrn_norm_g, w_out, ffn_w1, ffn_w3, ffn_w2, router_w, moe_w1,
           moe_w3, moe_w2):
    B, T, _ = x_prompt.shape
    SB, ST, _ = x_sample.shape
    w = _prep_weights(norm1_g, w_in, b_igate, b_fgate, conv_w, mlstm_norm_g, pool_w, pool_scale, lb_logits,
                      hgrn_norm_g, w_out)
    ffn_w1b, ffn_w3b, ffn_w2b = ffn_w1.astype(BF16), ffn_w3.astype(BF16), ffn_w2.astype(BF16)
    router_p = jnp.pad(router_w, ((0, 0), (0, 0), (0, HP - N_EXPERTS)))

    n_main = B * T
    off_s = n_main
    off_m = off_s + SB * ST
    assert SB * ST == TAIL and n_main % TAIL == 0
    n_tok = off_m + META_TOKENS
    tile = 1280 if n_main >= 16384 else 256
    expert_tile = 1024 if n_main >= 16384 else 256
    n_pad = -(-n_tok // tile) * tile
    xall = jnp.concatenate([x_prompt.reshape(n_main, D_MODEL), x_sample.reshape(SB * ST, D_MODEL),
                            meta_tokens.astype(F32), jnp.zeros((n_pad - n_tok, D_MODEL), F32)], axis=0)
    tv_main = _pick_tile(T, (256, 128))

    zero_states = (jnp.zeros((1, NH, HP, HP), F32), jnp.zeros((1, 8, HP), F32),
                   jnp.zeros((1, CONV_HDR, 2 * HW), F32), jnp.zeros((1, 16, P_W), F32),
                   jnp.zeros((1, NH, HP, HP), F32))
    p_states, s_states = [], []
    st_in = jax.vmap(_states_to_kernel)(state_mlstm_C, state_mlstm_n, state_mlstm_m, state_mlstm_conv,
                                        state_pool, state_hgrn)
    tail_blocks = tuple((b * T + T - TAIL) // TAIL for b in range(B)) + (off_s // TAIL,)
    for l in range(DEPTH):
        precise = l < PRECISE_LAYERS
        xall, st_meta = _mixer_call(xall, zero_states, True, w, l, row_off=off_m, seq_stride=META_TOKENS, nb=1,
                                    seq=META_TOKENS, tv=META_TOKENS, pos0=0, precise=False,
                                    name=f"mixer_meta_{l}")
        if precise:
            xall, st_p = _mixer_call(xall, st_meta, True, w, l, row_off=0, seq_stride=T, nb=B, seq=T - TAIL,
                                     tv=tv_main, pos0=META_TOKENS, precise=False, name=f"mixer_prompt_{l}")
            xall, st_p = _mixer_call(xall, st_p, False, w, l, row_off=T - TAIL, seq_stride=T, nb=B, seq=TAIL,
                                     tv=TAIL, pos0=META_TOKENS + T - TAIL, precise=True,
                                     name=f"mixer_prompt_tail_{l}")
        else:
            xall, st_p = _mixer_call(xall, st_meta, True, w, l, row_off=0, seq_stride=T, nb=B, seq=T,
                                     tv=tv_main, pos0=META_TOKENS, precise=False, name=f"mixer_prompt_{l}")
        xall, st_s = _mixer_call(xall, st_in, False, w, l, row_off=off_s, seq_stride=ST, nb=SB, seq=ST, tv=ST,
                                 pos0=META_TOKENS + PAST_LEN, precise=True, name=f"mixer_sample_{l}",
                                 stacked_states=True)
        p_states.append(st_p)
        s_states.append(st_s)
        i = l // 2
        if l % 2 == 0:
            f32_blocks = tail_blocks if precise else tail_blocks[-1:]
            rows = _ffn_rows_f32_call(xall, f32_blocks, norm2_g[l][None, :], ffn_w1[i], ffn_w3[i], ffn_w2[i],
                                      tm=TAIL, tf=512, name=f"ffn_rows_f32_{l}")
            xall = _ffn_call(xall, norm2_g[l][None, :], ffn_w1b[i], ffn_w3b[i], ffn_w2b[i],
                             tm=tile, tf=512, name=f"ffn_{l}")
            for j, blk in enumerate(f32_blocks):
                xall = lax.dynamic_update_slice(xall, rows[j * TAIL:(j + 1) * TAIL], (blk * TAIL, 0))
        else:
            xall = _moe_layer(xall, norm2_g[l][None, :], router_p[i], moe_w1[i], moe_w3[i], moe_w2[i],
                              final_g[None, :], tb=tile, tm=expert_tile, tf=512, final_norm=(l == DEPTH - 1),
                              tag=str(l))
    y_prompt = xall[0:n_main].reshape(B, T, D_MODEL)
    y_sample = xall[off_s:off_m].reshape(SB, ST, D_MODEL)
    p_out = jax.vmap(_states_from_kernel)(tuple(jnp.stack([s[j] for s in p_states], axis=0) for j in range(5)))
    s_out = jax.vmap(_states_from_kernel)(tuple(jnp.stack([s[j] for s in s_states], axis=0) for j in range(5)))
    return (y_prompt, y_sample) + tuple(p_out) + tuple(s_out)
```

```python
import functools

import jax
import jax.numpy as jnp
from jax import lax
from jax.experimental import pallas as pl
from jax.experimental.pallas import tpu as pltpu

F32 = jnp.float32
BF16 = jnp.bfloat16

D_MODEL = 1024
DEPTH = 4
META_TOKENS = 16
PAST_LEN = 4096
EPS = 1e-6
NH = 4
DH = 96
HP = 128
HW = NH * HP
P_W = 256
P_GW = 64
POOL_WINDOWS = (2, 4, 8, 16)
D_FF = 3584
N_EXPERTS = 8
NEG = -1e30

OFF_MQ, OFF_MK, OFF_MV, OFF_MO = 0, 512, 1024, 1536
OFF_G = 2048
OFF_PU = 2176
OFF_HQ, OFF_HF, OFF_HI, OFF_HG = 2432, 2944, 3456, 3968
D_INP = 4480
OFF_YM, OFF_YP, OFF_YH = 0, 512, 768
D_MIXP = 1280

M_CHUNK = 128
H_CHUNK = 64
H_SUB = 16
H_SAFE_LOG_DECAY = -60.0
CONV_HDR = 8
POOL_HDR = 32

TAIL = 256
PRECISE_LAYERS = 2

VMEM_LIMIT = 60 * 1024 * 1024


def _sigmoid(x):
    return 1.0 / (1.0 + jnp.exp(-x))


def _dot(a, b, precision=None):
    return jnp.dot(a, b, preferred_element_type=F32, precision=precision)


def _dot_nt(a, b, precision=None):
    return lax.dot_general(a, b, (((1,), (1,)), ((), ())), preferred_element_type=F32, precision=precision)


def _dot_tn(a, b, precision=None):
    return lax.dot_general(a, b, (((0,), (0,)), ((), ())), preferred_element_type=F32, precision=precision)


def _cumsum_rows(tril_bf, x, precise=False):
    if precise:
        return _dot(tril_bf.astype(F32), x, precision=lax.Precision.HIGHEST)
    n = x.shape[1]
    hi = x.astype(BF16)
    lo = (x - hi.astype(F32)).astype(BF16)
    both = _dot(tril_bf, jnp.concatenate([hi, lo], axis=1))
    return both[:, :n] + both[:, n:]


def _bcast_rows(row, n):
    return jnp.broadcast_to(row, (n, row.shape[1]))


def _mixer_kernel(x_ref, c0_ref, m0_ref, conv0_ref, pool0_ref, s0_ref,
                  g1_ref, win_ref, gb_ref, cw_ref, mng_ref, pw_ref, ps_ref, lbl_ref, hng_ref, wout_ref,
                  xo_ref, cf_ref, mf_ref, convf_ref, poolf_ref, sf_ref,
                  proj_s, qk_s, u_s, s2_s, s4_s, s8_s, hk_s, gate_s, y_s, xres_s, gx_s, c_s, m_s, st_s,
                  *, layer, tv, lb, pos0, n_t, precise, pipelined):
    pad = lb - tv
    t = pl.program_id(1)
    n_steps = n_t + 1 if pipelined else n_t
    if pipelined:
        ia = lax.rem(t, 2)
        ib = 1 - ia
        live = t >= 1
    else:
        ia = ib = 0
        live = None
    prec = lax.Precision.HIGHEST if precise else None
    dot = functools.partial(_dot, precision=prec)
    dot_nt = functools.partial(_dot_nt, precision=prec)
    dot_tn = functools.partial(_dot_tn, precision=prec)

    def mm(a):
        return a if precise else a.astype(BF16)

    def keep(new, old):
        return new if live is None else jnp.where(live, new, old)

    @pl.when(t == 0)
    def _init():
        c_s[...] = c0_ref[...]
        m_s[...] = m0_ref[...]
        st_s[...] = s0_ref[...]
        qk_s[...] = jnp.zeros(qk_s.shape, F32)
        qk_s[pad:pad + CONV_HDR, :] = conv0_ref[...]
        u_s[...] = jnp.zeros(u_s.shape, F32)
        u_s[pad + 16:pad + 32, :] = pool0_ref[...]
        s2_s[0:POOL_HDR, :] = jnp.zeros((POOL_HDR, P_W), F32)
        s4_s[0:POOL_HDR, :] = jnp.zeros((POOL_HDR, P_W), F32)
        s8_s[0:POOL_HDR, :] = jnp.zeros((POOL_HDR, P_W), F32)
        if pad:
            proj_s[0, 0:pad, :] = jnp.zeros((pad, D_INP), F32)
        if pipelined:
            proj_s[1] = jnp.zeros(proj_s.shape[1:], F32)
            hk_s[1] = jnp.zeros(hk_s.shape[1:], F32)
            gate_s[1] = jnp.zeros(gate_s.shape[1:], F32)
            y_s[1] = jnp.zeros(y_s.shape[1:], y_s.dtype)
            xres_s[1] = jnp.zeros(xres_s.shape[1:], F32)

    x = x_ref[...]
    ms = jnp.mean(x * x, axis=-1, keepdims=True)
    hn = mm(x * lax.rsqrt(ms + EPS) * g1_ref[...])
    qk_s[CONV_HDR + pad:CONV_HDR + lb, :] = dot(hn, win_ref[:, 0:OFF_MV])
    proj_s[ia, pad:lb, OFF_MV:OFF_PU] = dot(hn, win_ref[:, OFF_MV:OFF_PU])
    u_s[POOL_HDR + pad:POOL_HDR + lb, :] = dot(hn, win_ref[:, OFF_PU:OFF_HQ])
    proj_s[ia, pad:lb, OFF_HQ:D_INP] = dot(hn, win_ref[:, OFF_HQ:D_INP])
    if pipelined:
        xres_s[ia] = x

    row = lax.broadcasted_iota(jnp.int32, (lb, HP), 0)
    lane = lax.broadcasted_iota(jnp.int32, (lb, HP), 1)

    acc = qk_s[5:5 + lb, :] * cw_ref[0:1, :]
    for j in range(1, 4):
        acc = acc + qk_s[5 + j:5 + j + lb, :] * cw_ref[j:j + 1, :]
    qk = acc * _sigmoid(acc)
    proj_s[ia, :, OFF_MQ:OFF_MK] = qk[:, 0:HW]
    proj_s[ia, :, OFF_MK:OFF_MV] = qk[:, HW:2 * HW] * (DH ** -0.5)
    conv_tail = qk_s[lb:lb + CONV_HDR, :]
    qk_s[0:CONV_HDR, :] = conv_tail

    gpre = proj_s[ia, :, OFF_G:OFF_G + HP] + gb_ref[...]
    lsig = jnp.minimum(gpre, 0.0) - jnp.log(1.0 + jnp.exp(-jnp.abs(gpre)))
    gates = jnp.where(lane < NH, gpre, jnp.where(lane < 2 * NH, lsig, 0.0))
    if pad:
        gates = jnp.where(row >= pad, gates, jnp.where(lane < NH, NEG, 0.0))
    gate_s[ia] = gates

    n_ext = lb + 16
    s2_s[16:16 + n_ext, :] = u_s[16:16 + n_ext, :] + u_s[15:15 + n_ext, :]
    s4_s[16:16 + n_ext, :] = s2_s[16:16 + n_ext, :] + s2_s[14:14 + n_ext, :]
    s8_s[16:16 + n_ext, :] = s4_s[16:16 + n_ext, :] + s4_s[12:12 + n_ext, :]
    u_cur = u_s[POOL_HDR:POOL_HDR + lb, :]
    w2 = s2_s[POOL_HDR:POOL_HDR + lb, :]
    w4 = s4_s[POOL_HDR:POOL_HDR + lb, :]
    w8 = s8_s[POOL_HDR:POOL_HDR + lb, :]
    w16 = w8 + s8_s[POOL_HDR - 8:POOL_HDR - 8 + lb, :]
    lane_p = lax.broadcasted_iota(jnp.int32, (lb, P_W), 1)
    wsum = jnp.where(lane_p < P_GW, w2, jnp.where(lane_p < 2 * P_GW, w4, jnp.where(lane_p < 3 * P_GW, w8, w16)))
    if pos0 >= POOL_WINDOWS[-1] - 1:
        inv = jnp.where(lane_p < P_GW, 0.5, jnp.where(lane_p < 2 * P_GW, 0.25,
                                                       jnp.where(lane_p < 3 * P_GW, 0.125, 0.0625)))
        mean = wsum * inv
    else:
        row_p = lax.broadcasted_iota(jnp.int32, (lb, P_W), 0)
        posn = (row_p + (pos0 + 1 - pad + t * tv)).astype(F32)
        wlen = jnp.where(lane_p < P_GW, 2.0, jnp.where(lane_p < 2 * P_GW, 4.0,
                                                        jnp.where(lane_p < 3 * P_GW, 8.0, 16.0)))
        mean = wsum / jnp.maximum(jnp.minimum(wlen, posn), 1.0)
    pooled = mm(mean - u_cur)
    y_s[ia, :, OFF_YP:OFF_YH] = mm(dot(pooled, pw_ref[...]) * ps_ref[...])
    pool_tail = u_s[lb:lb + POOL_HDR, :]
    u_s[0:POOL_HDR, :] = pool_tail

    lbl = lbl_ref[...]
    e = jnp.exp(lbl - jnp.max(lbl, axis=0, keepdims=True))
    p = e / jnp.sum(e, axis=0, keepdims=True)
    lbv = jnp.sum(p[0:layer + 1, :], axis=0, keepdims=True) - p[0:1, :]
    hq = proj_s[ia, :, OFF_HQ:OFF_HF]
    hf = proj_s[ia, :, OFF_HF:OFF_HI]
    fg = lbv + (1.0 - lbv) * _sigmoid(hf)
    kh = 1.0 - fg
    lg = jnp.log(fg)
    if pad:
        row_h = lax.broadcasted_iota(jnp.int32, (lb, HW), 0)
        kh = jnp.where(row_h >= pad, kh, 0.0)
        lg = jnp.where(row_h >= pad, lg, 0.0)
    proj_s[ia, :, OFF_HQ:OFF_HF] = hq * _sigmoid(hq)
    proj_s[ia, :, OFF_HF:OFF_HI] = lg
    hk_s[ia] = kh

    rr = lax.broadcasted_iota(jnp.int32, (M_CHUNK, M_CHUNK), 0)
    cc = lax.broadcasted_iota(jnp.int32, (M_CHUNK, M_CHUNK), 1)
    causal = rr >= cc
    tril_m = jnp.where(causal, 1.0, 0.0).astype(BF16)
    lane_c = lax.broadcasted_iota(jnp.int32, (M_CHUNK, HP), 1)

    for c in range(lb // M_CHUNK):
        rows = slice(c * M_CHUNK, (c + 1) * M_CHUNK)
        gt = gate_s[ib, rows, :]
        bcum = _cumsum_rows(tril_m, jnp.where(lane_c >= NH, gt, 0.0), precise)
        cg = gt - pltpu.roll(bcum, HP - NH, axis=1)
        cg_t = cg.T
        for h in range(NH):
            q = proj_s[ib, rows, OFF_MQ + h * HP:OFF_MQ + (h + 1) * HP]
            k = proj_s[ib, rows, OFF_MK + h * HP:OFF_MK + (h + 1) * HP]
            v = proj_s[ib, rows, OFF_MV + h * HP:OFF_MV + (h + 1) * HP]
            c_row = cg_t[h:h + 1, :]
            c_col = cg[:, h:h + 1]
            b_col = bcum[:, NH + h:NH + h + 1]
            m_old = m_s[h:h + 1, :]
            m_prev = m_old[:, 0:1]
            mx = jnp.maximum(jnp.max(jnp.where(causal, c_row, NEG), axis=1, keepdims=True), m_prev)
            w = jnp.exp(jnp.where(causal, c_row - mx, NEG))
            w_int = jnp.exp(m_prev - mx)
            mx_last = mx[M_CHUNK - 1:M_CHUNK, :]
            s = dot_nt(mm(q), mm(k)) * w
            vaug = jnp.where(lane_c == DH, 1.0, v)
            caug = c_s[h]
            lhs = mm(jnp.concatenate([s, q * w_int], axis=1))
            rhs = mm(jnp.concatenate([vaug, caug], axis=0))
            nd = dot(lhs, rhs)
            den = nd[:, DH:DH + 1]
            rden = 1.0 / jnp.maximum(jnp.abs(den), jnp.exp(-(b_col + mx)))
            wl = jnp.exp(c_col - mx_last)
            decay = jnp.exp(m_prev - mx_last)
            c_s[h] = keep(decay * caug + dot_tn(mm(k * wl), mm(vaug)), caug)
            m_s[h:h + 1, :] = keep(jnp.broadcast_to(b_col[M_CHUNK - 1:M_CHUNK, :] + mx_last, (1, HP)), m_old)
            mo = proj_s[ib, rows, OFF_MO + h * HP:OFF_MO + (h + 1) * HP]
            z = jnp.where(lane_c < DH, nd * _sigmoid(mo), 0.0)
            ssq = jnp.sum(z * z, axis=1, keepdims=True) * (1.0 / DH)
            fac = rden * lax.rsqrt(rden * rden * ssq + EPS)
            y_s[ib, rows, OFF_YM + h * HP:OFF_YM + (h + 1) * HP] = mm(z * fac * mng_ref[:, h * HP:(h + 1) * HP])

    r64 = lax.broadcasted_iota(jnp.int32, (H_CHUNK, H_CHUNK), 0)
    c64 = lax.broadcasted_iota(jnp.int32, (H_CHUNK, H_CHUNK), 1)
    tril_h = jnp.where(r64 >= c64, 1.0, 0.0).astype(BF16)
    bdiff = r64 // H_SUB - c64 // H_SUB
    mask_intra = (bdiff == 0) & (r64 >= c64)
    n_sub = H_CHUNK // H_SUB

    def h_chunk(c, factored):
        rows = slice(c * H_CHUNK, (c + 1) * H_CHUNK)
        g = _cumsum_rows(tril_h, proj_s[ib, rows, OFF_HF:OFF_HI], precise)
        qh = proj_s[ib, rows, OFF_HQ:OFF_HF]
        khc = hk_s[ib, rows, :]
        if not factored:
            gx_s[...] = g
        bnd = [jnp.zeros((1, HW), F32)] + [g[(j + 1) * H_SUB - 1:(j + 1) * H_SUB, :] for j in range(n_sub)]
        g_start = jnp.concatenate([_bcast_rows(bnd[j], H_SUB) for j in range(n_sub)], axis=0)
        g_end = jnp.concatenate([_bcast_rows(bnd[j + 1], H_SUB) for j in range(n_sub)], axis=0)
        g_last = bnd[n_sub]
        qt = qh * jnp.exp(g - g_start)
        khat = khc * jnp.exp(g_end - g)
        kbar = khc * jnp.exp(g_start - g)
        qg = qh * jnp.exp(g)
        kend = khc * jnp.exp(g_last - g)
        dsub = [jnp.exp(bnd[j + 1] - bnd[j]) for j in range(n_sub - 1)]
        ones = jnp.ones((H_SUB, HW), F32)
        qlev = [qt]
        for d in range(1, n_sub - 1):
            fac = jnp.concatenate([ones] * d + [_bcast_rows(dsub[j - d], H_SUB) for j in range(d, n_sub)], axis=0)
            qlev.append(qlev[-1] * fac)
        for h in range(NH):
            sl = slice(h * HP, (h + 1) * HP)
            if factored:
                lhs = mm(jnp.concatenate([ql[:, sl] for ql in qlev], axis=0))
                inter = dot_nt(lhs, mm(khat[:, sl]))
                intra = dot_nt(mm(qt[:, sl]), mm(kbar[:, sl]))
                att = jnp.where(mask_intra, intra, 0.0)
                for d in range(n_sub - 1):
                    att = att + jnp.where(bdiff == d + 1, inter[d * H_CHUNK:(d + 1) * H_CHUNK, :], 0.0)
            else:
                def cols(j, att_acc, sl=sl, gh=g[:, sl], qhh=qh[:, sl]):
                    r0 = pl.multiple_of(j * 8, 8)
                    g8 = gx_s[pl.ds(r0, 8), sl]
                    k8 = hk_s[ib, pl.ds(c * H_CHUNK + r0, 8), sl]
                    for r in range(8):
                        wgt = jnp.exp(jnp.minimum(gh - g8[r:r + 1, :], 0.0))
                        pcol = jnp.sum(qhh * wgt * k8[r:r + 1, :], axis=1, keepdims=True)
                        att_acc = att_acc + jnp.where((c64 == r0 + r) & (r64 >= r0 + r), pcol, 0.0)
                    return att_acc
                att = lax.fori_loop(0, H_CHUNK // 8, cols, jnp.zeros((H_CHUNK, H_CHUNK), F32))
            iv = mm(proj_s[ib, rows, OFF_HI + h * HP:OFF_HI + (h + 1) * HP])
            st = st_s[h]
            o = dot(mm(att), iv) + dot_nt(mm(qg[:, sl]), mm(st))
            st_s[h] = keep(st * jnp.exp(g_last[:, sl]) + dot_tn(iv, mm(kend[:, sl])), st)
            msq = jnp.sum(o * o, axis=1, keepdims=True) * (1.0 / DH)
            on = o * lax.rsqrt(msq + EPS) * hng_ref[:, sl]
            hg = proj_s[ib, rows, OFF_HG + h * HP:OFF_HG + (h + 1) * HP]
            y_s[ib, rows, OFF_YH + h * HP:OFF_YH + (h + 1) * HP] = mm(on * (hg * _sigmoid(hg)))

    lg_sub = proj_s[ib, :, OFF_HF:OFF_HI].reshape(lb // H_SUB, H_SUB, HW)
    factor_ok = jnp.min(jnp.sum(lg_sub, axis=1)) > H_SAFE_LOG_DECAY

    @pl.when(factor_ok)
    def _factored():
        for c in range(lb // H_CHUNK):
            h_chunk(c, True)

    @pl.when(jnp.logical_not(factor_ok))
    def _direct():
        for c in range(lb // H_CHUNK):
            h_chunk(c, False)

    out = dot(y_s[ib], wout_ref[...])
    xres = xres_s[ib] if pipelined else x
    xo_ref[...] = xres + out[pad:lb, :]

    @pl.when(t == n_steps - 1)
    def _final():
        cf_ref[...] = c_s[...]
        mf_ref[...] = m_s[...]
        convf_ref[...] = conv_tail
        poolf_ref[...] = pool_tail[16:32, :]
        sf_ref[...] = st_s[...]


def _mixer_call(xall, states, shared_init, w, layer, *, row_off, seq_stride, nb, seq, tv, pos0, precise, name,
                stacked_states=False):
    lb = max(tv, M_CHUNK)
    n_t = seq // tv
    pipelined = False
    n_steps = n_t + 1 if pipelined else n_t
    n_buf = 2 if pipelined else 1
    assert row_off % tv == 0 and seq_stride % tv == 0 and seq % tv == 0
    blk0, blk_stride = row_off // tv, seq_stride // tv
    c0, m0, conv0, pool0, s0 = states
    lead_blk = (None,) if stacked_states else ()
    lead_idx = (layer,) if stacked_states else ()

    def x_map(b, t):
        return (blk0 + b * blk_stride + jnp.minimum(t, n_t - 1), 0)

    def xo_map(b, t):
        return (blk0 + b * blk_stride + (jnp.maximum(t - 1, 0) if pipelined else t), 0)

    def st_map(b, t):
        return lead_idx + (0 if shared_init else b, 0, 0, 0)

    def st_map3(b, t):
        return lead_idx + (0 if shared_init else b, 0, 0)

    def const_spec(shape):
        return pl.BlockSpec(shape, lambda b, t: (0,) * len(shape), pipeline_mode=pl.Buffered(1))

    in_specs = [
        pl.BlockSpec((tv, D_MODEL), x_map),
        pl.BlockSpec(lead_blk + (None, NH, HP, HP), st_map),
        pl.BlockSpec(lead_blk + (None, 8, HP), st_map3),
        pl.BlockSpec(lead_blk + (None, CONV_HDR, 2 * HW), st_map3),
        pl.BlockSpec(lead_blk + (None, 16, P_W), st_map3),
        pl.BlockSpec(lead_blk + (None, NH, HP, HP), st_map),
        const_spec((1, D_MODEL)),
        const_spec((D_MODEL, D_INP)),
        const_spec((1, HP)),
        const_spec((4, 2 * HW)),
        const_spec((1, HW)),
        const_spec((P_W, P_W)),
        const_spec((1, P_W)),
        const_spec((DEPTH, HW)),
        const_spec((1, HW)),
        const_spec((D_MIXP, D_MODEL)),
    ]
    out_specs = [
        pl.BlockSpec((tv, D_MODEL), xo_map),
        pl.BlockSpec((None, NH, HP, HP), lambda b, t: (b, 0, 0, 0)),
        pl.BlockSpec((None, 8, HP), lambda b, t: (b, 0, 0)),
        pl.BlockSpec((None, CONV_HDR, 2 * HW), lambda b, t: (b, 0, 0)),
        pl.BlockSpec((None, 16, P_W), lambda b, t: (b, 0, 0)),
        pl.BlockSpec((None, NH, HP, HP), lambda b, t: (b, 0, 0, 0)),
    ]
    out_shape = [
        jax.ShapeDtypeStruct(xall.shape, F32),
        jax.ShapeDtypeStruct((nb, NH, HP, HP), F32),
        jax.ShapeDtypeStruct((nb, 8, HP), F32),
        jax.ShapeDtypeStruct((nb, CONV_HDR, 2 * HW), F32),
        jax.ShapeDtypeStruct((nb, 16, P_W), F32),
        jax.ShapeDtypeStruct((nb, NH, HP, HP), F32),
    ]
    scratch = [
        pltpu.VMEM((n_buf, lb, D_INP), F32),
        pltpu.VMEM((CONV_HDR + lb, 2 * HW), F32),
        pltpu.VMEM((POOL_HDR + lb, P_W), F32),
        pltpu.VMEM((POOL_HDR + lb, P_W), F32),
        pltpu.VMEM((POOL_HDR + lb, P_W), F32),
        pltpu.VMEM((POOL_HDR + lb, P_W), F32),
        pltpu.VMEM((n_buf, lb, HW), F32),
        pltpu.VMEM((n_buf, lb, HP), F32),
        pltpu.VMEM((n_buf, lb, D_MIXP), F32 if precise else BF16),
        pltpu.VMEM((n_buf, tv, D_MODEL) if pipelined else (1, 8, HP), F32),
        pltpu.VMEM((H_CHUNK, HW), F32),
        pltpu.VMEM((NH, HP, HP), F32),
        pltpu.VMEM((8, HP), F32),
        pltpu.VMEM((NH, HP, HP), F32),
    ]
    sfx = "_f32" if precise else ""
    kern = functools.partial(_mixer_kernel, layer=layer, tv=tv, lb=lb, pos0=pos0, n_t=n_t, precise=precise,
                             pipelined=pipelined)
    outs = pl.pallas_call(
        kern,
        grid=(nb, n_steps),
        in_specs=in_specs,
        out_specs=out_specs,
        out_shape=out_shape,
        scratch_shapes=scratch,
        input_output_aliases={0: 0},
        compiler_params=pltpu.CompilerParams(dimension_semantics=("arbitrary", "arbitrary"),
                                             vmem_limit_bytes=VMEM_LIMIT),
        name=name,
    )(xall, c0, m0, conv0, pool0, s0,
      w["g1"][layer], w["w_in" + sfx][layer], w["gbias"][layer], w["conv_w"][layer], w["mnorm"][layer],
      w["pool_w" + sfx][layer], w["pool_scale"][layer], w["lb_logits"], w["hnorm"][layer],
      w["w_out" + sfx][layer])
    return outs[0], tuple(outs[1:])


def _ffn_kernel(x_ref, g_ref, w1_ref, w3_ref, w2_ref, o_ref, hn_s, *, precise):
    f = pl.program_id(1)
    prec = lax.Precision.HIGHEST if precise else None

    @pl.when(f == 0)
    def _start():
        x = x_ref[...]
        ms = jnp.mean(x * x, axis=-1, keepdims=True)
        hn_s[...] = (x * lax.rsqrt(ms + EPS) * g_ref[...]).astype(hn_s.dtype)
        o_ref[...] = x

    hn = hn_s[...]
    h1 = _dot(hn, w1_ref[...], prec)
    h3 = _dot(hn, w3_ref[...], prec)
    a = (h1 * _sigmoid(h1) * h3).astype(hn_s.dtype)
    o_ref[...] += _dot(a, w2_ref[...], prec)


def _ffn_rows_f32_call(xall, blocks, g, w1, w3, w2, *, tm, tf, name):
    n_f = D_FF // tf

    def x_map(i, f):
        idx = blocks[-1]
        for j in range(len(blocks) - 2, -1, -1):
            idx = jnp.where(i == j, blocks[j], idx)
        return (idx, 0)

    return pl.pallas_call(
        functools.partial(_ffn_kernel, precise=True),
        grid=(len(blocks), n_f),
        in_specs=[
            pl.BlockSpec((tm, D_MODEL), x_map),
            pl.BlockSpec((1, D_MODEL), lambda i, f: (0, 0)),
            pl.BlockSpec((D_MODEL, tf), lambda i, f: (0, f)),
            pl.BlockSpec((D_MODEL, tf), lambda i, f: (0, f)),
            pl.BlockSpec((tf, D_MODEL), lambda i, f: (f, 0)),
        ],
        out_specs=pl.BlockSpec((tm, D_MODEL), lambda i, f: (i, 0)),
        out_shape=jax.ShapeDtypeStruct((len(blocks) * tm, D_MODEL), F32),
        scratch_shapes=[pltpu.VMEM((tm, D_MODEL), F32)],
        compiler_params=pltpu.CompilerParams(dimension_semantics=("arbitrary", "arbitrary"),
                                             vmem_limit_bytes=VMEM_LIMIT),
        name=name,
    )(xall, g, w1, w3, w2)


def _ffn_call(xall, g, w1, w3, w2, *, tm, tf, name):
    n = xall.shape[0]
    n_f = D_FF // tf
    return pl.pallas_call(
        functools.partial(_ffn_kernel, precise=False),
        grid=(n // tm, n_f),
        in_specs=[
            pl.BlockSpec((tm, D_MODEL), lambda i, f: (i, 0)),
            pl.BlockSpec((1, D_MODEL), lambda i, f: (0, 0)),
            pl.BlockSpec((D_MODEL, tf), lambda i, f: (0, f)),
            pl.BlockSpec((D_MODEL, tf), lambda i, f: (0, f)),
            pl.BlockSpec((tf, D_MODEL), lambda i, f: (f, 0)),
        ],
        out_specs=pl.BlockSpec((tm, D_MODEL), lambda i, f: (i, 0)),
        out_shape=jax.ShapeDtypeStruct(xall.shape, F32),
        scratch_shapes=[pltpu.VMEM((tm, D_MODEL), BF16)],
        input_output_aliases={0: 0},
        compiler_params=pltpu.CompilerParams(dimension_semantics=("arbitrary", "arbitrary"),
                                             vmem_limit_bytes=VMEM_LIMIT),
        name=name,
    )(xall, g, w1, w3, w2)


GATHER_TILE = 256
NO_SLOT = -1e9


def _route_kernel(x_ref, g_ref, wr_ref, hn_ref, rank_t_ref, rank_c_ref, comb_ref, cnt_ref, *, tb):
    x = x_ref[...]
    ms = jnp.mean(x * x, axis=-1, keepdims=True)
    hn = x * lax.rsqrt(ms + EPS) * g_ref[...]
    hn_ref[...] = hn.astype(BF16)
    logits = jnp.dot(hn, wr_ref[...], preferred_element_type=F32, precision=lax.Precision.HIGHEST)
    lane = lax.broadcasted_iota(jnp.int32, (tb, HP), 1).astype(F32)
    lg = jnp.where(lane < N_EXPERTS, logits, NEG)
    v1 = jnp.max(lg, axis=1, keepdims=True)
    i1 = jnp.min(jnp.where(lg == v1, lane, float(HP)), axis=1, keepdims=True)
    mask1 = lane == i1
    lg2 = jnp.where(mask1, NEG, lg)
    v2 = jnp.max(lg2, axis=1, keepdims=True)
    i2 = jnp.min(jnp.where(lg2 == v2, lane, float(HP)), axis=1, keepdims=True)
    mask2 = lane == i2
    ex = jnp.exp(v2 - v1)
    ga = 1.0 / (1.0 + ex)
    comb_ref[...] = jnp.where(mask1, ga, 0.0) + jnp.where(mask2, ex * ga, 0.0)
    sel = mask1 | mask2
    rr = lax.broadcasted_iota(jnp.int32, (tb, tb), 0)
    cc = lax.broadcasted_iota(jnp.int32, (tb, tb), 1)
    tril_strict = jnp.where(rr > cc, 1.0, 0.0).astype(BF16)
    selb = jnp.where(sel, 1.0, 0.0)
    rank = jnp.where(sel, _dot(tril_strict, selb.astype(BF16)), NO_SLOT)
    rank_c_ref[...] = rank
    rank_t_ref[...] = rank.T[0:N_EXPERTS, :]
    cnt_ref[...] = jnp.broadcast_to(jnp.sum(selb, axis=0, keepdims=True), (8, HP))


def _route_call(xall, g, wr, *, tb, name):
    n = xall.shape[0]
    nb = n // tb
    return pl.pallas_call(
        functools.partial(_route_kernel, tb=tb),
        grid=(nb,),
        in_specs=[
            pl.BlockSpec((tb, D_MODEL), lambda i: (i, 0)),
            pl.BlockSpec((1, D_MODEL), lambda i: (0, 0)),
            pl.BlockSpec((D_MODEL, HP), lambda i: (0, 0)),
        ],
        out_specs=[
            pl.BlockSpec((tb, D_MODEL), lambda i: (i, 0)),
            pl.BlockSpec((None, N_EXPERTS, tb), lambda i: (i, 0, 0)),
            pl.BlockSpec((tb, HP), lambda i: (i, 0)),
            pl.BlockSpec((tb, HP), lambda i: (i, 0)),
            pl.BlockSpec((None, 8, HP), lambda i: (i, 0, 0)),
        ],
        out_shape=[
            jax.ShapeDtypeStruct((n, D_MODEL), BF16),
            jax.ShapeDtypeStruct((nb, N_EXPERTS, tb), F32),
            jax.ShapeDtypeStruct((n, HP), F32),
            jax.ShapeDtypeStruct((n, HP), F32),
            jax.ShapeDtypeStruct((nb, 8, HP), F32),
        ],
        compiler_params=pltpu.CompilerParams(dimension_semantics=("arbitrary",), vmem_limit_bytes=VMEM_LIMIT),
        name=name,
    )(xall, g, wr)


def _gather_kernel(tile_ref, blk_ref, exp_ref, off_ref, first_ref, act_ref, rank_t_ref, hn_ref, init_ref, xs_ref,
                   *, tb):
    del tile_ref, blk_ref, init_ref
    p = pl.program_id(0)

    @pl.when(act_ref[p] == 1)
    def _():
        rrow = rank_t_ref[pl.ds(exp_ref[p], 1), :]
        slot = (lax.broadcasted_iota(jnp.int32, (GATHER_TILE, tb), 0) - off_ref[p]).astype(F32)
        onehot = jnp.where(rrow == slot, 1.0, 0.0).astype(BF16)
        val = _dot(onehot, hn_ref[...]).astype(BF16)

        @pl.when(first_ref[p] == 1)
        def _set():
            xs_ref[...] = val

        @pl.when(first_ref[p] == 0)
        def _add():
            xs_ref[...] += val


def _gather_call(pairs, rank_t, hn, n_rows, *, tb, name):
    n_pairs = pairs[0].shape[0]
    grid_spec = pltpu.PrefetchScalarGridSpec(
        num_scalar_prefetch=6,
        grid=(n_pairs,),
        in_specs=[
            pl.BlockSpec((None, N_EXPERTS, tb), lambda p, tile, blk, *_: (blk[p], 0, 0)),
            pl.BlockSpec((tb, D_MODEL), lambda p, tile, blk, *_: (blk[p], 0)),
            pl.BlockSpec(memory_space=pl.ANY),
        ],
        out_specs=pl.BlockSpec((GATHER_TILE, D_MODEL), lambda p, tile, *_: (tile[p], 0)),
    )
    return pl.pallas_call(
        functools.partial(_gather_kernel, tb=tb),
        grid_spec=grid_spec,
        out_shape=jax.ShapeDtypeStruct((n_rows, D_MODEL), BF16),
        input_output_aliases={8: 0},
        compiler_params=pltpu.CompilerParams(dimension_semantics=("arbitrary",), vmem_limit_bytes=VMEM_LIMIT),
        name=name,
    )(*pairs, rank_t, hn, jnp.zeros((n_rows, D_MODEL), BF16))


def _experts_kernel(exp_ref, act_ref, xs_ref, w1_ref, w3_ref, w2_ref, ys_ref, acc_s, *, n_f):
    del exp_ref
    i = pl.program_id(0)
    f = pl.program_id(1)

    @pl.when(f == 0)
    def _zero():
        acc_s[...] = jnp.zeros(acc_s.shape, F32)

    @pl.when(act_ref[i] == 1)
    def _():
        xe = xs_ref[...]
        h1 = _dot(xe, w1_ref[...].astype(BF16))
        h3 = _dot(xe, w3_ref[...].astype(BF16))
        a = (h1 * _sigmoid(h1) * h3).astype(BF16)
        acc_s[...] += _dot(a, w2_ref[...].astype(BF16))

    @pl.when(f == n_f - 1)
    def _out():
        ys_ref[...] = acc_s[...].astype(BF16)


def _experts_call(tile_exp, tile_act, xs, w1, w3, w2, *, tm, tf, name):
    n_rows = xs.shape[0]
    n_f = D_FF // tf
    grid_spec = pltpu.PrefetchScalarGridSpec(
        num_scalar_prefetch=2,
        grid=(n_rows // tm, n_f),
        in_specs=[
            pl.BlockSpec((tm, D_MODEL), lambda i, f, ex, act: (i, 0)),
            pl.BlockSpec((None, D_MODEL, tf), lambda i, f, ex, act: (ex[i], 0, f)),
            pl.BlockSpec((None, D_MODEL, tf), lambda i, f, ex, act: (ex[i], 0, f)),
            pl.BlockSpec((None, tf, D_MODEL), lambda i, f, ex, act: (ex[i], f, 0)),
        ],
        out_specs=pl.BlockSpec((tm, D_MODEL), lambda i, f, ex, act: (i, 0)),
        scratch_shapes=[pltpu.VMEM((tm, D_MODEL), F32)],
    )
    return pl.pallas_call(
        functools.partial(_experts_kernel, n_f=n_f),
        grid_spec=grid_spec,
        out_shape=jax.ShapeDtypeStruct((n_rows, D_MODEL), BF16),
        compiler_params=pltpu.CompilerParams(dimension_semantics=("arbitrary", "arbitrary"),
                                             vmem_limit_bytes=VMEM_LIMIT),
        name=name,
    )(tile_exp, tile_act, xs, w1, w3, w2)


def _combine_kernel(blk_ref, tile_ref, exp_ref, off_ref, first_ref, last_ref, act_ref,
                    x_ref, rank_c_ref, comb_ref, ys_ref, fg_ref, o_ref, *, tb, final_norm):
    del blk_ref, tile_ref
    p = pl.program_id(0)

    @pl.when(first_ref[p] == 1)
    def _start():
        o_ref[...] = x_ref[...]

    @pl.when(act_ref[p] == 1)
    def _():
        pick = lax.broadcasted_iota(jnp.int32, (tb, HP), 1) == exp_ref[p]
        rcol = jnp.sum(jnp.where(pick, rank_c_ref[...], 0.0), axis=1, keepdims=True)
        gcol = jnp.sum(jnp.where(pick, comb_ref[...], 0.0), axis=1, keepdims=True)
        slot = (lax.broadcasted_iota(jnp.int32, (tb, GATHER_TILE), 1) - off_ref[p]).astype(F32)
        onehot = jnp.where(rcol == slot, 1.0, 0.0).astype(BF16)
        o_ref[...] += gcol * _dot(onehot, ys_ref[...])

    if final_norm:
        @pl.when(last_ref[p] == 1)
        def _norm():
            y = o_ref[...]
            ms = jnp.mean(y * y, axis=-1, keepdims=True)
            o_ref[...] = y * lax.rsqrt(ms + EPS) * fg_ref[...]


def _combine_call(pairs, xall, rank_c, comb, ys, fg, *, tb, final_norm, name):
    n_pairs = pairs[0].shape[0]
    grid_spec = pltpu.PrefetchScalarGridSpec(
        num_scalar_prefetch=7,
        grid=(n_pairs,),
        in_specs=[
            pl.BlockSpec((tb, D_MODEL), lambda p, blk, *_: (blk[p], 0)),
            pl.BlockSpec((tb, HP), lambda p, blk, *_: (blk[p], 0)),
            pl.BlockSpec((tb, HP), lambda p, blk, *_: (blk[p], 0)),
            pl.BlockSpec((GATHER_TILE, D_MODEL), lambda p, blk, tile, *_: (tile[p], 0)),
            pl.BlockSpec((1, D_MODEL), lambda p, *_: (0, 0)),
        ],
        out_specs=pl.BlockSpec((tb, D_MODEL), lambda p, blk, *_: (blk[p], 0)),
    )
    return pl.pallas_call(
        functools.partial(_combine_kernel, tb=tb, final_norm=final_norm),
        grid_spec=grid_spec,
        out_shape=jax.ShapeDtypeStruct(xall.shape, F32),
        input_output_aliases={7: 0},
        compiler_params=pltpu.CompilerParams(dimension_semantics=("arbitrary",), vmem_limit_bytes=VMEM_LIMIT),
        name=name,
    )(*pairs, xall, rank_c, comb, ys, fg)


def _pair_lists(cnt, *, tm, n_rows):
    nb = cnt.shape[0]
    i32 = jnp.int32
    tot = jnp.sum(cnt, axis=0)
    grp_rows = (tot + tm - 1) // tm * tm
    grp_end = jnp.cumsum(grp_rows)
    grp_start = grp_end - grp_rows
    base = grp_start[None, :] + jnp.cumsum(cnt, axis=0) - cnt
    lo = base // GATHER_TILE
    hi = (base + cnt - 1) // GATHER_TILE
    npair = jnp.where(cnt > 0, hi - lo + 1, 0)
    n_pairs = nb * N_EXPERTS + n_rows // GATHER_TILE
    n_keys = nb * N_EXPERTS

    def expand(expert_major):
        def flat(a):
            return (a.T if expert_major else a).reshape(-1)
        np_k = flat(npair)
        cum = jnp.cumsum(np_k)
        total = cum[-1]
        p = jnp.arange(n_pairs, dtype=i32)
        pc = jnp.minimum(p, total - 1)
        k = jnp.sum((cum[None, :] <= pc[:, None]).astype(i32), axis=1)
        table = jnp.stack([flat(lo), flat(base), cum - np_k], axis=1)
        row = jnp.take(table, jnp.minimum(k, n_keys - 1), axis=0)
        tile = row[:, 0] + pc - row[:, 2]
        blk, exp = (k % nb, k // nb) if expert_major else (k // N_EXPERTS, k % N_EXPERTS)
        act = (p < total).astype(i32)
        return tile.astype(i32), blk.astype(i32), exp.astype(i32), (row[:, 1] - tile * GATHER_TILE).astype(i32), act

    g_tile, g_blk, g_exp, g_off, g_act = expand(True)
    g_first = jnp.concatenate([jnp.ones((1,), i32), (g_tile[1:] != g_tile[:-1]).astype(i32)])
    c_tile, c_blk, c_exp, c_off, c_act = expand(False)
    c_first = jnp.concatenate([jnp.ones((1,), i32), (c_blk[1:] != c_blk[:-1]).astype(i32)])
    c_last = jnp.concatenate([(c_blk[1:] != c_blk[:-1]).astype(i32), jnp.ones((1,), i32)])
    c_last = jnp.where(jnp.arange(n_pairs) == jnp.sum(c_act) - 1, 1, c_last) * c_act
    n_tiles = n_rows // tm
    t0 = jnp.arange(n_tiles, dtype=i32) * tm
    t_act = (t0 < grp_end[-1]).astype(i32)
    t_exp = jnp.sum((grp_end[None, :] <= jnp.minimum(t0, grp_end[-1] - 1)[:, None]).astype(i32), axis=1)
    return ((g_tile, g_blk, g_exp, g_off, g_first, g_act),
            (c_blk, c_tile, c_exp, c_off, c_first, c_last, c_act), (t_exp, t_act))


def _moe_layer(xall, g, wr, w1, w3, w2, fg, *, tb, tm, tf, final_norm, tag):
    n = xall.shape[0]
    n_rows = 2 * n + N_EXPERTS * tm
    hn, rank_t, rank_c, comb, cnt = _route_call(xall, g, wr, tb=tb, name=f"moe_route_{tag}")
    cnt = cnt[:, 0, 0:N_EXPERTS].astype(jnp.int32)
    g_pairs, c_pairs, (t_exp, t_act) = _pair_lists(cnt, tm=tm, n_rows=n_rows)
    xs = _gather_call(g_pairs, rank_t, hn, n_rows, tb=tb, name=f"moe_gather_{tag}")
    ys = _experts_call(t_exp, t_act, xs, w1, w3, w2, tm=tm, tf=tf, name=f"moe_experts_{tag}")
    return _combine_call(c_pairs, xall, rank_c, comb, ys, fg, tb=tb, final_norm=final_norm,
                         name=f"moe_combine_{tag}")


def _pad_heads(a, axis):
    axis = axis % a.ndim
    shp = a.shape
    a = a.reshape(shp[:axis] + (NH, DH) + shp[axis + 1:])
    padw = [(0, 0)] * a.ndim
    padw[axis + 1] = (0, HP - DH)
    a = jnp.pad(a, padw)
    return a.reshape(shp[:axis] + (HW,) + shp[axis + 1:])


def _unpad_heads(a, axis):
    axis = axis % a.ndim
    shp = a.shape
    a = a.reshape(shp[:axis] + (NH, HP) + shp[axis + 1:])
    a = lax.slice_in_dim(a, 0, DH, axis=axis + 1)
    return a.reshape(shp[:axis] + (NH * DH,) + shp[axis + 1:])


def _prep_weights(norm1_g, w_in, b_igate, b_fgate, conv_w, mlstm_norm_g, pool_w, pool_scale, lb_logits,
                  hgrn_norm_g, w_out):
    m_w, h_kw = NH * DH, HW
    widths = (m_w, m_w, m_w, m_w, NH, NH, P_W, h_kw, h_kw, m_w, m_w)
    pts, acc = [], 0
    for wd in widths[:-1]:
        acc += wd
        pts.append(acc)
    mq, mk, mv, mo, mi, mf, pu, hq, hf, hi, hg = jnp.split(w_in, pts, axis=-1)
    gates = jnp.pad(jnp.concatenate([mi, mf], axis=-1), ((0, 0), (0, 0), (0, HP - 2 * NH)))
    w_in_p = jnp.concatenate([_pad_heads(mq, -1), _pad_heads(mk, -1), _pad_heads(mv, -1), _pad_heads(mo, -1),
                              gates, pu, hq, hf, _pad_heads(hi, -1), _pad_heads(hg, -1)], axis=-1)
    w_out_p = jnp.concatenate([_pad_heads(w_out[:, 0:m_w], 1), w_out[:, m_w:m_w + P_W],
                               _pad_heads(w_out[:, m_w + P_W:], 1)], axis=1)
    gbias = jnp.pad(jnp.concatenate([b_igate, b_fgate], axis=-1), ((0, 0), (0, HP - 2 * NH)))[:, None, :]
    conv_p = jnp.concatenate([_pad_heads(conv_w[..., 0:m_w], -1), _pad_heads(conv_w[..., m_w:], -1)], axis=-1)
    eye = jnp.eye(len(POOL_WINDOWS), dtype=F32)
    pool_bd = jnp.einsum('lgce,gh->lgche', pool_w, eye).reshape(DEPTH, P_W, P_W)
    return {
        "g1": norm1_g[:, None, :], "w_in": w_in_p.astype(BF16), "w_in_f32": w_in_p, "gbias": gbias,
        "conv_w": conv_p, "mnorm": _pad_heads(mlstm_norm_g, -1)[:, None, :],
        "pool_w": pool_bd.astype(BF16), "pool_w_f32": pool_bd,
        "pool_scale": pool_scale[:, None, :], "lb_logits": lb_logits,
        "hnorm": _pad_heads(hgrn_norm_g, -1)[:, None, :], "w_out": w_out_p.astype(BF16), "w_out_f32": w_out_p,
    }


def _states_to_kernel(C, n, m, conv, pool, S):
    nb = C.shape[0]
    caug = jnp.concatenate([C, n[..., None]], axis=-1)
    caug = jnp.pad(caug, ((0, 0), (0, 0), (0, HP - DH), (0, HP - DH - 1)))
    mk = jnp.pad(jnp.broadcast_to(m[:, :, None], (nb, NH, HP)), ((0, 0), (0, 8 - NH), (0, 0)))
    m_w = NH * DH
    convk = jnp.concatenate([_pad_heads(conv[..., 0:m_w], -1), _pad_heads(conv[..., m_w:], -1)], axis=-1)
    convk = jnp.pad(convk, ((0, 0), (CONV_HDR - 3, 0), (0, 0)))
    poolk = jnp.pad(pool, ((0, 0), (1, 0), (0, 0)))
    sk = jnp.pad(jnp.swapaxes(S, -1, -2), ((0, 0), (0, 0), (0, HP - DH), (0, 0)))
    return caug, mk, convk, poolk, sk


def _states_from_kernel(st):
    caug, mk, convk, poolk, sk = st
    C = caug[:, :, 0:DH, 0:DH]
    n = caug[:, :, 0:DH, DH]
    m = mk[:, 0:NH, 0]
    conv = convk[:, CONV_HDR - 3:, :]
    conv = jnp.concatenate([_unpad_heads(conv[..., 0:HW], -1), _unpad_heads(conv[..., HW:], -1)], axis=-1)
    pool = poolk[:, 1:, :]
    S = jnp.swapaxes(sk[:, :, 0:DH, :], -1, -2)
    return C, n, m, conv, pool, S


def _pick_tile(n, candidates):
    for c in candidates:
        if n % c == 0:
            return c
    raise ValueError(f"no row tile for {n}")


def kernel(x_prompt, x_sample, state_mlstm_C, state_mlstm_n, state_mlstm_m, state_mlstm_conv, state_pool,
           state_hgrn, meta_tokens, norm1_g, norm2_g, final_g, w_in, b_igate, b_fgate, conv_w, mlstm_norm_g,
           pool_w, pool_scale, lb_logits, hgrn_norm_g, w_out, ffn_w1, ffn_w3, ffn_w2, router_w, moe_w1,
           moe_w3, moe_w2):
    B, T, _ = x_prompt.shape
    SB, ST, _ = x_sample.shape
    w = _prep_weights(norm1_g, w_in, b_igate, b_fgate, conv_w, mlstm_norm_g, pool_w, pool_scale, lb_logits,
                      hgrn_norm_g, w_out)
    ffn_w1b, ffn_w3b, ffn_w2b = ffn_w1.astype(BF16), ffn_w3.astype(BF16), ffn_w2.astype(BF16)
    router_p = jnp.pad(router_w, ((0, 0), (0, 0), (0, HP - N_EXPERTS)))

    n_main = B * T
    off_s = n_main
    off_m = off_s + SB * ST
    assert SB * ST == TAIL and n_main % TAIL == 0
    n_tok = off_m + META_TOKENS
    tile = 1280 if n_main >= 16384 else 256
    expert_tile = 1024 if n_main >= 16384 else 256
    n_pad = -(-n_tok // tile) * tile
    xall = jnp.concatenate([x_prompt.reshape(n_main, D_MODEL), x_sample.reshape(SB * ST, D_MODEL),
                            meta_tokens.astype(F32), jnp.zeros((n_pad - n_tok, D_MODEL), F32)], axis=0)
    tv_main = _pick_tile(T, (256, 128))

    zero_states = (jnp.zeros((1, NH, HP, HP), F32), jnp.zeros((1, 8, HP), F32),
                   jnp.zeros((1, CONV_HDR, 2 * HW), F32), jnp.zeros((1, 16, P_W), F32),
                   jnp.zeros((1, NH, HP, HP), F32))
    p_states, s_states = [], []
    st_in = jax.vmap(_states_to_kernel)(state_mlstm_C, state_mlstm_n, state_mlstm_m, state_mlstm_conv,
                                        state_pool, state_hgrn)
    tail_blocks = tuple((b * T + T - TAIL) // TAIL for b in range(B)) + (off_s // TAIL,)
    for l in range(DEPTH):
        precise = l < PRECISE_LAYERS
        xall, st_meta = _mixer_call(xall, zero_states, True, w, l, row_off=off_m, seq_stride=META_TOKENS, nb=1,
                                    seq=META_TOKENS, tv=META_TOKENS, pos0=0, precise=False,
                                    name=f"mixer_meta_{l}")
        if precise:
            xall, st_p = _mixer_call(xall, st_meta, True, w, l, row_off=0, seq_stride=T, nb=B, seq=T - TAIL,
                                     tv=tv_main, pos0=META_TOKENS, precise=False, name=f"mixer_prompt_{l}")
            xall, st_p = _mixer_call(xall, st_p, False, w, l, row_off=T - TAIL, seq_stride=T, nb=B, seq=TAIL,
                                     tv=TAIL, pos0=META_TOKENS + T - TAIL, precise=True,
                                     name=f"mixer_prompt_tail_{l}")
        else:
            xall, st_p = _mixer_call(xall, st_meta, True, w, l, row_off=0, seq_stride=T, nb=B, seq=T,
                                     tv=tv_main, pos0=META_TOKENS, precise=False, name=f"mixer_prompt_{l}")
        xall, st_s = _mixer_call(xall, st_in, False, w, l, row_off=off_s, seq_stride=ST, nb=SB, seq=ST, tv=ST,
                                 pos0=META_TOKENS + PAST_LEN, precise=True, name=f"mixer_sample_{l}",
                                 stacked_states=True)
        p_states.append(st_p)
        s_states.append(st_s)
        i = l // 2
        if l % 2 == 0:
            f32_blocks = tail_blocks if precise else tail_blocks[-1:]
            rows = _ffn_rows_f32_call(xall, f32_blocks, norm2_g[l][None, :], ffn_w1[i], ffn_w3[i], ffn_w2[i],
                                      tm=TAIL, tf=512, name=f"ffn_rows_f32_{l}")
            xall = _ffn_call(xall, norm2_g[l][None, :], ffn_w1b[i], ffn_w3b[i], ffn_w2b[i],
                             tm=tile, tf=512, name=f"ffn_{l}")
            for j, blk in enumerate(f32_blocks):
                xall = lax.dynamic_update_slice(xall, rows[j * TAIL:(j + 1) * TAIL], (blk * TAIL, 0))
        else:
            xall = _moe_layer(xall, norm2_g[l][None, :], router_p[i], moe_w1[i], moe_w3[i], moe_w2[i],
                              final_g[None, :], tb=tile, tm=expert_tile, tf=512, final_norm=(l == DEPTH - 1),
                              tag=str(l))
    y_prompt = xall[0:n_main].reshape(B, T, D_MODEL)
    y_sample = xall[off_s:off_m].reshape(SB, ST, D_MODEL)
    p_out = jax.vmap(_states_from_kernel)(tuple(jnp.stack([s[j] for s in p_states], axis=0) for j in range(5)))
    s_out = jax.vmap(_states_from_kernel)(tuple(jnp.stack([s[j] for s in s_states], axis=0) for j in range(5)))
    return (y_prompt, y_sample) + tuple(p_out) + tuple(s_out)
```

```python
import functools

import jax
import jax.numpy as jnp
from jax import lax
from jax.experimental import pallas as pl
from jax.experimental.pallas import tpu as pltpu

F32 = jnp.float32
BF16 = jnp.bfloat16

D_MODEL = 1024
DEPTH = 4
META_TOKENS = 16
PAST_LEN = 4096
EPS = 1e-6
NH = 4
DH = 96
HP = 128
HW = NH * HP
P_W = 256
P_GW = 64
POOL_WINDOWS = (2, 4, 8, 16)
D_FF = 3584
N_EXPERTS = 8
NEG = -1e30

OFF_MQ, OFF_MK, OFF_MV, OFF_MO = 0, 512, 1024, 1536
OFF_G = 2048
OFF_PU = 2176
OFF_HQ, OFF_HF, OFF_HI, OFF_HG = 2432, 2944, 3456, 3968
D_INP = 4480
OFF_YM, OFF_YP, OFF_YH = 0, 512, 768
D_MIXP = 1280

M_CHUNK = 128
H_CHUNK = 64
H_SUB = 16
H_SAFE_LOG_DECAY = -60.0
CONV_HDR = 8
POOL_HDR = 32

TAIL = 256
PRECISE_LAYERS = 2

VMEM_LIMIT = 60 * 1024 * 1024


def _sigmoid(x):
    return 1.0 / (1.0 + jnp.exp(-x))


X3 = "bf16x3"


def _split_bf16(a):
    hi = a.astype(BF16)
    return hi, (a - hi.astype(F32)).astype(BF16)


def _dg(a, b, dims, precision=None):
    if precision == X3:
        ah, al = _split_bf16(a)
        bh, bl = _split_bf16(b)
        return (lax.dot_general(ah, bh, dims, preferred_element_type=F32)
                + lax.dot_general(al, bh, dims, preferred_element_type=F32)
                + lax.dot_general(ah, bl, dims, preferred_element_type=F32))
    return lax.dot_general(a, b, dims, preferred_element_type=F32, precision=precision)


def _dot(a, b, precision=None):
    return _dg(a, b, (((1,), (0,)), ((), ())), precision)


def _dot_nt(a, b, precision=None):
    return _dg(a, b, (((1,), (1,)), ((), ())), precision)


def _dot_tn(a, b, precision=None):
    return _dg(a, b, (((0,), (0,)), ((), ())), precision)


def _cumsum_rows(tril_bf, x, precise=False):
    n = x.shape[1]
    hi = x.astype(BF16)
    rest = x - hi.astype(F32)
    lo = rest.astype(BF16)
    parts = [hi, lo] + ([(rest - lo.astype(F32)).astype(BF16)] if precise else [])
    both = _dot(tril_bf, jnp.concatenate(parts, axis=1))
    out = both[:, :n] + both[:, n:2 * n]
    return out + both[:, 2 * n:] if precise else out


def _bcast_rows(row, n):
    return jnp.broadcast_to(row, (n, row.shape[1]))


def _mixer_kernel(x_ref, c0_ref, m0_ref, conv0_ref, pool0_ref, s0_ref,
                  g1_ref, win_ref, gb_ref, cw_ref, mng_ref, pw_ref, ps_ref, lbl_ref, hng_ref, wout_ref,
                  xo_ref, cf_ref, mf_ref, convf_ref, poolf_ref, sf_ref,
                  proj_s, qk_s, u_s, s2_s, s4_s, s8_s, hk_s, gate_s, y_s, xres_s, gx_s, st_old_s, c_s, m_s, st_s,
                  *, layer, tv, lb, pos0, n_t, precise, pipelined):
    pad = lb - tv
    t = pl.program_id(1)
    n_steps = n_t + 1 if pipelined else n_t
    if pipelined:
        ia = lax.rem(t, 2)
        ib = 1 - ia
        live = t >= 1
    else:
        ia = ib = 0
        live = None
    prec = X3 if precise else None
    dot = functools.partial(_dot, precision=prec)
    dot_nt = functools.partial(_dot_nt, precision=prec)
    dot_tn = functools.partial(_dot_tn, precision=prec)

    def mm(a):
        return a if precise else a.astype(BF16)

    def keep(new, old):
        return new if live is None else jnp.where(live, new, old)

    @pl.when(t == 0)
    def _init():
        c_s[...] = c0_ref[...]
        m_s[...] = m0_ref[...]
        st_s[...] = s0_ref[...]
        qk_s[...] = jnp.zeros(qk_s.shape, F32)
        qk_s[pad:pad + CONV_HDR, :] = conv0_ref[...]
        u_s[...] = jnp.zeros(u_s.shape, F32)
        u_s[pad + 16:pad + 32, :] = pool0_ref[...]
        s2_s[0:POOL_HDR, :] = jnp.zeros((POOL_HDR, P_W), F32)
        s4_s[0:POOL_HDR, :] = jnp.zeros((POOL_HDR, P_W), F32)
        s8_s[0:POOL_HDR, :] = jnp.zeros((POOL_HDR, P_W), F32)
        if pad:
            proj_s[0, 0:pad, :] = jnp.zeros((pad, D_INP), F32)
        if pipelined:
            proj_s[1] = jnp.zeros(proj_s.shape[1:], F32)
            hk_s[1] = jnp.zeros(hk_s.shape[1:], F32)
            gate_s[1] = jnp.zeros(gate_s.shape[1:], F32)
            y_s[1] = jnp.zeros(y_s.shape[1:], y_s.dtype)
            xres_s[1] = jnp.zeros(xres_s.shape[1:], F32)

    x = x_ref[...]
    ms = jnp.mean(x * x, axis=-1, keepdims=True)
    hn = mm(x * lax.rsqrt(ms + EPS) * g1_ref[...])
    qk_s[CONV_HDR + pad:CONV_HDR + lb, :] = dot(hn, win_ref[:, 0:OFF_MV])
    proj_s[ia, pad:lb, OFF_MV:OFF_PU] = dot(hn, win_ref[:, OFF_MV:OFF_PU])
    u_s[POOL_HDR + pad:POOL_HDR + lb, :] = dot(hn, win_ref[:, OFF_PU:OFF_HQ])
    proj_s[ia, pad:lb, OFF_HQ:D_INP] = dot(hn, win_ref[:, OFF_HQ:D_INP])
    if pipelined:
        xres_s[ia] = x

    row = lax.broadcasted_iota(jnp.int32, (lb, HP), 0)
    lane = lax.broadcasted_iota(jnp.int32, (lb, HP), 1)

    acc = qk_s[5:5 + lb, :] * cw_ref[0:1, :]
    for j in range(1, 4):
        acc = acc + qk_s[5 + j:5 + j + lb, :] * cw_ref[j:j + 1, :]
    qk = acc * _sigmoid(acc)
    proj_s[ia, :, OFF_MQ:OFF_MK] = qk[:, 0:HW]
    proj_s[ia, :, OFF_MK:OFF_MV] = qk[:, HW:2 * HW] * (DH ** -0.5)
    conv_tail = qk_s[lb:lb + CONV_HDR, :]
    qk_s[0:CONV_HDR, :] = conv_tail

    gpre = proj_s[ia, :, OFF_G:OFF_G + HP] + gb_ref[...]
    lsig = jnp.minimum(gpre, 0.0) - jnp.log(1.0 + jnp.exp(-jnp.abs(gpre)))
    gates = jnp.where(lane < NH, gpre, jnp.where(lane < 2 * NH, lsig, 0.0))
    if pad:
        gates = jnp.where(row >= pad, gates, jnp.where(lane < NH, NEG, 0.0))
    gate_s[ia] = gates

    n_ext = lb + 16
    s2_s[16:16 + n_ext, :] = u_s[16:16 + n_ext, :] + u_s[15:15 + n_ext, :]
    s4_s[16:16 + n_ext, :] = s2_s[16:16 + n_ext, :] + s2_s[14:14 + n_ext, :]
    s8_s[16:16 + n_ext, :] = s4_s[16:16 + n_ext, :] + s4_s[12:12 + n_ext, :]
    u_cur = u_s[POOL_HDR:POOL_HDR + lb, :]
    w2 = s2_s[POOL_HDR:POOL_HDR + lb, :]
    w4 = s4_s[POOL_HDR:POOL_HDR + lb, :]
    w8 = s8_s[POOL_HDR:POOL_HDR + lb, :]
    w16 = w8 + s8_s[POOL_HDR - 8:POOL_HDR - 8 + lb, :]
    lane_p = lax.broadcasted_iota(jnp.int32, (lb, P_W), 1)
    wsum = jnp.where(lane_p < P_GW, w2, jnp.where(lane_p < 2 * P_GW, w4, jnp.where(lane_p < 3 * P_GW, w8, w16)))
    if pos0 >= POOL_WINDOWS[-1] - 1:
        inv = jnp.where(lane_p < P_GW, 0.5, jnp.where(lane_p < 2 * P_GW, 0.25,
                                                       jnp.where(lane_p < 3 * P_GW, 0.125, 0.0625)))
        mean = wsum * inv
    else:
        row_p = lax.broadcasted_iota(jnp.int32, (lb, P_W), 0)
        posn = (row_p + (pos0 + 1 - pad + t * tv)).astype(F32)
        wlen = jnp.where(lane_p < P_GW, 2.0, jnp.where(lane_p < 2 * P_GW, 4.0,
                                                        jnp.where(lane_p < 3 * P_GW, 8.0, 16.0)))
        mean = wsum / jnp.maximum(jnp.minimum(wlen, posn), 1.0)
    pooled = mm(mean - u_cur)
    y_s[ia, :, OFF_YP:OFF_YH] = mm(dot(pooled, pw_ref[...]) * ps_ref[...])
    pool_tail = u_s[lb:lb + POOL_HDR, :]
    u_s[0:POOL_HDR, :] = pool_tail

    lbl = lbl_ref[...]
    e = jnp.exp(lbl - jnp.max(lbl, axis=0, keepdims=True))
    p = e / jnp.sum(e, axis=0, keepdims=True)
    lbv = jnp.sum(p[0:layer + 1, :], axis=0, keepdims=True) - p[0:1, :]
    hq = proj_s[ia, :, OFF_HQ:OFF_HF]
    hf = proj_s[ia, :, OFF_HF:OFF_HI]
    fg = lbv + (1.0 - lbv) * _sigmoid(hf)
    kh = 1.0 - fg
    lg = jnp.log(fg)
    if pad:
        row_h = lax.broadcasted_iota(jnp.int32, (lb, HW), 0)
        kh = jnp.where(row_h >= pad, kh, 0.0)
        lg = jnp.where(row_h >= pad, lg, 0.0)
    proj_s[ia, :, OFF_HQ:OFF_HF] = hq * _sigmoid(hq)
    proj_s[ia, :, OFF_HF:OFF_HI] = lg
    hk_s[ia] = kh

    rr = lax.broadcasted_iota(jnp.int32, (M_CHUNK, M_CHUNK), 0)
    cc = lax.broadcasted_iota(jnp.int32, (M_CHUNK, M_CHUNK), 1)
    causal = rr >= cc
    tril_m = jnp.where(causal, 1.0, 0.0).astype(BF16)
    lane_c = lax.broadcasted_iota(jnp.int32, (M_CHUNK, HP), 1)

    for c in range(lb // M_CHUNK):
        rows = slice(c * M_CHUNK, (c + 1) * M_CHUNK)
        gt = gate_s[ib, rows, :]
        bcum = _cumsum_rows(tril_m, jnp.where(lane_c >= NH, gt, 0.0), precise)
        cg = gt - pltpu.roll(bcum, HP - NH, axis=1)
        cg_t = cg.T
        for h in range(NH):
            q = proj_s[ib, rows, OFF_MQ + h * HP:OFF_MQ + (h + 1) * HP]
            k = proj_s[ib, rows, OFF_MK + h * HP:OFF_MK + (h + 1) * HP]
            v = proj_s[ib, rows, OFF_MV + h * HP:OFF_MV + (h + 1) * HP]
            c_row = cg_t[h:h + 1, :]
            c_col = cg[:, h:h + 1]
            b_col = bcum[:, NH + h:NH + h + 1]
            m_old = m_s[h:h + 1, :]
            m_prev = m_old[:, 0:1]
            mx = jnp.maximum(jnp.max(jnp.where(causal, c_row, NEG), axis=1, keepdims=True), m_prev)
            w = jnp.exp(jnp.where(causal, c_row - mx, NEG))
            w_int = jnp.exp(m_prev - mx)
            mx_last = mx[M_CHUNK - 1:M_CHUNK, :]
            s = dot_nt(mm(q), mm(k)) * w
            vaug = jnp.where(lane_c == DH, 1.0, v)
            caug = c_s[h]
            lhs = mm(jnp.concatenate([s, q * w_int], axis=1))
            rhs = mm(jnp.concatenate([vaug, caug], axis=0))
            nd = dot(lhs, rhs)
            den = nd[:, DH:DH + 1]
            rden = 1.0 / jnp.maximum(jnp.abs(den), jnp.exp(-(b_col + mx)))
            wl = jnp.exp(c_col - mx_last)
            decay = jnp.exp(m_prev - mx_last)
            c_s[h] = keep(decay * caug + dot_tn(mm(k * wl), mm(vaug)), caug)
            m_s[h:h + 1, :] = keep(jnp.broadcast_to(b_col[M_CHUNK - 1:M_CHUNK, :] + mx_last, (1, HP)), m_old)
            mo = proj_s[ib, rows, OFF_MO + h * HP:OFF_MO + (h + 1) * HP]
            z = jnp.where(lane_c < DH, nd * _sigmoid(mo), 0.0)
            ssq = jnp.sum(z * z, axis=1, keepdims=True) * (1.0 / DH)
            fac = rden * lax.rsqrt(rden * rden * ssq + EPS)
            y_s[ib, rows, OFF_YM + h * HP:OFF_YM + (h + 1) * HP] = mm(z * fac * mng_ref[:, h * HP:(h + 1) * HP])

    r64 = lax.broadcasted_iota(jnp.int32, (H_CHUNK, H_CHUNK), 0)
    c64 = lax.broadcasted_iota(jnp.int32, (H_CHUNK, H_CHUNK), 1)
    tril_h = jnp.where(r64 >= c64, 1.0, 0.0).astype(BF16)
    bdiff = r64 // H_SUB - c64 // H_SUB
    mask_intra = (bdiff == 0) & (r64 >= c64)
    n_sub = H_CHUNK // H_SUB

    def h_chunk(c, factored):
        rows = slice(c * H_CHUNK, (c + 1) * H_CHUNK)
        g = _cumsum_rows(tril_h, proj_s[ib, rows, OFF_HF:OFF_HI], precise)
        qh = proj_s[ib, rows, OFF_HQ:OFF_HF]
        khc = hk_s[ib, rows, :]
        if not factored:
            gx_s[...] = g
        bnd = [jnp.zeros((1, HW), F32)] + [g[(j + 1) * H_SUB - 1:(j + 1) * H_SUB, :] for j in range(n_sub)]
        g_start = jnp.concatenate([_bcast_rows(bnd[j], H_SUB) for j in range(n_sub)], axis=0)
        g_end = jnp.concatenate([_bcast_rows(bnd[j + 1], H_SUB) for j in range(n_sub)], axis=0)
        g_last = bnd[n_sub]
        qt = qh * jnp.exp(g - g_start)
        khat = khc * jnp.exp(g_end - g)
        kbar = khc * jnp.exp(g_start - g)
        qg = qh * jnp.exp(g)
        kend = khc * jnp.exp(g_last - g)
        dsub = [jnp.exp(bnd[j + 1] - bnd[j]) for j in range(n_sub - 1)]
        ones = jnp.ones((H_SUB, HW), F32)
        qlev = [qt]
        for d in range(1, n_sub - 1):
            fac = jnp.concatenate([ones] * d + [_bcast_rows(dsub[j - d], H_SUB) for j in range(d, n_sub)], axis=0)
            qlev.append(qlev[-1] * fac)
        for h in range(NH):
            sl = slice(h * HP, (h + 1) * HP)
            if factored:
                lhs = mm(jnp.concatenate([ql[:, sl] for ql in qlev], axis=0))
                inter = dot_nt(lhs, mm(khat[:, sl]))
                intra = dot_nt(mm(qt[:, sl]), mm(kbar[:, sl]))
                att = jnp.where(mask_intra, intra, 0.0)
                for d in range(n_sub - 1):
                    att = att + jnp.where(bdiff == d + 1, inter[d * H_CHUNK:(d + 1) * H_CHUNK, :], 0.0)
            else:
                def cols(j, att_acc, sl=sl, gh=g[:, sl], qhh=qh[:, sl]):
                    r0 = pl.multiple_of(j * 8, 8)
                    g8 = gx_s[pl.ds(r0, 8), sl]
                    k8 = hk_s[ib, pl.ds(c * H_CHUNK + r0, 8), sl]
                    for r in range(8):
                        wgt = jnp.exp(jnp.minimum(gh - g8[r:r + 1, :], 0.0))
                        pcol = jnp.sum(qhh * wgt * k8[r:r + 1, :], axis=1, keepdims=True)
                        att_acc = att_acc + jnp.where((c64 == r0 + r) & (r64 >= r0 + r), pcol, 0.0)
                    return att_acc
                att = lax.fori_loop(0, H_CHUNK // 8, cols, jnp.zeros((H_CHUNK, H_CHUNK), F32))
            iv = mm(proj_s[ib, rows, OFF_HI + h * HP:OFF_HI + (h + 1) * HP])
            st = st_s[h]
            o = dot(mm(att), iv) + dot_nt(mm(qg[:, sl]), mm(st))
            st_s[h] = keep(st * jnp.exp(g_last[:, sl]) + dot_tn(iv, mm(kend[:, sl])), st)
            msq = jnp.sum(o * o, axis=1, keepdims=True) * (1.0 / DH)
            on = o * lax.rsqrt(msq + EPS) * hng_ref[:, sl]
            hg = proj_s[ib, rows, OFF_HG + h * HP:OFF_HG + (h + 1) * HP]
            y_s[ib, rows, OFF_YH + h * HP:OFF_YH + (h + 1) * HP] = mm(on * (hg * _sigmoid(hg)))

    st_old_s[...] = st_s[...]
    for c in range(lb // H_CHUNK):
        h_chunk(c, True)

    xres = xres_s[ib] if pipelined else x

    def out_proj():
        out = dot(y_s[ib], wout_ref[...])
        xo_ref[...] = xres + out[pad:lb, :]

    out_proj()
    lg_sub = proj_s[ib, :, OFF_HF:OFF_HI].reshape(lb // H_SUB, H_SUB, HW)
    factor_ok = jnp.min(jnp.sum(lg_sub, axis=1)) > H_SAFE_LOG_DECAY

    @pl.when(jnp.logical_not(factor_ok))
    def _redo_direct():
        st_s[...] = st_old_s[...]
        for c in range(lb // H_CHUNK):
            h_chunk(c, False)
        out_proj()

    @pl.when(t == n_steps - 1)
    def _final():
        cf_ref[...] = c_s[...]
        mf_ref[...] = m_s[...]
        convf_ref[...] = conv_tail
        poolf_ref[...] = pool_tail[16:32, :]
        sf_ref[...] = st_s[...]


def _mixer_call(xall, states, shared_init, w, layer, *, row_off, seq_stride, nb, seq, tv, pos0, precise, name,
                stacked_states=False):
    lb = max(tv, M_CHUNK)
    n_t = seq // tv
    pipelined = False
    n_steps = n_t + 1 if pipelined else n_t
    n_buf = 2 if pipelined else 1
    assert row_off % tv == 0 and seq_stride % tv == 0 and seq % tv == 0
    blk0, blk_stride = row_off // tv, seq_stride // tv
    c0, m0, conv0, pool0, s0 = states
    lead_blk = (None,) if stacked_states else ()
    lead_idx = (layer,) if stacked_states else ()

    def x_map(b, t):
        return (blk0 + b * blk_stride + jnp.minimum(t, n_t - 1), 0)

    def xo_map(b, t):
        return (blk0 + b * blk_stride + (jnp.maximum(t - 1, 0) if pipelined else t), 0)

    def st_map(b, t):
        return lead_idx + (0 if shared_init else b, 0, 0, 0)

    def st_map3(b, t):
        return lead_idx + (0 if shared_init else b, 0, 0)

    def const_spec(shape):
        return pl.BlockSpec(shape, lambda b, t: (0,) * len(shape), pipeline_mode=pl.Buffered(1))

    in_specs = [
        pl.BlockSpec((tv, D_MODEL), x_map),
        pl.BlockSpec(lead_blk + (None, NH, HP, HP), st_map),
        pl.BlockSpec(lead_blk + (None, 8, HP), st_map3),
        pl.BlockSpec(lead_blk + (None, CONV_HDR, 2 * HW), st_map3),
        pl.BlockSpec(lead_blk + (None, 16, P_W), st_map3),
        pl.BlockSpec(lead_blk + (None, NH, HP, HP), st_map),
        const_spec((1, D_MODEL)),
        const_spec((D_MODEL, D_INP)),
        const_spec((1, HP)),
        const_spec((4, 2 * HW)),
        const_spec((1, HW)),
        const_spec((P_W, P_W)),
        const_spec((1, P_W)),
        const_spec((DEPTH, HW)),
        const_spec((1, HW)),
        const_spec((D_MIXP, D_MODEL)),
    ]
    out_specs = [
        pl.BlockSpec((tv, D_MODEL), xo_map),
        pl.BlockSpec((None, NH, HP, HP), lambda b, t: (b, 0, 0, 0)),
        pl.BlockSpec((None, 8, HP), lambda b, t: (b, 0, 0)),
        pl.BlockSpec((None, CONV_HDR, 2 * HW), lambda b, t: (b, 0, 0)),
        pl.BlockSpec((None, 16, P_W), lambda b, t: (b, 0, 0)),
        pl.BlockSpec((None, NH, HP, HP), lambda b, t: (b, 0, 0, 0)),
    ]
    out_shape = [
        jax.ShapeDtypeStruct(xall.shape, F32),
        jax.ShapeDtypeStruct((nb, NH, HP, HP), F32),
        jax.ShapeDtypeStruct((nb, 8, HP), F32),
        jax.ShapeDtypeStruct((nb, CONV_HDR, 2 * HW), F32),
        jax.ShapeDtypeStruct((nb, 16, P_W), F32),
        jax.ShapeDtypeStruct((nb, NH, HP, HP), F32),
    ]
    scratch = [
        pltpu.VMEM((n_buf, lb, D_INP), F32),
        pltpu.VMEM((CONV_HDR + lb, 2 * HW), F32),
        pltpu.VMEM((POOL_HDR + lb, P_W), F32),
        pltpu.VMEM((POOL_HDR + lb, P_W), F32),
        pltpu.VMEM((POOL_HDR + lb, P_W), F32),
        pltpu.VMEM((POOL_HDR + lb, P_W), F32),
        pltpu.VMEM((n_buf, lb, HW), F32),
        pltpu.VMEM((n_buf, lb, HP), F32),
        pltpu.VMEM((n_buf, lb, D_MIXP), F32 if precise else BF16),
        pltpu.VMEM((n_buf, tv, D_MODEL) if pipelined else (1, 8, HP), F32),
        pltpu.VMEM((H_CHUNK, HW), F32),
        pltpu.VMEM((NH, HP, HP), F32),
        pltpu.VMEM((NH, HP, HP), F32),
        pltpu.VMEM((8, HP), F32),
        pltpu.VMEM((NH, HP, HP), F32),
    ]
    sfx = "_f32" if precise else ""
    kern = functools.partial(_mixer_kernel, layer=layer, tv=tv, lb=lb, pos0=pos0, n_t=n_t, precise=precise,
                             pipelined=pipelined)
    outs = pl.pallas_call(
        kern,
        grid=(nb, n_steps),
        in_specs=in_specs,
        out_specs=out_specs,
        out_shape=out_shape,
        scratch_shapes=scratch,
        input_output_aliases={0: 0},
        compiler_params=pltpu.CompilerParams(dimension_semantics=("arbitrary", "arbitrary"),
                                             vmem_limit_bytes=VMEM_LIMIT),
        name=name,
    )(xall, c0, m0, conv0, pool0, s0,
      w["g1"][layer], w["w_in" + sfx][layer], w["gbias"][layer], w["conv_w"][layer], w["mnorm"][layer],
      w["pool_w" + sfx][layer], w["pool_scale"][layer], w["lb_logits"], w["hnorm"][layer],
      w["w_out" + sfx][layer])
    return outs[0], tuple(outs[1:])


def _ffn_kernel(x_ref, g_ref, w1_ref, w3_ref, w2_ref, o_ref, hn_s, *, precise):
    f = pl.program_id(1)
    prec = X3 if precise else None

    @pl.when(f == 0)
    def _start():
        x = x_ref[...]
        ms = jnp.mean(x * x, axis=-1, keepdims=True)
        hn_s[...] = (x * lax.rsqrt(ms + EPS) * g_ref[...]).astype(hn_s.dtype)
        o_ref[...] = x

    hn = hn_s[...]
    h1 = _dot(hn, w1_ref[...], prec)
    h3 = _dot(hn, w3_ref[...], prec)
    a = (h1 * _sigmoid(h1) * h3).astype(hn_s.dtype)
    o_ref[...] += _dot(a, w2_ref[...], prec)


def _ffn_rows_f32_call(xall, blocks, g, w1, w3, w2, *, tm, tf, name):
    n_f = D_FF // tf

    def x_map(i, f):
        idx = blocks[-1]
        for j in range(len(blocks) - 2, -1, -1):
            idx = jnp.where(i == j, blocks[j], idx)
        return (idx, 0)

    return pl.pallas_call(
        functools.partial(_ffn_kernel, precise=True),
        grid=(len(blocks), n_f),
        in_specs=[
            pl.BlockSpec((tm, D_MODEL), x_map),
            pl.BlockSpec((1, D_MODEL), lambda i, f: (0, 0)),
            pl.BlockSpec((D_MODEL, tf), lambda i, f: (0, f)),
            pl.BlockSpec((D_MODEL, tf), lambda i, f: (0, f)),
            pl.BlockSpec((tf, D_MODEL), lambda i, f: (f, 0)),
        ],
        out_specs=pl.BlockSpec((tm, D_MODEL), lambda i, f: (i, 0)),
        out_shape=jax.ShapeDtypeStruct((len(blocks) * tm, D_MODEL), F32),
        scratch_shapes=[pltpu.VMEM((tm, D_MODEL), F32)],
        compiler_params=pltpu.CompilerParams(dimension_semantics=("arbitrary", "arbitrary"),
                                             vmem_limit_bytes=VMEM_LIMIT),
        name=name,
    )(xall, g, w1, w3, w2)


def _ffn_call(xall, g, w1, w3, w2, *, tm, tf, name):
    n = xall.shape[0]
    n_f = D_FF // tf
    return pl.pallas_call(
        functools.partial(_ffn_kernel, precise=False),
        grid=(n // tm, n_f),
        in_specs=[
            pl.BlockSpec((tm, D_MODEL), lambda i, f: (i, 0)),
            pl.BlockSpec((1, D_MODEL), lambda i, f: (0, 0)),
            pl.BlockSpec((D_MODEL, tf), lambda i, f: (0, f)),
            pl.BlockSpec((D_MODEL, tf), lambda i, f: (0, f)),
            pl.BlockSpec((tf, D_MODEL), lambda i, f: (f, 0)),
        ],
        out_specs=pl.BlockSpec((tm, D_MODEL), lambda i, f: (i, 0)),
        out_shape=jax.ShapeDtypeStruct(xall.shape, F32),
        scratch_shapes=[pltpu.VMEM((tm, D_MODEL), BF16)],
        input_output_aliases={0: 0},
        compiler_params=pltpu.CompilerParams(dimension_semantics=("arbitrary", "arbitrary"),
                                             vmem_limit_bytes=VMEM_LIMIT),
        name=name,
    )(xall, g, w1, w3, w2)


GATHER_TILE = 256
NO_SLOT = -1e9


def _route_kernel(x_ref, g_ref, wr_ref, hn_ref, rank_t_ref, rank_c_ref, comb_ref, cnt_ref, *, tb):
    x = x_ref[...]
    ms = jnp.mean(x * x, axis=-1, keepdims=True)
    hn = x * lax.rsqrt(ms + EPS) * g_ref[...]
    hn_ref[...] = hn.astype(BF16)
    logits = jnp.dot(hn, wr_ref[...], preferred_element_type=F32, precision=lax.Precision.HIGHEST)
    lane = lax.broadcasted_iota(jnp.int32, (tb, HP), 1).astype(F32)
    lg = jnp.where(lane < N_EXPERTS, logits, NEG)
    v1 = jnp.max(lg, axis=1, keepdims=True)
    i1 = jnp.min(jnp.where(lg == v1, lane, float(HP)), axis=1, keepdims=True)
    mask1 = lane == i1
    lg2 = jnp.where(mask1, NEG, lg)
    v2 = jnp.max(lg2, axis=1, keepdims=True)
    i2 = jnp.min(jnp.where(lg2 == v2, lane, float(HP)), axis=1, keepdims=True)
    mask2 = lane == i2
    ex = jnp.exp(v2 - v1)
    ga = 1.0 / (1.0 + ex)
    comb_ref[...] = jnp.where(mask1, ga, 0.0) + jnp.where(mask2, ex * ga, 0.0)
    sel = mask1 | mask2
    rr = lax.broadcasted_iota(jnp.int32, (tb, tb), 0)
    cc = lax.broadcasted_iota(jnp.int32, (tb, tb), 1)
    tril_strict = jnp.where(rr > cc, 1.0, 0.0).astype(BF16)
    selb = jnp.where(sel, 1.0, 0.0)
    rank = jnp.where(sel, _dot(tril_strict, selb.astype(BF16)), NO_SLOT)
    rank_c_ref[...] = rank
    rank_t_ref[...] = rank.T[0:N_EXPERTS, :]
    cnt_ref[...] = jnp.broadcast_to(jnp.sum(selb, axis=0, keepdims=True), (8, HP))


def _route_call(xall, g, wr, *, tb, name):
    n = xall.shape[0]
    nb = n // tb
    return pl.pallas_call(
        functools.partial(_route_kernel, tb=tb),
        grid=(nb,),
        in_specs=[
            pl.BlockSpec((tb, D_MODEL), lambda i: (i, 0)),
            pl.BlockSpec((1, D_MODEL), lambda i: (0, 0)),
            pl.BlockSpec((D_MODEL, HP), lambda i: (0, 0)),
        ],
        out_specs=[
            pl.BlockSpec((tb, D_MODEL), lambda i: (i, 0)),
            pl.BlockSpec((None, N_EXPERTS, tb), lambda i: (i, 0, 0)),
            pl.BlockSpec((tb, HP), lambda i: (i, 0)),
            pl.BlockSpec((tb, HP), lambda i: (i, 0)),
            pl.BlockSpec((None, 8, HP), lambda i: (i, 0, 0)),
        ],
        out_shape=[
            jax.ShapeDtypeStruct((n, D_MODEL), BF16),
            jax.ShapeDtypeStruct((nb, N_EXPERTS, tb), F32),
            jax.ShapeDtypeStruct((n, HP), F32),
            jax.ShapeDtypeStruct((n, HP), F32),
            jax.ShapeDtypeStruct((nb, 8, HP), F32),
        ],
        compiler_params=pltpu.CompilerParams(dimension_semantics=("arbitrary",), vmem_limit_bytes=VMEM_LIMIT),
        name=name,
    )(xall, g, wr)


def _gather_kernel(tile_ref, blk_ref, exp_ref, off_ref, first_ref, act_ref, rank_t_ref, hn_ref, init_ref, xs_ref,
                   *, tb):
    del tile_ref, blk_ref, init_ref
    p = pl.program_id(0)

    @pl.when(act_ref[p] == 1)
    def _():
        rrow = rank_t_ref[pl.ds(exp_ref[p], 1), :]
        slot = (lax.broadcasted_iota(jnp.int32, (GATHER_TILE, tb), 0) - off_ref[p]).astype(F32)
        onehot = jnp.where(rrow == slot, 1.0, 0.0).astype(BF16)
        val = _dot(onehot, hn_ref[...]).astype(BF16)

        @pl.when(first_ref[p] == 1)
        def _set():
            xs_ref[...] = val

        @pl.when(first_ref[p] == 0)
        def _add():
            xs_ref[...] += val


def _gather_call(pairs, rank_t, hn, n_rows, *, tb, name):
    n_pairs = pairs[0].shape[0]
    grid_spec = pltpu.PrefetchScalarGridSpec(
        num_scalar_prefetch=6,
        grid=(n_pairs,),
        in_specs=[
            pl.BlockSpec((None, N_EXPERTS, tb), lambda p, tile, blk, *_: (blk[p], 0, 0)),
            pl.BlockSpec((tb, D_MODEL), lambda p, tile, blk, *_: (blk[p], 0)),
            pl.BlockSpec(memory_space=pl.ANY),
        ],
        out_specs=pl.BlockSpec((GATHER_TILE, D_MODEL), lambda p, tile, *_: (tile[p], 0)),
    )
    return pl.pallas_call(
        functools.partial(_gather_kernel, tb=tb),
        grid_spec=grid_spec,
        out_shape=jax.ShapeDtypeStruct((n_rows, D_MODEL), BF16),
        input_output_aliases={8: 0},
        compiler_params=pltpu.CompilerParams(dimension_semantics=("arbitrary",), vmem_limit_bytes=VMEM_LIMIT),
        name=name,
    )(*pairs, rank_t, hn, jnp.zeros((n_rows, D_MODEL), BF16))


def _experts_kernel(exp_ref, act_ref, xs_ref, w1_ref, w3_ref, w2_ref, ys_ref, acc_s, *, n_f):
    del exp_ref
    i = pl.program_id(0)
    f = pl.program_id(1)

    @pl.when(f == 0)
    def _zero():
        acc_s[...] = jnp.zeros(acc_s.shape, F32)

    @pl.when(act_ref[i] == 1)
    def _():
        xe = xs_ref[...]
        h1 = _dot(xe, w1_ref[...].astype(BF16))
        h3 = _dot(xe, w3_ref[...].astype(BF16))
        a = (h1 * _sigmoid(h1) * h3).astype(BF16)
        acc_s[...] += _dot(a, w2_ref[...].astype(BF16))

    @pl.when(f == n_f - 1)
    def _out():
        ys_ref[...] = acc_s[...].astype(BF16)


def _experts_call(tile_exp, tile_act, xs, w1, w3, w2, *, tm, tf, name):
    n_rows = xs.shape[0]
    n_f = D_FF // tf
    grid_spec = pltpu.PrefetchScalarGridSpec(
        num_scalar_prefetch=2,
        grid=(n_rows // tm, n_f),
        in_specs=[
            pl.BlockSpec((tm, D_MODEL), lambda i, f, ex, act: (i, 0)),
            pl.BlockSpec((None, D_MODEL, tf), lambda i, f, ex, act: (ex[i], 0, f)),
            pl.BlockSpec((None, D_MODEL, tf), lambda i, f, ex, act: (ex[i], 0, f)),
            pl.BlockSpec((None, tf, D_MODEL), lambda i, f, ex, act: (ex[i], f, 0)),
        ],
        out_specs=pl.BlockSpec((tm, D_MODEL), lambda i, f, ex, act: (i, 0)),
        scratch_shapes=[pltpu.VMEM((tm, D_MODEL), F32)],
    )
    return pl.pallas_call(
        functools.partial(_experts_kernel, n_f=n_f),
        grid_spec=grid_spec,
        out_shape=jax.ShapeDtypeStruct((n_rows, D_MODEL), BF16),
        compiler_params=pltpu.CompilerParams(dimension_semantics=("arbitrary", "arbitrary"),
                                             vmem_limit_bytes=VMEM_LIMIT),
        name=name,
    )(tile_exp, tile_act, xs, w1, w3, w2)


def _combine_kernel(blk_ref, tile_ref, exp_ref, off_ref, first_ref, last_ref, act_ref,
                    x_ref, rank_c_ref, comb_ref, ys_ref, fg_ref, o_ref, *, tb, final_norm):
    del blk_ref, tile_ref
    p = pl.program_id(0)

    @pl.when(first_ref[p] == 1)
    def _start():
        o_ref[...] = x_ref[...]

    @pl.when(act_ref[p] == 1)
    def _():
        pick = lax.broadcasted_iota(jnp.int32, (tb, HP), 1) == exp_ref[p]
        rcol = jnp.sum(jnp.where(pick, rank_c_ref[...], 0.0), axis=1, keepdims=True)
        gcol = jnp.sum(jnp.where(pick, comb_ref[...], 0.0), axis=1, keepdims=True)
        slot = (lax.broadcasted_iota(jnp.int32, (tb, GATHER_TILE), 1) - off_ref[p]).astype(F32)
        onehot = jnp.where(rcol == slot, 1.0, 0.0).astype(BF16)
        o_ref[...] += gcol * _dot(onehot, ys_ref[...])

    if final_norm:
        @pl.when(last_ref[p] == 1)
        def _norm():
            y = o_ref[...]
            ms = jnp.mean(y * y, axis=-1, keepdims=True)
            o_ref[...] = y * lax.rsqrt(ms + EPS) * fg_ref[...]


def _combine_call(pairs, xall, rank_c, comb, ys, fg, *, tb, final_norm, name):
    n_pairs = pairs[0].shape[0]
    grid_spec = pltpu.PrefetchScalarGridSpec(
        num_scalar_prefetch=7,
        grid=(n_pairs,),
        in_specs=[
            pl.BlockSpec((tb, D_MODEL), lambda p, blk, *_: (blk[p], 0)),
            pl.BlockSpec((tb, HP), lambda p, blk, *_: (blk[p], 0)),
            pl.BlockSpec((tb, HP), lambda p, blk, *_: (blk[p], 0)),
            pl.BlockSpec((GATHER_TILE, D_MODEL), lambda p, blk, tile, *_: (tile[p], 0)),
            pl.BlockSpec((1, D_MODEL), lambda p, *_: (0, 0)),
        ],
        out_specs=pl.BlockSpec((tb, D_MODEL), lambda p, blk, *_: (blk[p], 0)),
    )
    return pl.pallas_call(
        functools.partial(_combine_kernel, tb=tb, final_norm=final_norm),
        grid_spec=grid_spec,
        out_shape=jax.ShapeDtypeStruct(xall.shape, F32),
        input_output_aliases={7: 0},
        compiler_params=pltpu.CompilerParams(dimension_semantics=("arbitrary",), vmem_limit_bytes=VMEM_LIMIT),
        name=name,
    )(*pairs, xall, rank_c, comb, ys, fg)


def _pair_lists(cnt, *, tm, n_rows):
    nb = cnt.shape[0]
    i32 = jnp.int32
    tot = jnp.sum(cnt, axis=0)
    grp_rows = (tot + tm - 1) // tm * tm
    grp_end = jnp.cumsum(grp_rows)
    grp_start = grp_end - grp_rows
    base = grp_start[None, :] + jnp.cumsum(cnt, axis=0) - cnt
    lo = base // GATHER_TILE
    hi = (base + cnt - 1) // GATHER_TILE
    npair = jnp.where(cnt > 0, hi - lo + 1, 0)
    n_pairs = nb * N_EXPERTS + n_rows // GATHER_TILE
    n_keys = nb * N_EXPERTS

    def expand(expert_major):
        def flat(a):
            return (a.T if expert_major else a).reshape(-1)
        np_k = flat(npair)
        cum = jnp.cumsum(np_k)
        total = cum[-1]
        p = jnp.arange(n_pairs, dtype=i32)
        pc = jnp.minimum(p, total - 1)
        k = jnp.sum((cum[None, :] <= pc[:, None]).astype(i32), axis=1)
        table = jnp.stack([flat(lo), flat(base), cum - np_k], axis=1)
        row = jnp.take(table, jnp.minimum(k, n_keys - 1), axis=0)
        tile = row[:, 0] + pc - row[:, 2]
        blk, exp = (k % nb, k // nb) if expert_major else (k // N_EXPERTS, k % N_EXPERTS)
        act = (p < total).astype(i32)
        return tile.astype(i32), blk.astype(i32), exp.astype(i32), (row[:, 1] - tile * GATHER_TILE).astype(i32), act

    g_tile, g_blk, g_exp, g_off, g_act = expand(True)
    g_first = jnp.concatenate([jnp.ones((1,), i32), (g_tile[1:] != g_tile[:-1]).astype(i32)])
    c_tile, c_blk, c_exp, c_off, c_act = expand(False)
    c_first = jnp.concatenate([jnp.ones((1,), i32), (c_blk[1:] != c_blk[:-1]).astype(i32)])
    c_last = jnp.concatenate([(c_blk[1:] != c_blk[:-1]).astype(i32), jnp.ones((1,), i32)])
    c_last = jnp.where(jnp.arange(n_pairs) == jnp.sum(c_act) - 1, 1, c_last) * c_act
    n_tiles = n_rows // tm
    t0 = jnp.arange(n_tiles, dtype=i32) * tm
    t_act = (t0 < grp_end[-1]).astype(i32)
    t_exp = jnp.sum((grp_end[None, :] <= jnp.minimum(t0, grp_end[-1] - 1)[:, None]).astype(i32), axis=1)
    return ((g_tile, g_blk, g_exp, g_off, g_first, g_act),
            (c_blk, c_tile, c_exp, c_off, c_first, c_last, c_act), (t_exp, t_act))


def _moe_layer(xall, g, wr, w1, w3, w2, fg, *, tb, tm, tf, final_norm, tag):
    n = xall.shape[0]
    n_rows = 2 * n + N_EXPERTS * tm
    hn, rank_t, rank_c, comb, cnt = _route_call(xall, g, wr, tb=tb, name=f"moe_route_{tag}")
    cnt = cnt[:, 0, 0:N_EXPERTS].astype(jnp.int32)
    g_pairs, c_pairs, (t_exp, t_act) = _pair_lists(cnt, tm=tm, n_rows=n_rows)
    xs = _gather_call(g_pairs, rank_t, hn, n_rows, tb=tb, name=f"moe_gather_{tag}")
    ys = _experts_call(t_exp, t_act, xs, w1, w3, w2, tm=tm, tf=tf, name=f"moe_experts_{tag}")
    return _combine_call(c_pairs, xall, rank_c, comb, ys, fg, tb=tb, final_norm=final_norm,
                         name=f"moe_combine_{tag}")


def _pad_heads(a, axis):
    axis = axis % a.ndim
    shp = a.shape
    a = a.reshape(shp[:axis] + (NH, DH) + shp[axis + 1:])
    padw = [(0, 0)] * a.ndim
    padw[axis + 1] = (0, HP - DH)
    a = jnp.pad(a, padw)
    return a.reshape(shp[:axis] + (HW,) + shp[axis + 1:])


def _unpad_heads(a, axis):
    axis = axis % a.ndim
    shp = a.shape
    a = a.reshape(shp[:axis] + (NH, HP) + shp[axis + 1:])
    a = lax.slice_in_dim(a, 0, DH, axis=axis + 1)
    return a.reshape(shp[:axis] + (NH * DH,) + shp[axis + 1:])


_M_W = NH * DH
SRC_MQ, SRC_MK, SRC_MV, SRC_MO = 0, _M_W, 2 * _M_W, 3 * _M_W
SRC_MI = 4 * _M_W
SRC_MF = SRC_MI + NH
SRC_PU = SRC_MF + NH
SRC_HQ = SRC_PU + P_W
SRC_HF = SRC_HQ + HW
SRC_HI = SRC_HF + HW
SRC_HG = SRC_HI + _M_W
D_IN = SRC_HG + _M_W


def _w_in_relayout_kernel(w_ref, o32_ref, o16_ref):
    rows = w_ref.shape[0]

    def put(dst, val):
        o32_ref[:, dst:dst + val.shape[1]] = val
        o16_ref[:, dst:dst + val.shape[1]] = val.astype(BF16)

    for src, dst in ((SRC_MQ, OFF_MQ), (SRC_MK, OFF_MK), (SRC_MV, OFF_MV), (SRC_MO, OFF_MO),
                     (SRC_HI, OFF_HI), (SRC_HG, OFF_HG)):
        for h in range(NH):
            put(dst + HP * h, w_ref[:, src + DH * h:src + DH * (h + 1)])
            put(dst + HP * h + DH, jnp.zeros((rows, HP - DH), F32))
    put(OFF_G, w_ref[:, SRC_MI:SRC_MI + NH])
    put(OFF_G + NH, w_ref[:, SRC_MF:SRC_MF + NH])
    put(OFF_G + 2 * NH, jnp.zeros((rows, HP - 2 * NH), F32))
    put(OFF_PU, w_ref[:, SRC_PU:SRC_PU + P_W])
    put(OFF_HQ, w_ref[:, SRC_HQ:SRC_HQ + HW])
    put(OFF_HF, w_ref[:, SRC_HF:SRC_HF + HW])


def _w_in_relayout(w_in, *, tr=256):
    d, k, n = w_in.shape
    assert n == D_IN and k % tr == 0
    return pl.pallas_call(
        _w_in_relayout_kernel,
        grid=(d, k // tr),
        in_specs=[pl.BlockSpec((None, tr, n), lambda l, i: (l, i, 0))],
        out_specs=[pl.BlockSpec((None, tr, D_INP), lambda l, i: (l, i, 0)),
                   pl.BlockSpec((None, tr, D_INP), lambda l, i: (l, i, 0))],
        out_shape=[jax.ShapeDtypeStruct((d, k, D_INP), F32), jax.ShapeDtypeStruct((d, k, D_INP), BF16)],
        compiler_params=pltpu.CompilerParams(dimension_semantics=("arbitrary", "arbitrary"),
                                             vmem_limit_bytes=VMEM_LIMIT),
        name="w_in_relayout",
    )(w_in)


def _prep_weights(norm1_g, w_in, b_igate, b_fgate, conv_w, mlstm_norm_g, pool_w, pool_scale, lb_logits,
                  hgrn_norm_g, w_out):
    m_w = _M_W
    w_in_p, w_in_b = _w_in_relayout(w_in)
    w_out_p = jnp.concatenate([_pad_heads(w_out[:, 0:m_w], 1), w_out[:, m_w:m_w + P_W],
                               _pad_heads(w_out[:, m_w + P_W:], 1)], axis=1)
    gbias = jnp.pad(jnp.concatenate([b_igate, b_fgate], axis=-1), ((0, 0), (0, HP - 2 * NH)))[:, None, :]
    conv_p = jnp.concatenate([_pad_heads(conv_w[..., 0:m_w], -1), _pad_heads(conv_w[..., m_w:], -1)], axis=-1)
    eye = jnp.eye(len(POOL_WINDOWS), dtype=F32)
    pool_bd = jnp.einsum('lgce,gh->lgche', pool_w, eye).reshape(DEPTH, P_W, P_W)
    return {
        "g1": norm1_g[:, None, :], "w_in": w_in_b, "w_in_f32": w_in_p, "gbias": gbias,
        "conv_w": conv_p, "mnorm": _pad_heads(mlstm_norm_g, -1)[:, None, :],
        "pool_w": pool_bd.astype(BF16), "pool_w_f32": pool_bd,
        "pool_scale": pool_scale[:, None, :], "lb_logits": lb_logits,
        "hnorm": _pad_heads(hgrn_norm_g, -1)[:, None, :], "w_out": w_out_p.astype(BF16), "w_out_f32": w_out_p,
    }


def _states_to_kernel(C, n, m, conv, pool, S):
    nb = C.shape[0]
    caug = jnp.concatenate([C, n[..., None]], axis=-1)
    caug = jnp.pad(caug, ((0, 0), (0, 0), (0, HP - DH), (0, HP - DH - 1)))
    mk = jnp.pad(jnp.broadcast_to(m[:, :, None], (nb, NH, HP)), ((0, 0), (0, 8 - NH), (0, 0)))
    m_w = NH * DH
    convk = jnp.concatenate([_pad_heads(conv[..., 0:m_w], -1), _pad_heads(conv[..., m_w:], -1)], axis=-1)
    convk = jnp.pad(convk, ((0, 0), (CONV_HDR - 3, 0), (0, 0)))
    poolk = jnp.pad(pool, ((0, 0), (1, 0), (0, 0)))
    sk = jnp.pad(jnp.swapaxes(S, -1, -2), ((0, 0), (0, 0), (0, HP - DH), (0, 0)))
    return caug, mk, convk, poolk, sk


def _states_from_kernel(st):
    caug, mk, convk, poolk, sk = st
    C = caug[:, :, 0:DH, 0:DH]
    n = caug[:, :, 0:DH, DH]
    m = mk[:, 0:NH, 0]
    conv = convk[:, CONV_HDR - 3:, :]
    conv = jnp.concatenate([_unpad_heads(conv[..., 0:HW], -1), _unpad_heads(conv[..., HW:], -1)], axis=-1)
    pool = poolk[:, 1:, :]
    S = jnp.swapaxes(sk[:, :, 0:DH, :], -1, -2)
    return C, n, m, conv, pool, S


def _pick_tile(n, candidates):
    for c in candidates:
        if n % c == 0:
            return c
    raise ValueError(f"no row tile for {n}")


def kernel(x_prompt, x_sample, state_mlstm_C, state_mlstm_n, state_mlstm_m, state_mlstm_conv, state_pool,
           state_hgrn, meta_tokens, norm1_g, norm2_g, final_g, w_in, b_igate, b_fgate, conv_w, mlstm_norm_g,
           pool_w, pool_scale, lb_logits, hgrn_norm_g, w_out, ffn_w1, ffn_w3, ffn_w2, router_w, moe_w1,
           moe_w3, moe_w2):
    B, T, _ = x_prompt.shape
    SB, ST, _ = x_sample.shape
    w = _prep_weights(norm1_g, w_in, b_igate, b_fgate, conv_w, mlstm_norm_g, pool_w, pool_scale, lb_logits,
                      hgrn_norm_g, w_out)
    ffn_w1b, ffn_w3b, ffn_w2b = ffn_w1.astype(BF16), ffn_w3.astype(BF16), ffn_w2.astype(BF16)
    router_p = jnp.pad(router_w, ((0, 0), (0, 0), (0, HP - N_EXPERTS)))

    n_main = B * T
    off_s = n_main
    off_m = off_s + SB * ST
    assert SB * ST == TAIL and n_main % TAIL == 0
    n_tok = off_m + META_TOKENS
    tile = 1280 if n_main >= 16384 else 256
    expert_tile = 1024 if n_main >= 16384 else 256
    n_pad = -(-n_tok // tile) * tile
    xall = jnp.concatenate([x_prompt.reshape(n_main, D_MODEL), x_sample.reshape(SB * ST, D_MODEL),
                            meta_tokens.astype(F32), jnp.zeros((n_pad - n_tok, D_MODEL), F32)], axis=0)
    tv_main = _pick_tile(T, (256, 128))

    zero_states = (jnp.zeros((1, NH, HP, HP), F32), jnp.zeros((1, 8, HP), F32),
                   jnp.zeros((1, CONV_HDR, 2 * HW), F32), jnp.zeros((1, 16, P_W), F32),
                   jnp.zeros((1, NH, HP, HP), F32))
    p_states, s_states = [], []
    st_in = jax.vmap(_states_to_kernel)(state_mlstm_C, state_mlstm_n, state_mlstm_m, state_mlstm_conv,
                                        state_pool, state_hgrn)
    tail_blocks = tuple((b * T + T - TAIL) // TAIL for b in range(B)) + (off_s // TAIL,)
    for l in range(DEPTH):
        precise = l < PRECISE_LAYERS
        xall, st_meta = _mixer_call(xall, zero_states, True, w, l, row_off=off_m, seq_stride=META_TOKENS, nb=1,
                                    seq=META_TOKENS, tv=META_TOKENS, pos0=0, precise=False,
                                    name=f"mixer_meta_{l}")
        if precise:
            xall, st_p = _mixer_call(xall, st_meta, True, w, l, row_off=0, seq_stride=T, nb=B, seq=T - TAIL,
                                     tv=tv_main, pos0=META_TOKENS, precise=False, name=f"mixer_prompt_{l}")
            xall, st_p = _mixer_call(xall, st_p, False, w, l, row_off=T - TAIL, seq_stride=T, nb=B, seq=TAIL,
                                     tv=TAIL, pos0=META_TOKENS + T - TAIL, precise=True,
                                     name=f"mixer_prompt_tail_{l}")
        else:
            xall, st_p = _mixer_call(xall, st_meta, True, w, l, row_off=0, seq_stride=T, nb=B, seq=T,
                                     tv=tv_main, pos0=META_TOKENS, precise=False, name=f"mixer_prompt_{l}")
        xall, st_s = _mixer_call(xall, st_in, False, w, l, row_off=off_s, seq_stride=ST, nb=SB, seq=ST, tv=ST,
                                 pos0=META_TOKENS + PAST_LEN, precise=True, name=f"mixer_sample_{l}",
                                 stacked_states=True)
        p_states.append(st_p)
        s_states.append(st_s)
        i = l // 2
        if l % 2 == 0:
            f32_blocks = tail_blocks if precise else tail_blocks[-1:]
            rows = _ffn_rows_f32_call(xall, f32_blocks, norm2_g[l][None, :], ffn_w1[i], ffn_w3[i], ffn_w2[i],
                                      tm=TAIL, tf=512, name=f"ffn_rows_f32_{l}")
            xall = _ffn_call(xall, norm2_g[l][None, :], ffn_w1b[i], ffn_w3b[i], ffn_w2b[i],
                             tm=tile, tf=512, name=f"ffn_{l}")
            for j, blk in enumerate(f32_blocks):
                xall = lax.dynamic_update_slice(xall, rows[j * TAIL:(j + 1) * TAIL], (blk * TAIL, 0))
        else:
            xall = _moe_layer(xall, norm2_g[l][None, :], router_p[i], moe_w1[i], moe_w3[i], moe_w2[i],
                              final_g[None, :], tb=tile, tm=expert_tile, tf=512, final_norm=(l == DEPTH - 1),
                              tag=str(l))
    y_prompt = xall[0:n_main].reshape(B, T, D_MODEL)
    y_sample = xall[off_s:off_m].reshape(SB, ST, D_MODEL)
    p_out = jax.vmap(_states_from_kernel)(tuple(jnp.stack([s[j] for s in p_states], axis=0) for j in range(5)))
    s_out = jax.vmap(_states_from_kernel)(tuple(jnp.stack([s[j] for s in s_states], axis=0) for j in range(5)))
    return (y_prompt, y_sample) + tuple(p_out) + tuple(s_out)
```

```python
import functools

import jax
import jax.numpy as jnp
from jax import lax
from jax.experimental import pallas as pl
from jax.experimental.pallas import tpu as pltpu

F32 = jnp.float32
BF16 = jnp.bfloat16

D_MODEL = 1024
DEPTH = 4
META_TOKENS = 16
PAST_LEN = 4096
EPS = 1e-6
NH = 4
DH = 96
HP = 128
HW = NH * HP
P_W = 256
P_GW = 64
POOL_WINDOWS = (2, 4, 8, 16)
D_FF = 3584
N_EXPERTS = 8
NEG = -1e30

OFF_MQ, OFF_MK, OFF_MV, OFF_MO = 0, 512, 1024, 1536
OFF_G = 2048
OFF_PU = 2176
OFF_HQ, OFF_HF, OFF_HI, OFF_HG = 2432, 2944, 3456, 3968
D_INP = 4480
OFF_YM, OFF_YP, OFF_YH = 0, 512, 768
D_MIXP = 1280

M_CHUNK = 128
H_CHUNK = 64
H_SUB = 16
H_SAFE_LOG_DECAY = -60.0
CONV_HDR = 8
POOL_HDR = 32

TAIL = 256
PRECISE_LAYERS = 2

VMEM_LIMIT = 60 * 1024 * 1024


def _sigmoid(x):
    return 1.0 / (1.0 + jnp.exp(-x))


X3 = "bf16x3"


def _split_bf16(a):
    hi = a.astype(BF16)
    return hi, (a - hi.astype(F32)).astype(BF16)


def _dg(a, b, dims, precision=None):
    if precision == X3:
        ah, al = _split_bf16(a)
        bh, bl = _split_bf16(b)
        return (lax.dot_general(ah, bh, dims, preferred_element_type=F32)
                + lax.dot_general(al, bh, dims, preferred_element_type=F32)
                + lax.dot_general(ah, bl, dims, preferred_element_type=F32))
    return lax.dot_general(a, b, dims, preferred_element_type=F32, precision=precision)


def _dot(a, b, precision=None):
    return _dg(a, b, (((1,), (0,)), ((), ())), precision)


def _dot_nt(a, b, precision=None):
    return _dg(a, b, (((1,), (1,)), ((), ())), precision)


def _dot_tn(a, b, precision=None):
    return _dg(a, b, (((0,), (0,)), ((), ())), precision)


def _cumsum_rows(tril_bf, x, precise=False):
    n = x.shape[1]
    hi = x.astype(BF16)
    rest = x - hi.astype(F32)
    lo = rest.astype(BF16)
    parts = [hi, lo] + ([(rest - lo.astype(F32)).astype(BF16)] if precise else [])
    both = _dot(tril_bf, jnp.concatenate(parts, axis=1))
    out = both[:, :n] + both[:, n:2 * n]
    return out + both[:, 2 * n:] if precise else out


def _bcast_rows(row, n):
    return jnp.broadcast_to(row, (n, row.shape[1]))


def _mixer_kernel(x_ref, c0_ref, m0_ref, conv0_ref, pool0_ref, s0_ref,
                  g1_ref, win_ref, gb_ref, cw_ref, mng_ref, pw_ref, ps_ref, lbl_ref, hng_ref, wout_ref,
                  xo_ref, cf_ref, mf_ref, convf_ref, poolf_ref, sf_ref,
                  proj_s, qk_s, u_s, s2_s, s4_s, s8_s, hk_s, gate_s, y_s, xres_s, gx_s, st_old_s, c_s, m_s, st_s,
                  *, layer, tv, lb, pos0, n_t, precise, pipelined):
    pad = lb - tv
    t = pl.program_id(1)
    n_steps = n_t + 1 if pipelined else n_t
    if pipelined:
        ia = lax.rem(t, 2)
        ib = 1 - ia
        live = t >= 1
    else:
        ia = ib = 0
        live = None
    prec = X3 if precise else None
    dot = functools.partial(_dot, precision=prec)
    dot_nt = functools.partial(_dot_nt, precision=prec)
    dot_tn = functools.partial(_dot_tn, precision=prec)

    def mm(a):
        return a if precise else a.astype(BF16)

    def keep(new, old):
        return new if live is None else jnp.where(live, new, old)

    @pl.when(t == 0)
    def _init():
        c_s[...] = c0_ref[...]
        m_s[...] = m0_ref[...]
        st_s[...] = s0_ref[...]
        qk_s[...] = jnp.zeros(qk_s.shape, F32)
        qk_s[pad:pad + CONV_HDR, :] = conv0_ref[...]
        u_s[...] = jnp.zeros(u_s.shape, F32)
        u_s[pad + 16:pad + 32, :] = pool0_ref[...]
        s2_s[0:POOL_HDR, :] = jnp.zeros((POOL_HDR, P_W), F32)
        s4_s[0:POOL_HDR, :] = jnp.zeros((POOL_HDR, P_W), F32)
        s8_s[0:POOL_HDR, :] = jnp.zeros((POOL_HDR, P_W), F32)
        if pad:
            proj_s[0, 0:pad, :] = jnp.zeros((pad, D_INP), F32)
        if pipelined:
            proj_s[1] = jnp.zeros(proj_s.shape[1:], F32)
            hk_s[1] = jnp.zeros(hk_s.shape[1:], F32)
            gate_s[1] = jnp.zeros(gate_s.shape[1:], F32)
            y_s[1] = jnp.zeros(y_s.shape[1:], y_s.dtype)
            xres_s[1] = jnp.zeros(xres_s.shape[1:], F32)

    x = x_ref[...]
    ms = jnp.mean(x * x, axis=-1, keepdims=True)
    hn = mm(x * lax.rsqrt(ms + EPS) * g1_ref[...])
    qk_s[CONV_HDR + pad:CONV_HDR + lb, :] = dot(hn, win_ref[:, 0:OFF_MV])
    proj_s[ia, pad:lb, OFF_MV:OFF_PU] = dot(hn, win_ref[:, OFF_MV:OFF_PU])
    u_s[POOL_HDR + pad:POOL_HDR + lb, :] = dot(hn, win_ref[:, OFF_PU:OFF_HQ])
    proj_s[ia, pad:lb, OFF_HQ:D_INP] = dot(hn, win_ref[:, OFF_HQ:D_INP])
    if pipelined:
        xres_s[ia] = x

    row = lax.broadcasted_iota(jnp.int32, (lb, HP), 0)
    lane = lax.broadcasted_iota(jnp.int32, (lb, HP), 1)

    acc = qk_s[5:5 + lb, :] * cw_ref[0:1, :]
    for j in range(1, 4):
        acc = acc + qk_s[5 + j:5 + j + lb, :] * cw_ref[j:j + 1, :]
    qk = acc * _sigmoid(acc)
    proj_s[ia, :, OFF_MQ:OFF_MK] = qk[:, 0:HW]
    proj_s[ia, :, OFF_MK:OFF_MV] = qk[:, HW:2 * HW] * (DH ** -0.5)
    conv_tail = qk_s[lb:lb + CONV_HDR, :]
    qk_s[0:CONV_HDR, :] = conv_tail

    gpre = proj_s[ia, :, OFF_G:OFF_G + HP] + gb_ref[...]
    lsig = jnp.minimum(gpre, 0.0) - jnp.log(1.0 + jnp.exp(-jnp.abs(gpre)))
    gates = jnp.where(lane < NH, gpre, jnp.where(lane < 2 * NH, lsig, 0.0))
    if pad:
        gates = jnp.where(row >= pad, gates, jnp.where(lane < NH, NEG, 0.0))
    gate_s[ia] = gates

    n_ext = lb + 16
    s2_s[16:16 + n_ext, :] = u_s[16:16 + n_ext, :] + u_s[15:15 + n_ext, :]
    s4_s[16:16 + n_ext, :] = s2_s[16:16 + n_ext, :] + s2_s[14:14 + n_ext, :]
    s8_s[16:16 + n_ext, :] = s4_s[16:16 + n_ext, :] + s4_s[12:12 + n_ext, :]
    u_cur = u_s[POOL_HDR:POOL_HDR + lb, :]
    w2 = s2_s[POOL_HDR:POOL_HDR + lb, :]
    w4 = s4_s[POOL_HDR:POOL_HDR + lb, :]
    w8 = s8_s[POOL_HDR:POOL_HDR + lb, :]
    w16 = w8 + s8_s[POOL_HDR - 8:POOL_HDR - 8 + lb, :]
    lane_p = lax.broadcasted_iota(jnp.int32, (lb, P_W), 1)
    wsum = jnp.where(lane_p < P_GW, w2, jnp.where(lane_p < 2 * P_GW, w4, jnp.where(lane_p < 3 * P_GW, w8, w16)))
    if pos0 >= POOL_WINDOWS[-1] - 1:
        inv = jnp.where(lane_p < P_GW, 0.5, jnp.where(lane_p < 2 * P_GW, 0.25,
                                                       jnp.where(lane_p < 3 * P_GW, 0.125, 0.0625)))
        mean = wsum * inv
    else:
        row_p = lax.broadcasted_iota(jnp.int32, (lb, P_W), 0)
        posn = (row_p + (pos0 + 1 - pad + t * tv)).astype(F32)
        wlen = jnp.where(lane_p < P_GW, 2.0, jnp.where(lane_p < 2 * P_GW, 4.0,
                                                        jnp.where(lane_p < 3 * P_GW, 8.0, 16.0)))
        mean = wsum / jnp.maximum(jnp.minimum(wlen, posn), 1.0)
    pooled = mm(mean - u_cur)
    y_s[ia, :, OFF_YP:OFF_YH] = mm(dot(pooled, pw_ref[...]) * ps_ref[...])
    pool_tail = u_s[lb:lb + POOL_HDR, :]
    u_s[0:POOL_HDR, :] = pool_tail

    lbl = lbl_ref[...]
    e = jnp.exp(lbl - jnp.max(lbl, axis=0, keepdims=True))
    p = e / jnp.sum(e, axis=0, keepdims=True)
    lbv = jnp.sum(p[0:layer + 1, :], axis=0, keepdims=True) - p[0:1, :]
    hq = proj_s[ia, :, OFF_HQ:OFF_HF]
    hf = proj_s[ia, :, OFF_HF:OFF_HI]
    fg = lbv + (1.0 - lbv) * _sigmoid(hf)
    kh = 1.0 - fg
    lg = jnp.log(fg)
    if pad:
        row_h = lax.broadcasted_iota(jnp.int32, (lb, HW), 0)
        kh = jnp.where(row_h >= pad, kh, 0.0)
        lg = jnp.where(row_h >= pad, lg, 0.0)
    proj_s[ia, :, OFF_HQ:OFF_HF] = hq * _sigmoid(hq)
    proj_s[ia, :, OFF_HF:OFF_HI] = lg
    hk_s[ia] = kh

    rr = lax.broadcasted_iota(jnp.int32, (M_CHUNK, M_CHUNK), 0)
    cc = lax.broadcasted_iota(jnp.int32, (M_CHUNK, M_CHUNK), 1)
    causal = rr >= cc
    tril_m = jnp.where(causal, 1.0, 0.0).astype(BF16)
    lane_c = lax.broadcasted_iota(jnp.int32, (M_CHUNK, HP), 1)

    for c in range(lb // M_CHUNK):
        rows = slice(c * M_CHUNK, (c + 1) * M_CHUNK)
        gt = gate_s[ib, rows, :]
        bcum = _cumsum_rows(tril_m, jnp.where(lane_c >= NH, gt, 0.0), precise)
        cg = gt - pltpu.roll(bcum, HP - NH, axis=1)
        cg_t = cg.T
        for h in range(NH):
            q = proj_s[ib, rows, OFF_MQ + h * HP:OFF_MQ + (h + 1) * HP]
            k = proj_s[ib, rows, OFF_MK + h * HP:OFF_MK + (h + 1) * HP]
            v = proj_s[ib, rows, OFF_MV + h * HP:OFF_MV + (h + 1) * HP]
            c_row = cg_t[h:h + 1, :]
            c_col = cg[:, h:h + 1]
            b_col = bcum[:, NH + h:NH + h + 1]
            m_old = m_s[h:h + 1, :]
            m_prev = m_old[:, 0:1]
            mx = jnp.maximum(jnp.max(jnp.where(causal, c_row, NEG), axis=1, keepdims=True), m_prev)
            w = jnp.exp(jnp.where(causal, c_row - mx, NEG))
            w_int = jnp.exp(m_prev - mx)
            mx_last = mx[M_CHUNK - 1:M_CHUNK, :]
            s = dot_nt(mm(q), mm(k)) * w
            vaug = jnp.where(lane_c == DH, 1.0, v)
            caug = c_s[h]
            lhs = mm(jnp.concatenate([s, q * w_int], axis=1))
            rhs = mm(jnp.concatenate([vaug, caug], axis=0))
            nd = dot(lhs, rhs)
            den = nd[:, DH:DH + 1]
            rden = 1.0 / jnp.maximum(jnp.abs(den), jnp.exp(-(b_col + mx)))
            wl = jnp.exp(c_col - mx_last)
            decay = jnp.exp(m_prev - mx_last)
            c_s[h] = keep(decay * caug + dot_tn(mm(k * wl), mm(vaug)), caug)
            m_s[h:h + 1, :] = keep(jnp.broadcast_to(b_col[M_CHUNK - 1:M_CHUNK, :] + mx_last, (1, HP)), m_old)
            mo = proj_s[ib, rows, OFF_MO + h * HP:OFF_MO + (h + 1) * HP]
            z = jnp.where(lane_c < DH, nd * _sigmoid(mo), 0.0)
            ssq = jnp.sum(z * z, axis=1, keepdims=True) * (1.0 / DH)
            fac = rden * lax.rsqrt(rden * rden * ssq + EPS)
            y_s[ib, rows, OFF_YM + h * HP:OFF_YM + (h + 1) * HP] = mm(z * fac * mng_ref[:, h * HP:(h + 1) * HP])

    r64 = lax.broadcasted_iota(jnp.int32, (H_CHUNK, H_CHUNK), 0)
    c64 = lax.broadcasted_iota(jnp.int32, (H_CHUNK, H_CHUNK), 1)
    tril_h = jnp.where(r64 >= c64, 1.0, 0.0).astype(BF16)
    bdiff = r64 // H_SUB - c64 // H_SUB
    mask_intra = (bdiff == 0) & (r64 >= c64)
    n_sub = H_CHUNK // H_SUB

    def h_chunk(c, factored):
        rows = slice(c * H_CHUNK, (c + 1) * H_CHUNK)
        g = _cumsum_rows(tril_h, proj_s[ib, rows, OFF_HF:OFF_HI], precise)
        qh = proj_s[ib, rows, OFF_HQ:OFF_HF]
        khc = hk_s[ib, rows, :]
        if not factored:
            gx_s[...] = g
        bnd = [jnp.zeros((1, HW), F32)] + [g[(j + 1) * H_SUB - 1:(j + 1) * H_SUB, :] for j in range(n_sub)]
        g_start = jnp.concatenate([_bcast_rows(bnd[j], H_SUB) for j in range(n_sub)], axis=0)
        g_end = jnp.concatenate([_bcast_rows(bnd[j + 1], H_SUB) for j in range(n_sub)], axis=0)
        g_last = bnd[n_sub]
        qt = qh * jnp.exp(g - g_start)
        khat = khc * jnp.exp(g_end - g)
        kbar = khc * jnp.exp(g_start - g)
        qg = qh * jnp.exp(g)
        kend = khc * jnp.exp(g_last - g)
        dsub = [jnp.exp(bnd[j + 1] - bnd[j]) for j in range(n_sub - 1)]
        ones = jnp.ones((H_SUB, HW), F32)
        qlev = [qt]
        for d in range(1, n_sub - 1):
            fac = jnp.concatenate([ones] * d + [_bcast_rows(dsub[j - d], H_SUB) for j in range(d, n_sub)], axis=0)
            qlev.append(qlev[-1] * fac)
        for h in range(NH):
            sl = slice(h * HP, (h + 1) * HP)
            if factored:
                lhs = mm(jnp.concatenate([ql[:, sl] for ql in qlev], axis=0))
                inter = dot_nt(lhs, mm(khat[:, sl]))
                intra = dot_nt(mm(qt[:, sl]), mm(kbar[:, sl]))
                att = jnp.where(mask_intra, intra, 0.0)
                for d in range(n_sub - 1):
                    att = att + jnp.where(bdiff == d + 1, inter[d * H_CHUNK:(d + 1) * H_CHUNK, :], 0.0)
            else:
                def cols(j, att_acc, sl=sl, gh=g[:, sl], qhh=qh[:, sl]):
                    r0 = pl.multiple_of(j * 8, 8)
                    g8 = gx_s[pl.ds(r0, 8), sl]
                    k8 = hk_s[ib, pl.ds(c * H_CHUNK + r0, 8), sl]
                    for r in range(8):
                        wgt = jnp.exp(jnp.minimum(gh - g8[r:r + 1, :], 0.0))
                        pcol = jnp.sum(qhh * wgt * k8[r:r + 1, :], axis=1, keepdims=True)
                        att_acc = att_acc + jnp.where((c64 == r0 + r) & (r64 >= r0 + r), pcol, 0.0)
                    return att_acc
                att = lax.fori_loop(0, H_CHUNK // 8, cols, jnp.zeros((H_CHUNK, H_CHUNK), F32))
            iv = mm(proj_s[ib, rows, OFF_HI + h * HP:OFF_HI + (h + 1) * HP])
            st = st_s[h]
            o = dot(mm(att), iv) + dot_nt(mm(qg[:, sl]), mm(st))
            st_s[h] = keep(st * jnp.exp(g_last[:, sl]) + dot_tn(iv, mm(kend[:, sl])), st)
            msq = jnp.sum(o * o, axis=1, keepdims=True) * (1.0 / DH)
            on = o * lax.rsqrt(msq + EPS) * hng_ref[:, sl]
            hg = proj_s[ib, rows, OFF_HG + h * HP:OFF_HG + (h + 1) * HP]
            y_s[ib, rows, OFF_YH + h * HP:OFF_YH + (h + 1) * HP] = mm(on * (hg * _sigmoid(hg)))

    st_old_s[...] = st_s[...]
    for c in range(lb // H_CHUNK):
        h_chunk(c, True)

    xres = xres_s[ib] if pipelined else x

    def out_proj():
        out = dot(y_s[ib], wout_ref[...])
        xo_ref[...] = xres + out[pad:lb, :]

    out_proj()
    lg_sub = proj_s[ib, :, OFF_HF:OFF_HI].reshape(lb // H_SUB, H_SUB, HW)
    factor_ok = jnp.min(jnp.sum(lg_sub, axis=1)) > H_SAFE_LOG_DECAY

    @pl.when(jnp.logical_not(factor_ok))
    def _redo_direct():
        st_s[...] = st_old_s[...]
        for c in range(lb // H_CHUNK):
            h_chunk(c, False)
        out_proj()

    @pl.when(t == n_steps - 1)
    def _final():
        cf_ref[...] = c_s[...]
        mf_ref[...] = m_s[...]
        convf_ref[...] = conv_tail
        poolf_ref[...] = pool_tail[16:32, :]
        sf_ref[...] = st_s[...]


def _mixer_call(xall, states, shared_init, w, layer, *, row_off, seq_stride, nb, seq, tv, pos0, precise, name,
                stacked_states=False):
    lb = max(tv, M_CHUNK)
    n_t = seq // tv
    pipelined = False
    n_steps = n_t + 1 if pipelined else n_t
    n_buf = 2 if pipelined else 1
    assert row_off % tv == 0 and seq_stride % tv == 0 and seq % tv == 0
    blk0, blk_stride = row_off // tv, seq_stride // tv
    c0, m0, conv0, pool0, s0 = states
    lead_blk = (None,) if stacked_states else ()
    lead_idx = (layer,) if stacked_states else ()

    def x_map(b, t):
        return (blk0 + b * blk_stride + jnp.minimum(t, n_t - 1), 0)

    def xo_map(b, t):
        return (blk0 + b * blk_stride + (jnp.maximum(t - 1, 0) if pipelined else t), 0)

    def st_map(b, t):
        return lead_idx + (0 if shared_init else b, 0, 0, 0)

    def st_map3(b, t):
        return lead_idx + (0 if shared_init else b, 0, 0)

    def layer_spec(shape):
        return pl.BlockSpec((None,) + shape, lambda b, t: (layer,) + (0,) * len(shape),
                            pipeline_mode=pl.Buffered(1))

    in_specs = [
        pl.BlockSpec((tv, D_MODEL), x_map),
        pl.BlockSpec(lead_blk + (None, NH, HP, HP), st_map),
        pl.BlockSpec(lead_blk + (None, 8, HP), st_map3),
        pl.BlockSpec(lead_blk + (None, CONV_HDR, 2 * HW), st_map3),
        pl.BlockSpec(lead_blk + (None, 16, P_W), st_map3),
        pl.BlockSpec(lead_blk + (None, NH, HP, HP), st_map),
        layer_spec((1, D_MODEL)),
        layer_spec((D_MODEL, D_INP)),
        layer_spec((1, HP)),
        layer_spec((4, 2 * HW)),
        layer_spec((1, HW)),
        layer_spec((P_W, P_W)),
        layer_spec((1, P_W)),
        pl.BlockSpec((DEPTH, HW), lambda b, t: (0, 0), pipeline_mode=pl.Buffered(1)),
        layer_spec((1, HW)),
        layer_spec((D_MIXP, D_MODEL)),
    ]
    out_specs = [
        pl.BlockSpec((tv, D_MODEL), xo_map),
        pl.BlockSpec((None, NH, HP, HP), lambda b, t: (b, 0, 0, 0)),
        pl.BlockSpec((None, 8, HP), lambda b, t: (b, 0, 0)),
        pl.BlockSpec((None, CONV_HDR, 2 * HW), lambda b, t: (b, 0, 0)),
        pl.BlockSpec((None, 16, P_W), lambda b, t: (b, 0, 0)),
        pl.BlockSpec((None, NH, HP, HP), lambda b, t: (b, 0, 0, 0)),
    ]
    out_shape = [
        jax.ShapeDtypeStruct(xall.shape, F32),
        jax.ShapeDtypeStruct((nb, NH, HP, HP), F32),
        jax.ShapeDtypeStruct((nb, 8, HP), F32),
        jax.ShapeDtypeStruct((nb, CONV_HDR, 2 * HW), F32),
        jax.ShapeDtypeStruct((nb, 16, P_W), F32),
        jax.ShapeDtypeStruct((nb, NH, HP, HP), F32),
    ]
    scratch = [
        pltpu.VMEM((n_buf, lb, D_INP), F32),
        pltpu.VMEM((CONV_HDR + lb, 2 * HW), F32),
        pltpu.VMEM((POOL_HDR + lb, P_W), F32),
        pltpu.VMEM((POOL_HDR + lb, P_W), F32),
        pltpu.VMEM((POOL_HDR + lb, P_W), F32),
        pltpu.VMEM((POOL_HDR + lb, P_W), F32),
        pltpu.VMEM((n_buf, lb, HW), F32),
        pltpu.VMEM((n_buf, lb, HP), F32),
        pltpu.VMEM((n_buf, lb, D_MIXP), F32 if precise else BF16),
        pltpu.VMEM((n_buf, tv, D_MODEL) if pipelined else (1, 8, HP), F32),
        pltpu.VMEM((H_CHUNK, HW), F32),
        pltpu.VMEM((NH, HP, HP), F32),
        pltpu.VMEM((NH, HP, HP), F32),
        pltpu.VMEM((8, HP), F32),
        pltpu.VMEM((NH, HP, HP), F32),
    ]
    sfx = "_f32" if precise else ""
    kern = functools.partial(_mixer_kernel, layer=layer, tv=tv, lb=lb, pos0=pos0, n_t=n_t, precise=precise,
                             pipelined=pipelined)
    outs = pl.pallas_call(
        kern,
        grid=(nb, n_steps),
        in_specs=in_specs,
        out_specs=out_specs,
        out_shape=out_shape,
        scratch_shapes=scratch,
        input_output_aliases={0: 0},
        compiler_params=pltpu.CompilerParams(dimension_semantics=("arbitrary", "arbitrary"),
                                             vmem_limit_bytes=VMEM_LIMIT),
        name=name,
    )(xall, c0, m0, conv0, pool0, s0,
      w["g1"], w["w_in" + sfx], w["gbias"], w["conv_w"], w["mnorm"],
      w["pool_w" + sfx], w["pool_scale"], w["lb_logits"], w["hnorm"], w["w_out" + sfx])
    return outs[0], tuple(outs[1:])


def _ffn_kernel(x_ref, g_ref, w1_ref, w3_ref, w2_ref, o_ref, hn_s, *, precise):
    f = pl.program_id(1)
    prec = X3 if precise else None

    @pl.when(f == 0)
    def _start():
        x = x_ref[...]
        ms = jnp.mean(x * x, axis=-1, keepdims=True)
        hn_s[...] = (x * lax.rsqrt(ms + EPS) * g_ref[...]).astype(hn_s.dtype)
        o_ref[...] = x

    hn = hn_s[...]
    h1 = _dot(hn, w1_ref[...], prec)
    h3 = _dot(hn, w3_ref[...], prec)
    a = (h1 * _sigmoid(h1) * h3).astype(hn_s.dtype)
    o_ref[...] += _dot(a, w2_ref[...], prec)


def _ffn_rows_f32_call(xall, blocks, g, w1, w3, w2, slot, *, tm, tf, name):
    n_f = D_FF // tf

    def x_map(i, f):
        idx = blocks[-1]
        for j in range(len(blocks) - 2, -1, -1):
            idx = jnp.where(i == j, blocks[j], idx)
        return (idx, 0)

    return pl.pallas_call(
        functools.partial(_ffn_kernel, precise=True),
        grid=(len(blocks), n_f),
        in_specs=[
            pl.BlockSpec((tm, D_MODEL), x_map),
            pl.BlockSpec((1, D_MODEL), lambda i, f: (0, 0)),
            pl.BlockSpec((None, D_MODEL, tf), lambda i, f: (slot, 0, f)),
            pl.BlockSpec((None, D_MODEL, tf), lambda i, f: (slot, 0, f)),
            pl.BlockSpec((None, tf, D_MODEL), lambda i, f: (slot, f, 0)),
        ],
        out_specs=pl.BlockSpec((tm, D_MODEL), lambda i, f: (i, 0)),
        out_shape=jax.ShapeDtypeStruct((len(blocks) * tm, D_MODEL), F32),
        scratch_shapes=[pltpu.VMEM((tm, D_MODEL), F32)],
        compiler_params=pltpu.CompilerParams(dimension_semantics=("arbitrary", "arbitrary"),
                                             vmem_limit_bytes=VMEM_LIMIT),
        name=name,
    )(xall, g, w1, w3, w2)


def _ffn_call(xall, g, w1, w3, w2, slot, *, tm, tf, name):
    n = xall.shape[0]
    n_f = D_FF // tf
    return pl.pallas_call(
        functools.partial(_ffn_kernel, precise=False),
        grid=(n // tm, n_f),
        in_specs=[
            pl.BlockSpec((tm, D_MODEL), lambda i, f: (i, 0)),
            pl.BlockSpec((1, D_MODEL), lambda i, f: (0, 0)),
            pl.BlockSpec((None, D_MODEL, tf), lambda i, f: (slot, 0, f)),
            pl.BlockSpec((None, D_MODEL, tf), lambda i, f: (slot, 0, f)),
            pl.BlockSpec((None, tf, D_MODEL), lambda i, f: (slot, f, 0)),
        ],
        out_specs=pl.BlockSpec((tm, D_MODEL), lambda i, f: (i, 0)),
        out_shape=jax.ShapeDtypeStruct(xall.shape, F32),
        scratch_shapes=[pltpu.VMEM((tm, D_MODEL), BF16)],
        input_output_aliases={0: 0},
        compiler_params=pltpu.CompilerParams(dimension_semantics=("arbitrary", "arbitrary"),
                                             vmem_limit_bytes=VMEM_LIMIT),
        name=name,
    )(xall, g, w1, w3, w2)


GATHER_TILE = 256
NO_SLOT = -1e9


def _route_kernel(x_ref, g_ref, wr_ref, hn_ref, rank_t_ref, rank_c_ref, comb_ref, cnt_ref, *, tb):
    x = x_ref[...]
    ms = jnp.mean(x * x, axis=-1, keepdims=True)
    hn = x * lax.rsqrt(ms + EPS) * g_ref[...]
    hn_ref[...] = hn.astype(BF16)
    logits = jnp.dot(hn, wr_ref[...], preferred_element_type=F32, precision=lax.Precision.HIGHEST)
    lane = lax.broadcasted_iota(jnp.int32, (tb, HP), 1).astype(F32)
    lg = jnp.where(lane < N_EXPERTS, logits, NEG)
    v1 = jnp.max(lg, axis=1, keepdims=True)
    i1 = jnp.min(jnp.where(lg == v1, lane, float(HP)), axis=1, keepdims=True)
    mask1 = lane == i1
    lg2 = jnp.where(mask1, NEG, lg)
    v2 = jnp.max(lg2, axis=1, keepdims=True)
    i2 = jnp.min(jnp.where(lg2 == v2, lane, float(HP)), axis=1, keepdims=True)
    mask2 = lane == i2
    ex = jnp.exp(v2 - v1)
    ga = 1.0 / (1.0 + ex)
    comb_ref[...] = jnp.where(mask1, ga, 0.0) + jnp.where(mask2, ex * ga, 0.0)
    sel = mask1 | mask2
    rr = lax.broadcasted_iota(jnp.int32, (tb, tb), 0)
    cc = lax.broadcasted_iota(jnp.int32, (tb, tb), 1)
    tril_strict = jnp.where(rr > cc, 1.0, 0.0).astype(BF16)
    selb = jnp.where(sel, 1.0, 0.0)
    rank = jnp.where(sel, _dot(tril_strict, selb.astype(BF16)), NO_SLOT)
    rank_c_ref[...] = rank
    rank_t_ref[...] = rank.T[0:N_EXPERTS, :]
    cnt_ref[...] = jnp.broadcast_to(jnp.sum(selb, axis=0, keepdims=True), (8, HP))


def _route_call(xall, g, wr, *, tb, name):
    n = xall.shape[0]
    nb = n // tb
    return pl.pallas_call(
        functools.partial(_route_kernel, tb=tb),
        grid=(nb,),
        in_specs=[
            pl.BlockSpec((tb, D_MODEL), lambda i: (i, 0)),
            pl.BlockSpec((1, D_MODEL), lambda i: (0, 0)),
            pl.BlockSpec((D_MODEL, HP), lambda i: (0, 0)),
        ],
        out_specs=[
            pl.BlockSpec((tb, D_MODEL), lambda i: (i, 0)),
            pl.BlockSpec((None, N_EXPERTS, tb), lambda i: (i, 0, 0)),
            pl.BlockSpec((tb, HP), lambda i: (i, 0)),
            pl.BlockSpec((tb, HP), lambda i: (i, 0)),
            pl.BlockSpec((None, 8, HP), lambda i: (i, 0, 0)),
        ],
        out_shape=[
            jax.ShapeDtypeStruct((n, D_MODEL), BF16),
            jax.ShapeDtypeStruct((nb, N_EXPERTS, tb), F32),
            jax.ShapeDtypeStruct((n, HP), F32),
            jax.ShapeDtypeStruct((n, HP), F32),
            jax.ShapeDtypeStruct((nb, 8, HP), F32),
        ],
        compiler_params=pltpu.CompilerParams(dimension_semantics=("arbitrary",), vmem_limit_bytes=VMEM_LIMIT),
        name=name,
    )(xall, g, wr)


def _gather_kernel(tile_ref, blk_ref, exp_ref, off_ref, first_ref, act_ref, rank_t_ref, hn_ref, init_ref, xs_ref,
                   *, tb):
    del tile_ref, blk_ref, init_ref
    p = pl.program_id(0)

    @pl.when(act_ref[p] == 1)
    def _():
        rrow = rank_t_ref[pl.ds(exp_ref[p], 1), :]
        slot = (lax.broadcasted_iota(jnp.int32, (GATHER_TILE, tb), 0) - off_ref[p]).astype(F32)
        onehot = jnp.where(rrow == slot, 1.0, 0.0).astype(BF16)
        val = _dot(onehot, hn_ref[...]).astype(BF16)

        @pl.when(first_ref[p] == 1)
        def _set():
            xs_ref[...] = val

        @pl.when(first_ref[p] == 0)
        def _add():
            xs_ref[...] += val


def _gather_call(pairs, rank_t, hn, init, *, tb, name):
    n_rows = init.shape[0]
    n_pairs = pairs[0].shape[0]
    grid_spec = pltpu.PrefetchScalarGridSpec(
        num_scalar_prefetch=6,
        grid=(n_pairs,),
        in_specs=[
            pl.BlockSpec((None, N_EXPERTS, tb), lambda p, tile, blk, *_: (blk[p], 0, 0)),
            pl.BlockSpec((tb, D_MODEL), lambda p, tile, blk, *_: (blk[p], 0)),
            pl.BlockSpec(memory_space=pl.ANY),
        ],
        out_specs=pl.BlockSpec((GATHER_TILE, D_MODEL), lambda p, tile, *_: (tile[p], 0)),
    )
    return pl.pallas_call(
        functools.partial(_gather_kernel, tb=tb),
        grid_spec=grid_spec,
        out_shape=jax.ShapeDtypeStruct((n_rows, D_MODEL), BF16),
        input_output_aliases={8: 0},
        compiler_params=pltpu.CompilerParams(dimension_semantics=("arbitrary",), vmem_limit_bytes=VMEM_LIMIT),
        name=name,
    )(*pairs, rank_t, hn, init)


def _experts_kernel(exp_ref, act_ref, xs_ref, w1_ref, w3_ref, w2_ref, ys_ref, acc_s, *, n_f):
    del exp_ref
    i = pl.program_id(0)
    f = pl.program_id(1)

    @pl.when(f == 0)
    def _zero():
        acc_s[...] = jnp.zeros(acc_s.shape, F32)

    @pl.when(act_ref[i] == 1)
    def _():
        xe = xs_ref[...]
        h1 = _dot(xe, w1_ref[...].astype(BF16))
        h3 = _dot(xe, w3_ref[...].astype(BF16))
        a = (h1 * _sigmoid(h1) * h3).astype(BF16)
        acc_s[...] += _dot(a, w2_ref[...].astype(BF16))

    @pl.when(f == n_f - 1)
    def _out():
        ys_ref[...] = acc_s[...].astype(BF16)


def _experts_call(tile_exp, tile_act, xs, w1, w3, w2, slot, *, tm, tf, name):
    n_rows = xs.shape[0]
    n_f = D_FF // tf
    grid_spec = pltpu.PrefetchScalarGridSpec(
        num_scalar_prefetch=2,
        grid=(n_rows // tm, n_f),
        in_specs=[
            pl.BlockSpec((tm, D_MODEL), lambda i, f, ex, act: (i, 0)),
            pl.BlockSpec((None, None, D_MODEL, tf), lambda i, f, ex, act: (slot, ex[i], 0, f)),
            pl.BlockSpec((None, None, D_MODEL, tf), lambda i, f, ex, act: (slot, ex[i], 0, f)),
            pl.BlockSpec((None, None, tf, D_MODEL), lambda i, f, ex, act: (slot, ex[i], f, 0)),
        ],
        out_specs=pl.BlockSpec((tm, D_MODEL), lambda i, f, ex, act: (i, 0)),
        scratch_shapes=[pltpu.VMEM((tm, D_MODEL), F32)],
    )
    return pl.pallas_call(
        functools.partial(_experts_kernel, n_f=n_f),
        grid_spec=grid_spec,
        out_shape=jax.ShapeDtypeStruct((n_rows, D_MODEL), BF16),
        compiler_params=pltpu.CompilerParams(dimension_semantics=("arbitrary", "arbitrary"),
                                             vmem_limit_bytes=VMEM_LIMIT),
        name=name,
    )(tile_exp, tile_act, xs, w1, w3, w2)


def _combine_kernel(blk_ref, tile_ref, exp_ref, off_ref, first_ref, last_ref, act_ref,
                    x_ref, rank_c_ref, comb_ref, ys_ref, fg_ref, o_ref, *, tb, final_norm):
    del blk_ref, tile_ref
    p = pl.program_id(0)

    @pl.when(first_ref[p] == 1)
    def _start():
        o_ref[...] = x_ref[...]

    @pl.when(act_ref[p] == 1)
    def _():
        pick = lax.broadcasted_iota(jnp.int32, (tb, HP), 1) == exp_ref[p]
        rcol = jnp.sum(jnp.where(pick, rank_c_ref[...], 0.0), axis=1, keepdims=True)
        gcol = jnp.sum(jnp.where(pick, comb_ref[...], 0.0), axis=1, keepdims=True)
        slot = (lax.broadcasted_iota(jnp.int32, (tb, GATHER_TILE), 1) - off_ref[p]).astype(F32)
        onehot = jnp.where(rcol == slot, 1.0, 0.0).astype(BF16)
        o_ref[...] += gcol * _dot(onehot, ys_ref[...])

    if final_norm:
        @pl.when(last_ref[p] == 1)
        def _norm():
            y = o_ref[...]
            ms = jnp.mean(y * y, axis=-1, keepdims=True)
            o_ref[...] = y * lax.rsqrt(ms + EPS) * fg_ref[...]


def _combine_call(pairs, xall, rank_c, comb, ys, fg, *, tb, final_norm, name):
    n_pairs = pairs[0].shape[0]
    grid_spec = pltpu.PrefetchScalarGridSpec(
        num_scalar_prefetch=7,
        grid=(n_pairs,),
        in_specs=[
            pl.BlockSpec((tb, D_MODEL), lambda p, blk, *_: (blk[p], 0)),
            pl.BlockSpec((tb, HP), lambda p, blk, *_: (blk[p], 0)),
            pl.BlockSpec((tb, HP), lambda p, blk, *_: (blk[p], 0)),
            pl.BlockSpec((GATHER_TILE, D_MODEL), lambda p, blk, tile, *_: (tile[p], 0)),
            pl.BlockSpec((1, D_MODEL), lambda p, *_: (0, 0)),
        ],
        out_specs=pl.BlockSpec((tb, D_MODEL), lambda p, blk, *_: (blk[p], 0)),
    )
    return pl.pallas_call(
        functools.partial(_combine_kernel, tb=tb, final_norm=final_norm),
        grid_spec=grid_spec,
        out_shape=jax.ShapeDtypeStruct(xall.shape, F32),
        input_output_aliases={7: 0},
        compiler_params=pltpu.CompilerParams(dimension_semantics=("arbitrary",), vmem_limit_bytes=VMEM_LIMIT),
        name=name,
    )(*pairs, xall, rank_c, comb, ys, fg)


def _pair_lists(cnt, *, tm, n_rows):
    nb = cnt.shape[0]
    i32 = jnp.int32
    tot = jnp.sum(cnt, axis=0)
    grp_rows = (tot + tm - 1) // tm * tm
    grp_end = jnp.cumsum(grp_rows)
    grp_start = grp_end - grp_rows
    base = grp_start[None, :] + jnp.cumsum(cnt, axis=0) - cnt
    lo = base // GATHER_TILE
    hi = (base + cnt - 1) // GATHER_TILE
    npair = jnp.where(cnt > 0, hi - lo + 1, 0)
    n_pairs = nb * N_EXPERTS + n_rows // GATHER_TILE
    n_keys = nb * N_EXPERTS

    def expand(expert_major):
        def flat(a):
            return (a.T if expert_major else a).reshape(-1)
        np_k = flat(npair)
        cum = jnp.cumsum(np_k)
        total = cum[-1]
        p = jnp.arange(n_pairs, dtype=i32)
        pc = jnp.minimum(p, total - 1)
        k = jnp.sum((cum[None, :] <= pc[:, None]).astype(i32), axis=1)
        table = jnp.stack([flat(lo), flat(base), cum - np_k], axis=1)
        row = jnp.take(table, jnp.minimum(k, n_keys - 1), axis=0)
        tile = row[:, 0] + pc - row[:, 2]
        blk, exp = (k % nb, k // nb) if expert_major else (k // N_EXPERTS, k % N_EXPERTS)
        act = (p < total).astype(i32)
        return tile.astype(i32), blk.astype(i32), exp.astype(i32), (row[:, 1] - tile * GATHER_TILE).astype(i32), act

    g_tile, g_blk, g_exp, g_off, g_act = expand(True)
    g_first = jnp.concatenate([jnp.ones((1,), i32), (g_tile[1:] != g_tile[:-1]).astype(i32)])
    c_tile, c_blk, c_exp, c_off, c_act = expand(False)
    c_first = jnp.concatenate([jnp.ones((1,), i32), (c_blk[1:] != c_blk[:-1]).astype(i32)])
    c_last = jnp.concatenate([(c_blk[1:] != c_blk[:-1]).astype(i32), jnp.ones((1,), i32)])
    c_last = jnp.where(jnp.arange(n_pairs) == jnp.sum(c_act) - 1, 1, c_last) * c_act
    n_tiles = n_rows // tm
    t0 = jnp.arange(n_tiles, dtype=i32) * tm
    t_act = (t0 < grp_end[-1]).astype(i32)
    t_exp = jnp.sum((grp_end[None, :] <= jnp.minimum(t0, grp_end[-1] - 1)[:, None]).astype(i32), axis=1)
    return ((g_tile, g_blk, g_exp, g_off, g_first, g_act),
            (c_blk, c_tile, c_exp, c_off, c_first, c_last, c_act), (t_exp, t_act))


def _moe_layer(xall, g, wr, w1, w3, w2, slot, fg, xs_init, *, tb, tm, tf, final_norm, tag):
    n = xall.shape[0]
    n_rows = 2 * n + N_EXPERTS * tm
    if xs_init is None:
        xs_init = jnp.zeros((n_rows, D_MODEL), BF16)
    hn, rank_t, rank_c, comb, cnt = _route_call(xall, g, wr, tb=tb, name=f"moe_route_{tag}")
    cnt = cnt[:, 0, 0:N_EXPERTS].astype(jnp.int32)
    g_pairs, c_pairs, (t_exp, t_act) = _pair_lists(cnt, tm=tm, n_rows=n_rows)
    xs = _gather_call(g_pairs, rank_t, hn, xs_init, tb=tb, name=f"moe_gather_{tag}")
    ys = _experts_call(t_exp, t_act, xs, w1, w3, w2, slot, tm=tm, tf=tf, name=f"moe_experts_{tag}")
    out = _combine_call(c_pairs, xall, rank_c, comb, ys, fg, tb=tb, final_norm=final_norm,
                        name=f"moe_combine_{tag}")
    return out, xs


def _pad_heads(a, axis):
    axis = axis % a.ndim
    shp = a.shape
    a = a.reshape(shp[:axis] + (NH, DH) + shp[axis + 1:])
    padw = [(0, 0)] * a.ndim
    padw[axis + 1] = (0, HP - DH)
    a = jnp.pad(a, padw)
    return a.reshape(shp[:axis] + (HW,) + shp[axis + 1:])


def _unpad_heads(a, axis):
    axis = axis % a.ndim
    shp = a.shape
    a = a.reshape(shp[:axis] + (NH, HP) + shp[axis + 1:])
    a = lax.slice_in_dim(a, 0, DH, axis=axis + 1)
    return a.reshape(shp[:axis] + (NH * DH,) + shp[axis + 1:])


_M_W = NH * DH
SRC_MQ, SRC_MK, SRC_MV, SRC_MO = 0, _M_W, 2 * _M_W, 3 * _M_W
SRC_MI = 4 * _M_W
SRC_MF = SRC_MI + NH
SRC_PU = SRC_MF + NH
SRC_HQ = SRC_PU + P_W
SRC_HF = SRC_HQ + HW
SRC_HI = SRC_HF + HW
SRC_HG = SRC_HI + _M_W
D_IN = SRC_HG + _M_W


def _w_in_relayout_kernel(w_ref, o32_ref, o16_ref):
    rows = w_ref.shape[0]

    def put(dst, val):
        o32_ref[:, dst:dst + val.shape[1]] = val
        o16_ref[:, dst:dst + val.shape[1]] = val.astype(BF16)

    for src, dst in ((SRC_MQ, OFF_MQ), (SRC_MK, OFF_MK), (SRC_MV, OFF_MV), (SRC_MO, OFF_MO),
                     (SRC_HI, OFF_HI), (SRC_HG, OFF_HG)):
        for h in range(NH):
            put(dst + HP * h, w_ref[:, src + DH * h:src + DH * (h + 1)])
            put(dst + HP * h + DH, jnp.zeros((rows, HP - DH), F32))
    put(OFF_G, w_ref[:, SRC_MI:SRC_MI + NH])
    put(OFF_G + NH, w_ref[:, SRC_MF:SRC_MF + NH])
    put(OFF_G + 2 * NH, jnp.zeros((rows, HP - 2 * NH), F32))
    put(OFF_PU, w_ref[:, SRC_PU:SRC_PU + P_W])
    put(OFF_HQ, w_ref[:, SRC_HQ:SRC_HQ + HW])
    put(OFF_HF, w_ref[:, SRC_HF:SRC_HF + HW])


def _w_in_relayout(w_in, *, tr=256):
    d, k, n = w_in.shape
    assert n == D_IN and k % tr == 0
    return pl.pallas_call(
        _w_in_relayout_kernel,
        grid=(d, k // tr),
        in_specs=[pl.BlockSpec((None, tr, n), lambda l, i: (l, i, 0))],
        out_specs=[pl.BlockSpec((None, tr, D_INP), lambda l, i: (l, i, 0)),
                   pl.BlockSpec((None, tr, D_INP), lambda l, i: (l, i, 0))],
        out_shape=[jax.ShapeDtypeStruct((d, k, D_INP), F32), jax.ShapeDtypeStruct((d, k, D_INP), BF16)],
        compiler_params=pltpu.CompilerParams(dimension_semantics=("arbitrary", "arbitrary"),
                                             vmem_limit_bytes=VMEM_LIMIT),
        name="w_in_relayout",
    )(w_in)


def _prep_weights(norm1_g, w_in, b_igate, b_fgate, conv_w, mlstm_norm_g, pool_w, pool_scale, lb_logits,
                  hgrn_norm_g, w_out):
    m_w = _M_W
    w_in_p, w_in_b = _w_in_relayout(w_in)
    w_out_p = jnp.concatenate([_pad_heads(w_out[:, 0:m_w], 1), w_out[:, m_w:m_w + P_W],
                               _pad_heads(w_out[:, m_w + P_W:], 1)], axis=1)
    gbias = jnp.pad(jnp.concatenate([b_igate, b_fgate], axis=-1), ((0, 0), (0, HP - 2 * NH)))[:, None, :]
    conv_p = jnp.concatenate([_pad_heads(conv_w[..., 0:m_w], -1), _pad_heads(conv_w[..., m_w:], -1)], axis=-1)
    eye = jnp.eye(len(POOL_WINDOWS), dtype=F32)
    pool_bd = jnp.einsum('lgce,gh->lgche', pool_w, eye).reshape(DEPTH, P_W, P_W)
    return {
        "g1": norm1_g[:, None, :], "w_in": w_in_b, "w_in_f32": w_in_p, "gbias": gbias,
        "conv_w": conv_p, "mnorm": _pad_heads(mlstm_norm_g, -1)[:, None, :],
        "pool_w": pool_bd.astype(BF16), "pool_w_f32": pool_bd,
        "pool_scale": pool_scale[:, None, :], "lb_logits": lb_logits,
        "hnorm": _pad_heads(hgrn_norm_g, -1)[:, None, :], "w_out": w_out_p.astype(BF16), "w_out_f32": w_out_p,
    }


def _states_to_kernel(C, n, m, conv, pool, S):
    nb = C.shape[0]
    caug = jnp.concatenate([C, n[..., None]], axis=-1)
    caug = jnp.pad(caug, ((0, 0), (0, 0), (0, HP - DH), (0, HP - DH - 1)))
    mk = jnp.pad(jnp.broadcast_to(m[:, :, None], (nb, NH, HP)), ((0, 0), (0, 8 - NH), (0, 0)))
    m_w = NH * DH
    convk = jnp.concatenate([_pad_heads(conv[..., 0:m_w], -1), _pad_heads(conv[..., m_w:], -1)], axis=-1)
    convk = jnp.pad(convk, ((0, 0), (CONV_HDR - 3, 0), (0, 0)))
    poolk = jnp.pad(pool, ((0, 0), (1, 0), (0, 0)))
    sk = jnp.pad(jnp.swapaxes(S, -1, -2), ((0, 0), (0, 0), (0, HP - DH), (0, 0)))
    return caug, mk, convk, poolk, sk


def _states_from_kernel(st):
    caug, mk, convk, poolk, sk = st
    C = caug[:, :, 0:DH, 0:DH]
    n = caug[:, :, 0:DH, DH]
    m = mk[:, 0:NH, 0]
    conv = convk[:, CONV_HDR - 3:, :]
    conv = jnp.concatenate([_unpad_heads(conv[..., 0:HW], -1), _unpad_heads(conv[..., HW:], -1)], axis=-1)
    pool = poolk[:, 1:, :]
    S = jnp.swapaxes(sk[:, :, 0:DH, :], -1, -2)
    return C, n, m, conv, pool, S


def _pick_tile(n, candidates):
    for c in candidates:
        if n % c == 0:
            return c
    raise ValueError(f"no row tile for {n}")


def kernel(x_prompt, x_sample, state_mlstm_C, state_mlstm_n, state_mlstm_m, state_mlstm_conv, state_pool,
           state_hgrn, meta_tokens, norm1_g, norm2_g, final_g, w_in, b_igate, b_fgate, conv_w, mlstm_norm_g,
           pool_w, pool_scale, lb_logits, hgrn_norm_g, w_out, ffn_w1, ffn_w3, ffn_w2, router_w, moe_w1,
           moe_w3, moe_w2):
    B, T, _ = x_prompt.shape
    SB, ST, _ = x_sample.shape
    w = _prep_weights(norm1_g, w_in, b_igate, b_fgate, conv_w, mlstm_norm_g, pool_w, pool_scale, lb_logits,
                      hgrn_norm_g, w_out)
    ffn_w1b, ffn_w3b, ffn_w2b = ffn_w1.astype(BF16), ffn_w3.astype(BF16), ffn_w2.astype(BF16)
    router_p = jnp.pad(router_w, ((0, 0), (0, 0), (0, HP - N_EXPERTS)))

    n_main = B * T
    off_s = n_main
    off_m = off_s + SB * ST
    assert SB * ST == TAIL and n_main % TAIL == 0
    n_tok = off_m + META_TOKENS
    tile = 1280 if n_main >= 16384 else 256
    expert_tile = 1024 if n_main >= 16384 else 256
    n_pad = -(-n_tok // tile) * tile
    xall = jnp.concatenate([x_prompt.reshape(n_main, D_MODEL), x_sample.reshape(SB * ST, D_MODEL),
                            meta_tokens.astype(F32), jnp.zeros((n_pad - n_tok, D_MODEL), F32)], axis=0)
    tv_main = _pick_tile(T, (256, 128))

    zero_states = (jnp.zeros((1, NH, HP, HP), F32), jnp.zeros((1, 8, HP), F32),
                   jnp.zeros((1, CONV_HDR, 2 * HW), F32), jnp.zeros((1, 16, P_W), F32),
                   jnp.zeros((1, NH, HP, HP), F32))
    p_states, s_states = [], []
    xs_buf = None
    st_in = jax.vmap(_states_to_kernel)(state_mlstm_C, state_mlstm_n, state_mlstm_m, state_mlstm_conv,
                                        state_pool, state_hgrn)
    tail_blocks = tuple((b * T + T - TAIL) // TAIL for b in range(B)) + (off_s // TAIL,)
    for l in range(DEPTH):
        precise = l < PRECISE_LAYERS
        xall, st_meta = _mixer_call(xall, zero_states, True, w, l, row_off=off_m, seq_stride=META_TOKENS, nb=1,
                                    seq=META_TOKENS, tv=META_TOKENS, pos0=0, precise=False,
                                    name=f"mixer_meta_{l}")
        if precise:
            xall, st_p = _mixer_call(xall, st_meta, True, w, l, row_off=0, seq_stride=T, nb=B, seq=T - TAIL,
                                     tv=tv_main, pos0=META_TOKENS, precise=False, name=f"mixer_prompt_{l}")
            xall, st_p = _mixer_call(xall, st_p, False, w, l, row_off=T - TAIL, seq_stride=T, nb=B, seq=TAIL,
                                     tv=TAIL, pos0=META_TOKENS + T - TAIL, precise=True,
                                     name=f"mixer_prompt_tail_{l}")
        else:
            xall, st_p = _mixer_call(xall, st_meta, True, w, l, row_off=0, seq_stride=T, nb=B, seq=T,
                                     tv=tv_main, pos0=META_TOKENS, precise=False, name=f"mixer_prompt_{l}")
        xall, st_s = _mixer_call(xall, st_in, False, w, l, row_off=off_s, seq_stride=ST, nb=SB, seq=ST, tv=ST,
                                 pos0=META_TOKENS + PAST_LEN, precise=True, name=f"mixer_sample_{l}",
                                 stacked_states=True)
        p_states.append(st_p)
        s_states.append(st_s)
        i = l // 2
        if l % 2 == 0:
            f32_blocks = tail_blocks if precise else tail_blocks[-1:]
            rows = _ffn_rows_f32_call(xall, f32_blocks, norm2_g[l][None, :], ffn_w1, ffn_w3, ffn_w2, i,
                                      tm=TAIL, tf=512, name=f"ffn_rows_f32_{l}")
            xall = _ffn_call(xall, norm2_g[l][None, :], ffn_w1b, ffn_w3b, ffn_w2b, i,
                             tm=tile, tf=512, name=f"ffn_{l}")
            for j, blk in enumerate(f32_blocks):
                xall = lax.dynamic_update_slice(xall, rows[j * TAIL:(j + 1) * TAIL], (blk * TAIL, 0))
        else:
            xall, xs_buf = _moe_layer(xall, norm2_g[l][None, :], router_p[i], moe_w1, moe_w3, moe_w2, i,
                                      final_g[None, :], xs_buf, tb=tile, tm=expert_tile, tf=512,
                                      final_norm=(l == DEPTH - 1), tag=str(l))
    y_prompt = xall[0:n_main].reshape(B, T, D_MODEL)
    y_sample = xall[off_s:off_m].reshape(SB, ST, D_MODEL)
    p_out = jax.vmap(_states_from_kernel)(tuple(jnp.stack([s[j] for s in p_states], axis=0) for j in range(5)))
    s_out = jax.vmap(_states_from_kernel)(tuple(jnp.stack([s[j] for s in s_states], axis=0) for j in range(5)))
    return (y_prompt, y_sample) + tuple(p_out) + tuple(s_out)
```

```python
import functools

import jax
import jax.numpy as jnp
from jax import lax
from jax.experimental import pallas as pl
from jax.experimental.pallas import tpu as pltpu

F32 = jnp.float32
BF16 = jnp.bfloat16

D_MODEL = 1024
DEPTH = 4
META_TOKENS = 16
PAST_LEN = 4096
EPS = 1e-6
NH = 4
DH = 96
HP = 128
HW = NH * HP
P_W = 256
P_GW = 64
POOL_WINDOWS = (2, 4, 8, 16)
D_FF = 3584
N_EXPERTS = 8
NEG = -1e30

OFF_MQ, OFF_MK, OFF_MV, OFF_MO = 0, 512, 1024, 1536
OFF_G = 2048
OFF_PU = 2176
OFF_HQ, OFF_HF, OFF_HI, OFF_HG = 2432, 2944, 3456, 3968
D_INP = 4480
OFF_YM, OFF_YP, OFF_YH = 0, 512, 768
D_MIXP = 1280

M_CHUNK = 128
H_CHUNK = 64
H_SUB = 16
H_SAFE_LOG_DECAY = -60.0
CONV_HDR = 8
POOL_HDR = 32

TAIL = 256
PRECISE_LAYERS = 2

VMEM_LIMIT = 60 * 1024 * 1024


def _sigmoid(x):
    return 1.0 / (1.0 + jnp.exp(-x))


X3 = "bf16x3"


def _split_bf16(a):
    hi = a.astype(BF16)
    return hi, (a - hi.astype(F32)).astype(BF16)


def _dg(a, b, dims, precision=None):
    if precision == X3:
        ah, al = _split_bf16(a)
        bh, bl = _split_bf16(b)
        return (lax.dot_general(ah, bh, dims, preferred_element_type=F32)
                + lax.dot_general(al, bh, dims, preferred_element_type=F32)
                + lax.dot_general(ah, bl, dims, preferred_element_type=F32))
    return lax.dot_general(a, b, dims, preferred_element_type=F32, precision=precision)


def _dot(a, b, precision=None):
    return _dg(a, b, (((1,), (0,)), ((), ())), precision)


def _dot_nt(a, b, precision=None):
    return _dg(a, b, (((1,), (1,)), ((), ())), precision)


def _dot_tn(a, b, precision=None):
    return _dg(a, b, (((0,), (0,)), ((), ())), precision)


def _cumsum_rows(tril_bf, x, precise=False):
    n = x.shape[1]
    hi = x.astype(BF16)
    rest = x - hi.astype(F32)
    lo = rest.astype(BF16)
    parts = [hi, lo] + ([(rest - lo.astype(F32)).astype(BF16)] if precise else [])
    both = _dot(tril_bf, jnp.concatenate(parts, axis=1))
    out = both[:, :n] + both[:, n:2 * n]
    return out + both[:, 2 * n:] if precise else out


def _bcast_rows(row, n):
    return jnp.broadcast_to(row, (n, row.shape[1]))


def _mixer_kernel(x_ref, c0_ref, m0_ref, conv0_ref, pool0_ref, s0_ref,
                  g1_ref, win_ref, gb_ref, cw_ref, mng_ref, pw_ref, ps_ref, lbl_ref, hng_ref, wout_ref,
                  xo_ref, cf_ref, mf_ref, convf_ref, poolf_ref, sf_ref,
                  proj_s, qk_s, u_s, s2_s, s4_s, s8_s, hk_s, gate_s, y_s, gx_s, st_old_s, c_s, m_s, st_s,
                  *, layer, tv, lb, pos0, n_t, precise):
    pad = lb - tv
    t = pl.program_id(1)
    prec = X3 if precise else None
    dot = functools.partial(_dot, precision=prec)
    dot_nt = functools.partial(_dot_nt, precision=prec)
    dot_tn = functools.partial(_dot_tn, precision=prec)

    def mm(a):
        return a if precise else a.astype(BF16)

    @pl.when(t == 0)
    def _init():
        c_s[...] = c0_ref[...]
        m_s[...] = m0_ref[...]
        st_s[...] = s0_ref[...]
        qk_s[...] = jnp.zeros(qk_s.shape, F32)
        qk_s[pad:pad + CONV_HDR, :] = conv0_ref[...]
        u_s[...] = jnp.zeros(u_s.shape, F32)
        u_s[pad + 16:pad + 32, :] = pool0_ref[...]
        s2_s[0:POOL_HDR, :] = jnp.zeros((POOL_HDR, P_W), F32)
        s4_s[0:POOL_HDR, :] = jnp.zeros((POOL_HDR, P_W), F32)
        s8_s[0:POOL_HDR, :] = jnp.zeros((POOL_HDR, P_W), F32)
        if pad:
            proj_s[0:pad, :] = jnp.zeros((pad, D_INP), F32)

    x = x_ref[...]
    ms = jnp.mean(x * x, axis=-1, keepdims=True)
    hn = mm(x * lax.rsqrt(ms + EPS) * g1_ref[...])
    qk_s[CONV_HDR + pad:CONV_HDR + lb, :] = dot(hn, win_ref[:, 0:OFF_MV])
    proj_s[pad:lb, OFF_MV:OFF_PU] = dot(hn, win_ref[:, OFF_MV:OFF_PU])
    u_s[POOL_HDR + pad:POOL_HDR + lb, :] = dot(hn, win_ref[:, OFF_PU:OFF_HQ])
    proj_s[pad:lb, OFF_HQ:D_INP] = dot(hn, win_ref[:, OFF_HQ:D_INP])

    row = lax.broadcasted_iota(jnp.int32, (lb, HP), 0)
    lane = lax.broadcasted_iota(jnp.int32, (lb, HP), 1)

    acc = qk_s[5:5 + lb, :] * cw_ref[0:1, :]
    for j in range(1, 4):
        acc = acc + qk_s[5 + j:5 + j + lb, :] * cw_ref[j:j + 1, :]
    qk = acc * _sigmoid(acc)
    proj_s[:, OFF_MQ:OFF_MK] = qk[:, 0:HW]
    proj_s[:, OFF_MK:OFF_MV] = qk[:, HW:2 * HW] * (DH ** -0.5)
    conv_tail = qk_s[lb:lb + CONV_HDR, :]
    qk_s[0:CONV_HDR, :] = conv_tail

    gpre = proj_s[:, OFF_G:OFF_G + HP] + gb_ref[...]
    lsig = jnp.minimum(gpre, 0.0) - jnp.log(1.0 + jnp.exp(-jnp.abs(gpre)))
    gates = jnp.where(lane < NH, gpre, jnp.where(lane < 2 * NH, lsig, 0.0))
    if pad:
        gates = jnp.where(row >= pad, gates, jnp.where(lane < NH, NEG, 0.0))
    gate_s[...] = gates

    n_ext = lb + 16
    s2_s[16:16 + n_ext, :] = u_s[16:16 + n_ext, :] + u_s[15:15 + n_ext, :]
    s4_s[16:16 + n_ext, :] = s2_s[16:16 + n_ext, :] + s2_s[14:14 + n_ext, :]
    s8_s[16:16 + n_ext, :] = s4_s[16:16 + n_ext, :] + s4_s[12:12 + n_ext, :]
    u_cur = u_s[POOL_HDR:POOL_HDR + lb, :]
    w2 = s2_s[POOL_HDR:POOL_HDR + lb, :]
    w4 = s4_s[POOL_HDR:POOL_HDR + lb, :]
    w8 = s8_s[POOL_HDR:POOL_HDR + lb, :]
    w16 = w8 + s8_s[POOL_HDR - 8:POOL_HDR - 8 + lb, :]
    lane_p = lax.broadcasted_iota(jnp.int32, (lb, P_W), 1)
    wsum = jnp.where(lane_p < P_GW, w2, jnp.where(lane_p < 2 * P_GW, w4, jnp.where(lane_p < 3 * P_GW, w8, w16)))
    if pos0 >= POOL_WINDOWS[-1] - 1:
        inv = jnp.where(lane_p < P_GW, 0.5, jnp.where(lane_p < 2 * P_GW, 0.25,
                                                       jnp.where(lane_p < 3 * P_GW, 0.125, 0.0625)))
        mean = wsum * inv
    else:
        row_p = lax.broadcasted_iota(jnp.int32, (lb, P_W), 0)
        posn = (row_p + (pos0 + 1 - pad + t * tv)).astype(F32)
        wlen = jnp.where(lane_p < P_GW, 2.0, jnp.where(lane_p < 2 * P_GW, 4.0,
                                                        jnp.where(lane_p < 3 * P_GW, 8.0, 16.0)))
        mean = wsum / jnp.maximum(jnp.minimum(wlen, posn), 1.0)
    pooled = mm(mean - u_cur)
    y_s[:, OFF_YP:OFF_YH] = mm(dot(pooled, pw_ref[...]) * ps_ref[...])
    pool_tail = u_s[lb:lb + POOL_HDR, :]
    u_s[0:POOL_HDR, :] = pool_tail

    lbl = lbl_ref[...]
    e = jnp.exp(lbl - jnp.max(lbl, axis=0, keepdims=True))
    p = e / jnp.sum(e, axis=0, keepdims=True)
    lbv = jnp.sum(p[0:layer + 1, :], axis=0, keepdims=True) - p[0:1, :]
    hq = proj_s[:, OFF_HQ:OFF_HF]
    hf = proj_s[:, OFF_HF:OFF_HI]
    fg = lbv + (1.0 - lbv) * _sigmoid(hf)
    kh = 1.0 - fg
    lg = jnp.log(fg)
    if pad:
        row_h = lax.broadcasted_iota(jnp.int32, (lb, HW), 0)
        kh = jnp.where(row_h >= pad, kh, 0.0)
        lg = jnp.where(row_h >= pad, lg, 0.0)
    proj_s[:, OFF_HQ:OFF_HF] = hq * _sigmoid(hq)
    proj_s[:, OFF_HF:OFF_HI] = lg
    hk_s[...] = kh

    rr = lax.broadcasted_iota(jnp.int32, (M_CHUNK, M_CHUNK), 0)
    cc = lax.broadcasted_iota(jnp.int32, (M_CHUNK, M_CHUNK), 1)
    causal = rr >= cc
    tril_m = jnp.where(causal, 1.0, 0.0).astype(BF16)
    lane_c = lax.broadcasted_iota(jnp.int32, (M_CHUNK, HP), 1)

    for c in range(lb // M_CHUNK):
        rows = slice(c * M_CHUNK, (c + 1) * M_CHUNK)
        gt = gate_s[rows, :]
        bcum = _cumsum_rows(tril_m, jnp.where(lane_c >= NH, gt, 0.0), precise)
        cg = gt - pltpu.roll(bcum, HP - NH, axis=1)
        cg_t = cg.T
        for h in range(NH):
            q = proj_s[rows, OFF_MQ + h * HP:OFF_MQ + (h + 1) * HP]
            k = proj_s[rows, OFF_MK + h * HP:OFF_MK + (h + 1) * HP]
            v = proj_s[rows, OFF_MV + h * HP:OFF_MV + (h + 1) * HP]
            c_row = cg_t[h:h + 1, :]
            c_col = cg[:, h:h + 1]
            b_col = bcum[:, NH + h:NH + h + 1]
            m_prev = m_s[h:h + 1, 0:1]
            mx = jnp.maximum(jnp.max(jnp.where(causal, c_row, NEG), axis=1, keepdims=True), m_prev)
            w = jnp.exp(jnp.where(causal, c_row - mx, NEG))
            w_int = jnp.exp(m_prev - mx)
            mx_last = mx[M_CHUNK - 1:M_CHUNK, :]
            s = dot_nt(mm(q), mm(k)) * w
            vaug = jnp.where(lane_c == DH, 1.0, v)
            caug = c_s[h]
            lhs = mm(jnp.concatenate([s, q * w_int], axis=1))
            rhs = mm(jnp.concatenate([vaug, caug], axis=0))
            nd = dot(lhs, rhs)
            den = nd[:, DH:DH + 1]
            rden = 1.0 / jnp.maximum(jnp.abs(den), jnp.exp(-(b_col + mx)))
            wl = jnp.exp(c_col - mx_last)
            decay = jnp.exp(m_prev - mx_last)
            c_s[h] = decay * caug + dot_tn(mm(k * wl), mm(vaug))
            m_s[h:h + 1, :] = jnp.broadcast_to(b_col[M_CHUNK - 1:M_CHUNK, :] + mx_last, (1, HP))
            mo = proj_s[rows, OFF_MO + h * HP:OFF_MO + (h + 1) * HP]
            z = jnp.where(lane_c < DH, nd * _sigmoid(mo), 0.0)
            ssq = jnp.sum(z * z, axis=1, keepdims=True) * (1.0 / DH)
            fac = rden * lax.rsqrt(rden * rden * ssq + EPS)
            y_s[rows, OFF_YM + h * HP:OFF_YM + (h + 1) * HP] = mm(z * fac * mng_ref[:, h * HP:(h + 1) * HP])

    r64 = lax.broadcasted_iota(jnp.int32, (H_CHUNK, H_CHUNK), 0)
    c64 = lax.broadcasted_iota(jnp.int32, (H_CHUNK, H_CHUNK), 1)
    tril_h = jnp.where(r64 >= c64, 1.0, 0.0).astype(BF16)
    bdiff = r64 // H_SUB - c64 // H_SUB
    mask_intra = (bdiff == 0) & (r64 >= c64)
    n_sub = H_CHUNK // H_SUB

    def h_chunk(c, factored):
        rows = slice(c * H_CHUNK, (c + 1) * H_CHUNK)
        g = _cumsum_rows(tril_h, proj_s[rows, OFF_HF:OFF_HI], precise)
        qh = proj_s[rows, OFF_HQ:OFF_HF]
        khc = hk_s[rows, :]
        if not factored:
            gx_s[...] = g
        bnd = [jnp.zeros((1, HW), F32)] + [g[(j + 1) * H_SUB - 1:(j + 1) * H_SUB, :] for j in range(n_sub)]
        g_start = jnp.concatenate([_bcast_rows(bnd[j], H_SUB) for j in range(n_sub)], axis=0)
        g_end = jnp.concatenate([_bcast_rows(bnd[j + 1], H_SUB) for j in range(n_sub)], axis=0)
        g_last = bnd[n_sub]
        qt = qh * jnp.exp(g - g_start)
        khat = khc * jnp.exp(g_end - g)
        kbar = khc * jnp.exp(g_start - g)
        qg = qh * jnp.exp(g)
        kend = khc * jnp.exp(g_last - g)
        dsub = [jnp.exp(bnd[j + 1] - bnd[j]) for j in range(n_sub - 1)]
        ones = jnp.ones((H_SUB, HW), F32)
        qlev = [qt]
        for d in range(1, n_sub - 1):
            fac = jnp.concatenate([ones] * d + [_bcast_rows(dsub[j - d], H_SUB) for j in range(d, n_sub)], axis=0)
            qlev.append(qlev[-1] * fac)
        for h in range(NH):
            sl = slice(h * HP, (h + 1) * HP)
            if factored:
                lhs = mm(jnp.concatenate([ql[:, sl] for ql in qlev], axis=0))
                inter = dot_nt(lhs, mm(khat[:, sl]))
                intra = dot_nt(mm(qt[:, sl]), mm(kbar[:, sl]))
                att = jnp.where(mask_intra, intra, 0.0)
                for d in range(n_sub - 1):
                    att = att + jnp.where(bdiff == d + 1, inter[d * H_CHUNK:(d + 1) * H_CHUNK, :], 0.0)
            else:
                def cols(j, att_acc, sl=sl, gh=g[:, sl], qhh=qh[:, sl]):
                    r0 = pl.multiple_of(j * 8, 8)
                    g8 = gx_s[pl.ds(r0, 8), sl]
                    k8 = hk_s[pl.ds(c * H_CHUNK + r0, 8), sl]
                    for r in range(8):
                        wgt = jnp.exp(jnp.minimum(gh - g8[r:r + 1, :], 0.0))
                        pcol = jnp.sum(qhh * wgt * k8[r:r + 1, :], axis=1, keepdims=True)
                        att_acc = att_acc + jnp.where((c64 == r0 + r) & (r64 >= r0 + r), pcol, 0.0)
                    return att_acc
                att = lax.fori_loop(0, H_CHUNK // 8, cols, jnp.zeros((H_CHUNK, H_CHUNK), F32))
            iv = mm(proj_s[rows, OFF_HI + h * HP:OFF_HI + (h + 1) * HP])
            st = st_s[h]
            o = dot(mm(att), iv) + dot_nt(mm(qg[:, sl]), mm(st))
            st_s[h] = st * jnp.exp(g_last[:, sl]) + dot_tn(iv, mm(kend[:, sl]))
            msq = jnp.sum(o * o, axis=1, keepdims=True) * (1.0 / DH)
            on = o * lax.rsqrt(msq + EPS) * hng_ref[:, sl]
            hg = proj_s[rows, OFF_HG + h * HP:OFF_HG + (h + 1) * HP]
            y_s[rows, OFF_YH + h * HP:OFF_YH + (h + 1) * HP] = mm(on * (hg * _sigmoid(hg)))

    st_old_s[...] = st_s[...]
    for c in range(lb // H_CHUNK):
        h_chunk(c, True)

    def out_proj():
        out = dot(y_s[...], wout_ref[...])
        xo_ref[...] = x + out[pad:lb, :]

    out_proj()
    lg_sub = proj_s[:, OFF_HF:OFF_HI].reshape(lb // H_SUB, H_SUB, HW)
    factor_ok = jnp.min(jnp.sum(lg_sub, axis=1)) > H_SAFE_LOG_DECAY

    @pl.when(jnp.logical_not(factor_ok))
    def _redo_direct():
        st_s[...] = st_old_s[...]
        for c in range(lb // H_CHUNK):
            h_chunk(c, False)
        out_proj()

    @pl.when(t == n_t - 1)
    def _final():
        cf_ref[...] = c_s[...]
        mf_ref[...] = m_s[...]
        convf_ref[...] = conv_tail
        poolf_ref[...] = pool_tail[16:32, :]
        sf_ref[...] = st_s[...]


def _mixer_call(xall, states, shared_init, w, layer, *, row_off, seq_stride, nb, seq, tv, pos0, precise, name,
                stacked_states=False):
    lb = max(tv, M_CHUNK)
    n_t = seq // tv
    assert row_off % tv == 0 and seq_stride % tv == 0 and seq % tv == 0
    blk0, blk_stride = row_off // tv, seq_stride // tv
    c0, m0, conv0, pool0, s0 = states
    lead_blk = (None,) if stacked_states else ()
    lead_idx = (layer,) if stacked_states else ()

    def x_map(b, t):
        return (blk0 + b * blk_stride + t, 0)

    def st_map(b, t):
        return lead_idx + (0 if shared_init else b, 0, 0, 0)

    def st_map3(b, t):
        return lead_idx + (0 if shared_init else b, 0, 0)

    def layer_spec(shape):
        return pl.BlockSpec((None,) + shape, lambda b, t: (layer,) + (0,) * len(shape),
                            pipeline_mode=pl.Buffered(1))

    in_specs = [
        pl.BlockSpec((tv, D_MODEL), x_map),
        pl.BlockSpec(lead_blk + (None, NH, HP, HP), st_map),
        pl.BlockSpec(lead_blk + (None, 8, HP), st_map3),
        pl.BlockSpec(lead_blk + (None, CONV_HDR, 2 * HW), st_map3),
        pl.BlockSpec(lead_blk + (None, 16, P_W), st_map3),
        pl.BlockSpec(lead_blk + (None, NH, HP, HP), st_map),
        layer_spec((1, D_MODEL)),
        layer_spec((D_MODEL, D_INP)),
        layer_spec((1, HP)),
        layer_spec((4, 2 * HW)),
        layer_spec((1, HW)),
        layer_spec((P_W, P_W)),
        layer_spec((1, P_W)),
        pl.BlockSpec((DEPTH, HW), lambda b, t: (0, 0), pipeline_mode=pl.Buffered(1)),
        layer_spec((1, HW)),
        layer_spec((D_MIXP, D_MODEL)),
    ]
    out_specs = [
        pl.BlockSpec((tv, D_MODEL), x_map),
        pl.BlockSpec((None, NH, HP, HP), lambda b, t: (b, 0, 0, 0)),
        pl.BlockSpec((None, 8, HP), lambda b, t: (b, 0, 0)),
        pl.BlockSpec((None, CONV_HDR, 2 * HW), lambda b, t: (b, 0, 0)),
        pl.BlockSpec((None, 16, P_W), lambda b, t: (b, 0, 0)),
        pl.BlockSpec((None, NH, HP, HP), lambda b, t: (b, 0, 0, 0)),
    ]
    out_shape = [
        jax.ShapeDtypeStruct(xall.shape, F32),
        jax.ShapeDtypeStruct((nb, NH, HP, HP), F32),
        jax.ShapeDtypeStruct((nb, 8, HP), F32),
        jax.ShapeDtypeStruct((nb, CONV_HDR, 2 * HW), F32),
        jax.ShapeDtypeStruct((nb, 16, P_W), F32),
        jax.ShapeDtypeStruct((nb, NH, HP, HP), F32),
    ]
    scratch = [
        pltpu.VMEM((lb, D_INP), F32),
        pltpu.VMEM((CONV_HDR + lb, 2 * HW), F32),
        pltpu.VMEM((POOL_HDR + lb, P_W), F32),
        pltpu.VMEM((POOL_HDR + lb, P_W), F32),
        pltpu.VMEM((POOL_HDR + lb, P_W), F32),
        pltpu.VMEM((POOL_HDR + lb, P_W), F32),
        pltpu.VMEM((lb, HW), F32),
        pltpu.VMEM((lb, HP), F32),
        pltpu.VMEM((lb, D_MIXP), F32 if precise else BF16),
        pltpu.VMEM((H_CHUNK, HW), F32),
        pltpu.VMEM((NH, HP, HP), F32),
        pltpu.VMEM((NH, HP, HP), F32),
        pltpu.VMEM((8, HP), F32),
        pltpu.VMEM((NH, HP, HP), F32),
    ]
    sfx = "_f32" if precise else ""
    kern = functools.partial(_mixer_kernel, layer=layer, tv=tv, lb=lb, pos0=pos0, n_t=n_t, precise=precise)
    outs = pl.pallas_call(
        kern,
        grid=(nb, n_t),
        in_specs=in_specs,
        out_specs=out_specs,
        out_shape=out_shape,
        scratch_shapes=scratch,
        input_output_aliases={0: 0},
        compiler_params=pltpu.CompilerParams(dimension_semantics=("arbitrary", "arbitrary"),
                                             vmem_limit_bytes=VMEM_LIMIT),
        name=name,
    )(xall, c0, m0, conv0, pool0, s0,
      w["g1"], w["w_in" + sfx], w["gbias"], w["conv_w"], w["mnorm"],
      w["pool_w" + sfx], w["pool_scale"], w["lb_logits"], w["hnorm"], w["w_out" + sfx])
    return outs[0], tuple(outs[1:])


def _ffn_kernel(x_ref, g_ref, w1_ref, w3_ref, w2_ref, o_ref, hn_s, *, precise):
    f = pl.program_id(1)
    prec = X3 if precise else None

    @pl.when(f == 0)
    def _start():
        x = x_ref[...]
        ms = jnp.mean(x * x, axis=-1, keepdims=True)
        hn_s[...] = (x * lax.rsqrt(ms + EPS) * g_ref[...]).astype(hn_s.dtype)
        o_ref[...] = x

    hn = hn_s[...]
    h1 = _dot(hn, w1_ref[...], prec)
    h3 = _dot(hn, w3_ref[...], prec)
    a = (h1 * _sigmoid(h1) * h3).astype(hn_s.dtype)
    o_ref[...] += _dot(a, w2_ref[...], prec)


def _ffn_rows_f32_call(xall, blocks, g, w1, w3, w2, slot, *, tm, tf, name):
    n_f = D_FF // tf

    def x_map(i, f):
        idx = blocks[-1]
        for j in range(len(blocks) - 2, -1, -1):
            idx = jnp.where(i == j, blocks[j], idx)
        return (idx, 0)

    return pl.pallas_call(
        functools.partial(_ffn_kernel, precise=True),
        grid=(len(blocks), n_f),
        in_specs=[
            pl.BlockSpec((tm, D_MODEL), x_map),
            pl.BlockSpec((1, D_MODEL), lambda i, f: (0, 0)),
            pl.BlockSpec((None, D_MODEL, tf), lambda i, f: (slot, 0, f)),
            pl.BlockSpec((None, D_MODEL, tf), lambda i, f: (slot, 0, f)),
            pl.BlockSpec((None, tf, D_MODEL), lambda i, f: (slot, f, 0)),
        ],
        out_specs=pl.BlockSpec((tm, D_MODEL), lambda i, f: (i, 0)),
        out_shape=jax.ShapeDtypeStruct((len(blocks) * tm, D_MODEL), F32),
        scratch_shapes=[pltpu.VMEM((tm, D_MODEL), F32)],
        compiler_params=pltpu.CompilerParams(dimension_semantics=("arbitrary", "arbitrary"),
                                             vmem_limit_bytes=VMEM_LIMIT),
        name=name,
    )(xall, g, w1, w3, w2)


def _ffn_call(xall, g, w1, w3, w2, slot, *, tm, tf, name):
    n = xall.shape[0]
    n_f = D_FF // tf
    return pl.pallas_call(
        functools.partial(_ffn_kernel, precise=False),
        grid=(n // tm, n_f),
        in_specs=[
            pl.BlockSpec((tm, D_MODEL), lambda i, f: (i, 0)),
            pl.BlockSpec((1, D_MODEL), lambda i, f: (0, 0)),
            pl.BlockSpec((None, D_MODEL, tf), lambda i, f: (slot, 0, f)),
            pl.BlockSpec((None, D_MODEL, tf), lambda i, f: (slot, 0, f)),
            pl.BlockSpec((None, tf, D_MODEL), lambda i, f: (slot, f, 0)),
        ],
        out_specs=pl.BlockSpec((tm, D_MODEL), lambda i, f: (i, 0)),
        out_shape=jax.ShapeDtypeStruct(xall.shape, F32),
        scratch_shapes=[pltpu.VMEM((tm, D_MODEL), BF16)],
        input_output_aliases={0: 0},
        compiler_params=pltpu.CompilerParams(dimension_semantics=("arbitrary", "arbitrary"),
                                             vmem_limit_bytes=VMEM_LIMIT),
        name=name,
    )(xall, g, w1, w3, w2)


GATHER_TILE = 256
NO_SLOT = -1e9


def _route_kernel(x_ref, g_ref, wr_ref, hn_ref, rank_t_ref, rank_c_ref, comb_ref, cnt_ref, *, tb):
    x = x_ref[...]
    ms = jnp.mean(x * x, axis=-1, keepdims=True)
    hn = x * lax.rsqrt(ms + EPS) * g_ref[...]
    hn_ref[...] = hn.astype(BF16)
    logits = _dot(hn, wr_ref[...], X3)
    lane = lax.broadcasted_iota(jnp.int32, (tb, HP), 1).astype(F32)
    lg = jnp.where(lane < N_EXPERTS, logits, NEG)
    v1 = jnp.max(lg, axis=1, keepdims=True)
    i1 = jnp.min(jnp.where(lg == v1, lane, float(HP)), axis=1, keepdims=True)
    mask1 = lane == i1
    lg2 = jnp.where(mask1, NEG, lg)
    v2 = jnp.max(lg2, axis=1, keepdims=True)
    i2 = jnp.min(jnp.where(lg2 == v2, lane, float(HP)), axis=1, keepdims=True)
    mask2 = lane == i2
    ex = jnp.exp(v2 - v1)
    ga = 1.0 / (1.0 + ex)
    comb_ref[...] = jnp.where(mask1, ga, 0.0) + jnp.where(mask2, ex * ga, 0.0)
    sel = mask1 | mask2
    rr = lax.broadcasted_iota(jnp.int32, (tb, tb), 0)
    cc = lax.broadcasted_iota(jnp.int32, (tb, tb), 1)
    tril_strict = jnp.where(rr > cc, 1.0, 0.0).astype(BF16)
    selb = jnp.where(sel, 1.0, 0.0)
    rank = jnp.where(sel, _dot(tril_strict, selb.astype(BF16)), NO_SLOT)
    rank_c_ref[...] = rank
    rank_t_ref[...] = rank.T[0:N_EXPERTS, :]
    cnt_ref[...] = jnp.broadcast_to(jnp.sum(selb, axis=0, keepdims=True), (8, HP))


def _route_call(xall, g, wr, *, tb, name):
    n = xall.shape[0]
    nb = n // tb
    return pl.pallas_call(
        functools.partial(_route_kernel, tb=tb),
        grid=(nb,),
        in_specs=[
            pl.BlockSpec((tb, D_MODEL), lambda i: (i, 0)),
            pl.BlockSpec((1, D_MODEL), lambda i: (0, 0)),
            pl.BlockSpec((D_MODEL, HP), lambda i: (0, 0)),
        ],
        out_specs=[
            pl.BlockSpec((tb, D_MODEL), lambda i: (i, 0)),
            pl.BlockSpec((None, N_EXPERTS, tb), lambda i: (i, 0, 0)),
            pl.BlockSpec((tb, HP), lambda i: (i, 0)),
            pl.BlockSpec((tb, HP), lambda i: (i, 0)),
            pl.BlockSpec((None, 8, HP), lambda i: (i, 0, 0)),
        ],
        out_shape=[
            jax.ShapeDtypeStruct((n, D_MODEL), BF16),
            jax.ShapeDtypeStruct((nb, N_EXPERTS, tb), F32),
            jax.ShapeDtypeStruct((n, HP), F32),
            jax.ShapeDtypeStruct((n, HP), F32),
            jax.ShapeDtypeStruct((nb, 8, HP), F32),
        ],
        compiler_params=pltpu.CompilerParams(dimension_semantics=("arbitrary",), vmem_limit_bytes=VMEM_LIMIT),
        name=name,
    )(xall, g, wr)


def _gather_kernel(tile_ref, blk_ref, exp_ref, off_ref, first_ref, act_ref, rank_t_ref, hn_ref, init_ref, xs_ref,
                   *, tb):
    del tile_ref, blk_ref, init_ref
    p = pl.program_id(0)

    @pl.when(act_ref[p] == 1)
    def _():
        rrow = rank_t_ref[pl.ds(exp_ref[p], 1), :]
        slot = (lax.broadcasted_iota(jnp.int32, (GATHER_TILE, tb), 0) - off_ref[p]).astype(F32)
        onehot = jnp.where(rrow == slot, 1.0, 0.0).astype(BF16)
        val = _dot(onehot, hn_ref[...]).astype(BF16)

        @pl.when(first_ref[p] == 1)
        def _set():
            xs_ref[...] = val

        @pl.when(first_ref[p] == 0)
        def _add():
            xs_ref[...] += val


def _gather_call(pairs, rank_t, hn, init, *, tb, name):
    n_rows = init.shape[0]
    n_pairs = pairs[0].shape[0]
    grid_spec = pltpu.PrefetchScalarGridSpec(
        num_scalar_prefetch=6,
        grid=(n_pairs,),
        in_specs=[
            pl.BlockSpec((None, N_EXPERTS, tb), lambda p, tile, blk, *_: (blk[p], 0, 0)),
            pl.BlockSpec((tb, D_MODEL), lambda p, tile, blk, *_: (blk[p], 0)),
            pl.BlockSpec(memory_space=pl.ANY),
        ],
        out_specs=pl.BlockSpec((GATHER_TILE, D_MODEL), lambda p, tile, *_: (tile[p], 0)),
    )
    return pl.pallas_call(
        functools.partial(_gather_kernel, tb=tb),
        grid_spec=grid_spec,
        out_shape=jax.ShapeDtypeStruct((n_rows, D_MODEL), BF16),
        input_output_aliases={8: 0},
        compiler_params=pltpu.CompilerParams(dimension_semantics=("arbitrary",), vmem_limit_bytes=VMEM_LIMIT),
        name=name,
    )(*pairs, rank_t, hn, init)


def _experts_kernel(exp_ref, act_ref, xs_ref, w1_ref, w3_ref, w2_ref, ys_ref, acc_s, *, n_f):
    del exp_ref
    i = pl.program_id(0)
    f = pl.program_id(1)

    @pl.when(f == 0)
    def _zero():
        acc_s[...] = jnp.zeros(acc_s.shape, F32)

    @pl.when(act_ref[i] == 1)
    def _():
        xe = xs_ref[...]
        h1 = _dot(xe, w1_ref[...].astype(BF16))
        h3 = _dot(xe, w3_ref[...].astype(BF16))
        a = (h1 * _sigmoid(h1) * h3).astype(BF16)
        acc_s[...] += _dot(a, w2_ref[...].astype(BF16))

    @pl.when(f == n_f - 1)
    def _out():
        ys_ref[...] = acc_s[...].astype(BF16)


def _experts_call(tile_exp, tile_act, xs, w1, w3, w2, slot, *, tm, tf, name):
    n_rows = xs.shape[0]
    n_f = D_FF // tf
    grid_spec = pltpu.PrefetchScalarGridSpec(
        num_scalar_prefetch=2,
        grid=(n_rows // tm, n_f),
        in_specs=[
            pl.BlockSpec((tm, D_MODEL), lambda i, f, ex, act: (i, 0)),
            pl.BlockSpec((None, None, D_MODEL, tf), lambda i, f, ex, act: (slot, ex[i], 0, f)),
            pl.BlockSpec((None, None, D_MODEL, tf), lambda i, f, ex, act: (slot, ex[i], 0, f)),
            pl.BlockSpec((None, None, tf, D_MODEL), lambda i, f, ex, act: (slot, ex[i], f, 0)),
        ],
        out_specs=pl.BlockSpec((tm, D_MODEL), lambda i, f, ex, act: (i, 0)),
        scratch_shapes=[pltpu.VMEM((tm, D_MODEL), F32)],
    )
    return pl.pallas_call(
        functools.partial(_experts_kernel, n_f=n_f),
        grid_spec=grid_spec,
        out_shape=jax.ShapeDtypeStruct((n_rows, D_MODEL), BF16),
        compiler_params=pltpu.CompilerParams(dimension_semantics=("arbitrary", "arbitrary"),
                                             vmem_limit_bytes=VMEM_LIMIT),
        name=name,
    )(tile_exp, tile_act, xs, w1, w3, w2)


def _combine_kernel(blk_ref, tile_ref, exp_ref, off_ref, first_ref, last_ref, act_ref,
                    x_ref, rank_c_ref, comb_ref, ys_ref, fg_ref, o_ref, *, tb, final_norm):
    del blk_ref, tile_ref
    p = pl.program_id(0)

    @pl.when(first_ref[p] == 1)
    def _start():
        o_ref[...] = x_ref[...]

    @pl.when(act_ref[p] == 1)
    def _():
        pick = lax.broadcasted_iota(jnp.int32, (tb, HP), 1) == exp_ref[p]
        rcol = jnp.sum(jnp.where(pick, rank_c_ref[...], 0.0), axis=1, keepdims=True)
        gcol = jnp.sum(jnp.where(pick, comb_ref[...], 0.0), axis=1, keepdims=True)
        slot = (lax.broadcasted_iota(jnp.int32, (tb, GATHER_TILE), 1) - off_ref[p]).astype(F32)
        onehot = jnp.where(rcol == slot, 1.0, 0.0).astype(BF16)
        o_ref[...] += gcol * _dot(onehot, ys_ref[...])

    if final_norm:
        @pl.when(last_ref[p] == 1)
        def _norm():
            y = o_ref[...]
            ms = jnp.mean(y * y, axis=-1, keepdims=True)
            o_ref[...] = y * lax.rsqrt(ms + EPS) * fg_ref[...]


def _combine_call(pairs, xall, rank_c, comb, ys, fg, *, tb, final_norm, name):
    n_pairs = pairs[0].shape[0]
    grid_spec = pltpu.PrefetchScalarGridSpec(
        num_scalar_prefetch=7,
        grid=(n_pairs,),
        in_specs=[
            pl.BlockSpec((tb, D_MODEL), lambda p, blk, *_: (blk[p], 0)),
            pl.BlockSpec((tb, HP), lambda p, blk, *_: (blk[p], 0)),
            pl.BlockSpec((tb, HP), lambda p, blk, *_: (blk[p], 0)),
            pl.BlockSpec((GATHER_TILE, D_MODEL), lambda p, blk, tile, *_: (tile[p], 0)),
            pl.BlockSpec((1, D_MODEL), lambda p, *_: (0, 0)),
        ],
        out_specs=pl.BlockSpec((tb, D_MODEL), lambda p, blk, *_: (blk[p], 0)),
    )
    return pl.pallas_call(
        functools.partial(_combine_kernel, tb=tb, final_norm=final_norm),
        grid_spec=grid_spec,
        out_shape=jax.ShapeDtypeStruct(xall.shape, F32),
        input_output_aliases={7: 0},
        compiler_params=pltpu.CompilerParams(dimension_semantics=("arbitrary",), vmem_limit_bytes=VMEM_LIMIT),
        name=name,
    )(*pairs, xall, rank_c, comb, ys, fg)


def _pair_lists(cnt, *, tm, n_rows):
    nb = cnt.shape[0]
    i32 = jnp.int32
    tot = jnp.sum(cnt, axis=0)
    grp_rows = (tot + tm - 1) // tm * tm
    grp_end = jnp.cumsum(grp_rows)
    grp_start = grp_end - grp_rows
    base = grp_start[None, :] + jnp.cumsum(cnt, axis=0) - cnt
    lo = base // GATHER_TILE
    hi = (base + cnt - 1) // GATHER_TILE
    npair = jnp.where(cnt > 0, hi - lo + 1, 0)
    n_pairs = nb * N_EXPERTS + n_rows // GATHER_TILE
    n_keys = nb * N_EXPERTS

    def expand(expert_major):
        def flat(a):
            return (a.T if expert_major else a).reshape(-1)
        np_k = flat(npair)
        cum = jnp.cumsum(np_k)
        total = cum[-1]
        p = jnp.arange(n_pairs, dtype=i32)
        pc = jnp.minimum(p, total - 1)
        k = jnp.sum((cum[None, :] <= pc[:, None]).astype(i32), axis=1)
        table = jnp.stack([flat(lo), flat(base), cum - np_k], axis=1)
        row = jnp.take(table, jnp.minimum(k, n_keys - 1), axis=0)
        tile = row[:, 0] + pc - row[:, 2]
        blk, exp = (k % nb, k // nb) if expert_major else (k // N_EXPERTS, k % N_EXPERTS)
        act = (p < total).astype(i32)
        return tile.astype(i32), blk.astype(i32), exp.astype(i32), (row[:, 1] - tile * GATHER_TILE).astype(i32), act

    g_tile, g_blk, g_exp, g_off, g_act = expand(True)
    g_first = jnp.concatenate([jnp.ones((1,), i32), (g_tile[1:] != g_tile[:-1]).astype(i32)])
    c_tile, c_blk, c_exp, c_off, c_act = expand(False)
    c_first = jnp.concatenate([jnp.ones((1,), i32), (c_blk[1:] != c_blk[:-1]).astype(i32)])
    c_last = jnp.concatenate([(c_blk[1:] != c_blk[:-1]).astype(i32), jnp.ones((1,), i32)])
    c_last = jnp.where(jnp.arange(n_pairs) == jnp.sum(c_act) - 1, 1, c_last) * c_act
    n_tiles = n_rows // tm
    t0 = jnp.arange(n_tiles, dtype=i32) * tm
    t_act = (t0 < grp_end[-1]).astype(i32)
    t_exp = jnp.sum((grp_end[None, :] <= jnp.minimum(t0, grp_end[-1] - 1)[:, None]).astype(i32), axis=1)
    return ((g_tile, g_blk, g_exp, g_off, g_first, g_act),
            (c_blk, c_tile, c_exp, c_off, c_first, c_last, c_act), (t_exp, t_act))


def _moe_layer(xall, g, wr, w1, w3, w2, slot, fg, xs_init, *, tb, tm, tf, final_norm, tag):
    n = xall.shape[0]
    n_rows = 2 * n + N_EXPERTS * tm
    if xs_init is None:
        xs_init = jnp.zeros((n_rows, D_MODEL), BF16)
    hn, rank_t, rank_c, comb, cnt = _route_call(xall, g, wr, tb=tb, name=f"moe_route_{tag}")
    cnt = cnt[:, 0, 0:N_EXPERTS].astype(jnp.int32)
    g_pairs, c_pairs, (t_exp, t_act) = _pair_lists(cnt, tm=tm, n_rows=n_rows)
    xs = _gather_call(g_pairs, rank_t, hn, xs_init, tb=tb, name=f"moe_gather_{tag}")
    ys = _experts_call(t_exp, t_act, xs, w1, w3, w2, slot, tm=tm, tf=tf, name=f"moe_experts_{tag}")
    out = _combine_call(c_pairs, xall, rank_c, comb, ys, fg, tb=tb, final_norm=final_norm,
                        name=f"moe_combine_{tag}")
    return out, xs


def _pad_heads(a, axis):
    axis = axis % a.ndim
    shp = a.shape
    a = a.reshape(shp[:axis] + (NH, DH) + shp[axis + 1:])
    padw = [(0, 0)] * a.ndim
    padw[axis + 1] = (0, HP - DH)
    a = jnp.pad(a, padw)
    return a.reshape(shp[:axis] + (HW,) + shp[axis + 1:])


def _unpad_heads(a, axis):
    axis = axis % a.ndim
    shp = a.shape
    a = a.reshape(shp[:axis] + (NH, HP) + shp[axis + 1:])
    a = lax.slice_in_dim(a, 0, DH, axis=axis + 1)
    return a.reshape(shp[:axis] + (NH * DH,) + shp[axis + 1:])


_M_W = NH * DH
SRC_MQ, SRC_MK, SRC_MV, SRC_MO = 0, _M_W, 2 * _M_W, 3 * _M_W
SRC_MI = 4 * _M_W
SRC_MF = SRC_MI + NH
SRC_PU = SRC_MF + NH
SRC_HQ = SRC_PU + P_W
SRC_HF = SRC_HQ + HW
SRC_HI = SRC_HF + HW
SRC_HG = SRC_HI + _M_W
D_IN = SRC_HG + _M_W


def _w_in_relayout_kernel(w_ref, o32_ref, o16_ref):
    rows = w_ref.shape[0]

    def put(dst, val):
        o32_ref[:, dst:dst + val.shape[1]] = val
        o16_ref[:, dst:dst + val.shape[1]] = val.astype(BF16)

    for src, dst in ((SRC_MQ, OFF_MQ), (SRC_MK, OFF_MK), (SRC_MV, OFF_MV), (SRC_MO, OFF_MO),
                     (SRC_HI, OFF_HI), (SRC_HG, OFF_HG)):
        for h in range(NH):
            put(dst + HP * h, w_ref[:, src + DH * h:src + DH * (h + 1)])
            put(dst + HP * h + DH, jnp.zeros((rows, HP - DH), F32))
    put(OFF_G, w_ref[:, SRC_MI:SRC_MI + NH])
    put(OFF_G + NH, w_ref[:, SRC_MF:SRC_MF + NH])
    put(OFF_G + 2 * NH, jnp.zeros((rows, HP - 2 * NH), F32))
    put(OFF_PU, w_ref[:, SRC_PU:SRC_PU + P_W])
    put(OFF_HQ, w_ref[:, SRC_HQ:SRC_HQ + HW])
    put(OFF_HF, w_ref[:, SRC_HF:SRC_HF + HW])


def _w_in_relayout(w_in, *, tr=256):
    d, k, n = w_in.shape
    assert n == D_IN and k % tr == 0
    return pl.pallas_call(
        _w_in_relayout_kernel,
        grid=(d, k // tr),
        in_specs=[pl.BlockSpec((None, tr, n), lambda l, i: (l, i, 0))],
        out_specs=[pl.BlockSpec((None, tr, D_INP), lambda l, i: (l, i, 0)),
                   pl.BlockSpec((None, tr, D_INP), lambda l, i: (l, i, 0))],
        out_shape=[jax.ShapeDtypeStruct((d, k, D_INP), F32), jax.ShapeDtypeStruct((d, k, D_INP), BF16)],
        compiler_params=pltpu.CompilerParams(dimension_semantics=("arbitrary", "arbitrary"),
                                             vmem_limit_bytes=VMEM_LIMIT),
        name="w_in_relayout",
    )(w_in)


def _prep_weights(norm1_g, w_in, b_igate, b_fgate, conv_w, mlstm_norm_g, pool_w, pool_scale, lb_logits,
                  hgrn_norm_g, w_out):
    m_w = _M_W
    w_in_p, w_in_b = _w_in_relayout(w_in)
    w_out_p = jnp.concatenate([_pad_heads(w_out[:, 0:m_w], 1), w_out[:, m_w:m_w + P_W],
                               _pad_heads(w_out[:, m_w + P_W:], 1)], axis=1)
    gbias = jnp.pad(jnp.concatenate([b_igate, b_fgate], axis=-1), ((0, 0), (0, HP - 2 * NH)))[:, None, :]
    conv_p = jnp.concatenate([_pad_heads(conv_w[..., 0:m_w], -1), _pad_heads(conv_w[..., m_w:], -1)], axis=-1)
    eye = jnp.eye(len(POOL_WINDOWS), dtype=F32)
    pool_bd = jnp.einsum('lgce,gh->lgche', pool_w, eye).reshape(DEPTH, P_W, P_W)
    return {
        "g1": norm1_g[:, None, :], "w_in": w_in_b, "w_in_f32": w_in_p, "gbias": gbias,
        "conv_w": conv_p, "mnorm": _pad_heads(mlstm_norm_g, -1)[:, None, :],
        "pool_w": pool_bd.astype(BF16), "pool_w_f32": pool_bd,
        "pool_scale": pool_scale[:, None, :], "lb_logits": lb_logits,
        "hnorm": _pad_heads(hgrn_norm_g, -1)[:, None, :], "w_out": w_out_p.astype(BF16), "w_out_f32": w_out_p,
    }


def _states_to_kernel(C, n, m, conv, pool, S):
    nb = C.shape[0]
    caug = jnp.concatenate([C, n[..., None]], axis=-1)
    caug = jnp.pad(caug, ((0, 0), (0, 0), (0, HP - DH), (0, HP - DH - 1)))
    mk = jnp.pad(jnp.broadcast_to(m[:, :, None], (nb, NH, HP)), ((0, 0), (0, 8 - NH), (0, 0)))
    m_w = NH * DH
    convk = jnp.concatenate([_pad_heads(conv[..., 0:m_w], -1), _pad_heads(conv[..., m_w:], -1)], axis=-1)
    convk = jnp.pad(convk, ((0, 0), (CONV_HDR - 3, 0), (0, 0)))
    poolk = jnp.pad(pool, ((0, 0), (1, 0), (0, 0)))
    sk = jnp.pad(jnp.swapaxes(S, -1, -2), ((0, 0), (0, 0), (0, HP - DH), (0, 0)))
    return caug, mk, convk, poolk, sk


def _states_from_kernel(st):
    caug, mk, convk, poolk, sk = st
    C = caug[:, :, 0:DH, 0:DH]
    n = caug[:, :, 0:DH, DH]
    m = mk[:, 0:NH, 0]
    conv = convk[:, CONV_HDR - 3:, :]
    conv = jnp.concatenate([_unpad_heads(conv[..., 0:HW], -1), _unpad_heads(conv[..., HW:], -1)], axis=-1)
    pool = poolk[:, 1:, :]
    S = jnp.swapaxes(sk[:, :, 0:DH, :], -1, -2)
    return C, n, m, conv, pool, S


def _pick_tile(n, candidates):
    for c in candidates:
        if n % c == 0:
            return c
    raise ValueError(f"no row tile for {n}")


def kernel(x_prompt, x_sample, state_mlstm_C, state_mlstm_n, state_mlstm_m, state_mlstm_conv, state_pool,
           state_hgrn, meta_tokens, norm1_g, norm2_g, final_g, w_in, b_igate, b_fgate, conv_w, mlstm_norm_g,
           pool_w, pool_scale, lb_logits, hgrn_norm_g, w_out, ffn_w1, ffn_w3, ffn_w2, router_w, moe_w1,
           moe_w3, moe_w2):
    B, T, _ = x_prompt.shape
    SB, ST, _ = x_sample.shape
    w = _prep_weights(norm1_g, w_in, b_igate, b_fgate, conv_w, mlstm_norm_g, pool_w, pool_scale, lb_logits,
                      hgrn_norm_g, w_out)
    ffn_w1b, ffn_w3b, ffn_w2b = ffn_w1.astype(BF16), ffn_w3.astype(BF16), ffn_w2.astype(BF16)
    router_p = jnp.pad(router_w, ((0, 0), (0, 0), (0, HP - N_EXPERTS)))

    n_main = B * T
    off_s = n_main
    off_m = off_s + SB * ST
    assert SB * ST == TAIL and n_main % TAIL == 0
    n_tok = off_m + META_TOKENS
    tile = 1280 if n_main >= 16384 else 256
    expert_tile = 1024 if n_main >= 16384 else 256
    n_pad = -(-n_tok // tile) * tile
    xall = jnp.concatenate([x_prompt.reshape(n_main, D_MODEL), x_sample.reshape(SB * ST, D_MODEL),
                            meta_tokens.astype(F32), jnp.zeros((n_pad - n_tok, D_MODEL), F32)], axis=0)
    tv_main = _pick_tile(T, (256, 128))

    zero_states = (jnp.zeros((1, NH, HP, HP), F32), jnp.zeros((1, 8, HP), F32),
                   jnp.zeros((1, CONV_HDR, 2 * HW), F32), jnp.zeros((1, 16, P_W), F32),
                   jnp.zeros((1, NH, HP, HP), F32))
    p_states, s_states = [], []
    xs_buf = None
    st_in = jax.vmap(_states_to_kernel)(state_mlstm_C, state_mlstm_n, state_mlstm_m, state_mlstm_conv,
                                        state_pool, state_hgrn)
    tail_blocks = tuple((b * T + T - TAIL) // TAIL for b in range(B)) + (off_s // TAIL,)
    for l in range(DEPTH):
        precise = l < PRECISE_LAYERS
        xall, st_meta = _mixer_call(xall, zero_states, True, w, l, row_off=off_m, seq_stride=META_TOKENS, nb=1,
                                    seq=META_TOKENS, tv=META_TOKENS, pos0=0, precise=False,
                                    name=f"mixer_meta_{l}")
        if precise:
            xall, st_p = _mixer_call(xall, st_meta, True, w, l, row_off=0, seq_stride=T, nb=B, seq=T - TAIL,
                                     tv=tv_main, pos0=META_TOKENS, precise=False, name=f"mixer_prompt_{l}")
            xall, st_p = _mixer_call(xall, st_p, False, w, l, row_off=T - TAIL, seq_stride=T, nb=B, seq=TAIL,
                                     tv=TAIL, pos0=META_TOKENS + T - TAIL, precise=True,
                                     name=f"mixer_prompt_tail_{l}")
        else:
            xall, st_p = _mixer_call(xall, st_meta, True, w, l, row_off=0, seq_stride=T, nb=B, seq=T,
                                     tv=tv_main, pos0=META_TOKENS, precise=False, name=f"mixer_prompt_{l}")
        xall, st_s = _mixer_call(xall, st_in, False, w, l, row_off=off_s, seq_stride=ST, nb=SB, seq=ST, tv=ST,
                                 pos0=META_TOKENS + PAST_LEN, precise=True, name=f"mixer_sample_{l}",
                                 stacked_states=True)
        p_states.append(st_p)
        s_states.append(st_s)
        i = l // 2
        if l % 2 == 0:
            f32_blocks = tail_blocks if precise else tail_blocks[-1:]
            rows = _ffn_rows_f32_call(xall, f32_blocks, norm2_g[l][None, :], ffn_w1, ffn_w3, ffn_w2, i,
                                      tm=TAIL, tf=512, name=f"ffn_rows_f32_{l}")
            xall = _ffn_call(xall, norm2_g[l][None, :], ffn_w1b, ffn_w3b, ffn_w2b, i,
                             tm=tile, tf=512, name=f"ffn_{l}")
            for j, blk in enumerate(f32_blocks):
                xall = lax.dynamic_update_slice(xall, rows[j * TAIL:(j + 1) * TAIL], (blk * TAIL, 0))
        else:
            xall, xs_buf = _moe_layer(xall, norm2_g[l][None, :], router_p[i], moe_w1, moe_w3, moe_w2, i,
                                      final_g[None, :], xs_buf, tb=tile, tm=expert_tile, tf=512,
                                      final_norm=(l == DEPTH - 1), tag=str(l))
    y_prompt = xall[0:n_main].reshape(B, T, D_MODEL)
    y_sample = xall[off_s:off_m].reshape(SB, ST, D_MODEL)
    p_out = jax.vmap(_states_from_kernel)(tuple(jnp.stack([s[j] for s in p_states], axis=0) for j in range(5)))
    s_out = jax.vmap(_states_from_kernel)(tuple(jnp.stack([s[j] for s in s_states], axis=0) for j in range(5)))
    return (y_prompt, y_sample) + tuple(p_out) + tuple(s_out)
```

```python
import functools

import jax
import jax.numpy as jnp
from jax import lax
from jax.experimental import pallas as pl
from jax.experimental.pallas import tpu as pltpu

F32 = jnp.float32
BF16 = jnp.bfloat16

D_MODEL = 1024
DEPTH = 4
META_TOKENS = 16
PAST_LEN = 4096
EPS = 1e-6
NH = 4
DH = 96
HP = 128
HW = NH * HP
P_W = 256
P_GW = 64
POOL_WINDOWS = (2, 4, 8, 16)
D_FF = 3584
N_EXPERTS = 8
NEG = -1e30

OFF_MQ, OFF_MK, OFF_MV, OFF_MO = 0, 512, 1024, 1536
OFF_G = 2048
OFF_PU = 2176
OFF_HQ, OFF_HF, OFF_HI, OFF_HG = 2432, 2944, 3456, 3968
D_INP = 4480
OFF_YM, OFF_YP, OFF_YH = 0, 512, 768
D_MIXP = 1280

M_CHUNK = 128
H_CHUNK = 128
H_SUB = 32
H_SAFE_LOG_DECAY = -60.0
CONV_HDR = 8
POOL_HDR = 32

TAIL = 256
PRECISE_LAYERS = 2

VMEM_LIMIT = 60 * 1024 * 1024


def _sigmoid(x):
    return 1.0 / (1.0 + jnp.exp(-x))


X3 = "bf16x3"


def _split_bf16(a):
    hi = a.astype(BF16)
    return hi, (a - hi.astype(F32)).astype(BF16)


def _dg(a, b, dims, precision=None):
    if precision == X3:
        ah, al = _split_bf16(a)
        bh, bl = _split_bf16(b)
        return (lax.dot_general(ah, bh, dims, preferred_element_type=F32)
                + lax.dot_general(al, bh, dims, preferred_element_type=F32)
                + lax.dot_general(ah, bl, dims, preferred_element_type=F32))
    return lax.dot_general(a, b, dims, preferred_element_type=F32, precision=precision)


def _dot(a, b, precision=None):
    return _dg(a, b, (((1,), (0,)), ((), ())), precision)


def _dot_nt(a, b, precision=None):
    return _dg(a, b, (((1,), (1,)), ((), ())), precision)


def _dot_tn(a, b, precision=None):
    return _dg(a, b, (((0,), (0,)), ((), ())), precision)


def _cumsum_rows(tril_bf, x, precise=False):
    n = x.shape[1]
    hi = x.astype(BF16)
    rest = x - hi.astype(F32)
    lo = rest.astype(BF16)
    parts = [hi, lo] + ([(rest - lo.astype(F32)).astype(BF16)] if precise else [])
    both = _dot(tril_bf, jnp.concatenate(parts, axis=1))
    out = both[:, :n] + both[:, n:2 * n]
    return out + both[:, 2 * n:] if precise else out


def _bcast_rows(row, n):
    return jnp.broadcast_to(row, (n, row.shape[1]))


def _mixer_kernel(x_ref, c0_ref, m0_ref, conv0_ref, pool0_ref, s0_ref,
                  g1_ref, win_ref, gb_ref, cw_ref, mng_ref, pw_ref, ps_ref, lbl_ref, hng_ref, wout_ref,
                  xo_ref, cf_ref, mf_ref, convf_ref, poolf_ref, sf_ref,
                  proj_s, qk_s, u_s, s2_s, s4_s, s8_s, hk_s, gate_s, y_s, gx_s, st_old_s, c_s, m_s, st_s,
                  *, layer, tv, lb, pos0, n_t, precise):
    pad = lb - tv
    t = pl.program_id(1)
    prec = X3 if precise else None
    dot = functools.partial(_dot, precision=prec)
    dot_nt = functools.partial(_dot_nt, precision=prec)
    dot_tn = functools.partial(_dot_tn, precision=prec)

    def mm(a):
        return a if precise else a.astype(BF16)

    @pl.when(t == 0)
    def _init():
        c_s[...] = c0_ref[...]
        m_s[...] = m0_ref[...]
        st_s[...] = s0_ref[...]
        qk_s[...] = jnp.zeros(qk_s.shape, F32)
        qk_s[pad:pad + CONV_HDR, :] = conv0_ref[...]
        u_s[...] = jnp.zeros(u_s.shape, F32)
        u_s[pad + 16:pad + 32, :] = pool0_ref[...]
        s2_s[0:POOL_HDR, :] = jnp.zeros((POOL_HDR, P_W), F32)
        s4_s[0:POOL_HDR, :] = jnp.zeros((POOL_HDR, P_W), F32)
        s8_s[0:POOL_HDR, :] = jnp.zeros((POOL_HDR, P_W), F32)
        if pad:
            proj_s[0:pad, :] = jnp.zeros((pad, D_INP), F32)

    x = x_ref[...]
    ms = jnp.mean(x * x, axis=-1, keepdims=True)
    hn = mm(x * lax.rsqrt(ms + EPS) * g1_ref[...])
    qk_s[CONV_HDR + pad:CONV_HDR + lb, :] = dot(hn, win_ref[:, 0:OFF_MV])
    proj_s[pad:lb, OFF_MV:OFF_PU] = dot(hn, win_ref[:, OFF_MV:OFF_PU])
    u_s[POOL_HDR + pad:POOL_HDR + lb, :] = dot(hn, win_ref[:, OFF_PU:OFF_HQ])
    proj_s[pad:lb, OFF_HQ:D_INP] = dot(hn, win_ref[:, OFF_HQ:D_INP])

    row = lax.broadcasted_iota(jnp.int32, (lb, HP), 0)
    lane = lax.broadcasted_iota(jnp.int32, (lb, HP), 1)

    acc = qk_s[5:5 + lb, :] * cw_ref[0:1, :]
    for j in range(1, 4):
        acc = acc + qk_s[5 + j:5 + j + lb, :] * cw_ref[j:j + 1, :]
    qk = acc * _sigmoid(acc)
    proj_s[:, OFF_MQ:OFF_MK] = qk[:, 0:HW]
    proj_s[:, OFF_MK:OFF_MV] = qk[:, HW:2 * HW] * (DH ** -0.5)
    conv_tail = qk_s[lb:lb + CONV_HDR, :]
    qk_s[0:CONV_HDR, :] = conv_tail

    gpre = proj_s[:, OFF_G:OFF_G + HP] + gb_ref[...]
    lsig = jnp.minimum(gpre, 0.0) - jnp.log(1.0 + jnp.exp(-jnp.abs(gpre)))
    gates = jnp.where(lane < NH, gpre, jnp.where(lane < 2 * NH, lsig, 0.0))
    if pad:
        gates = jnp.where(row >= pad, gates, jnp.where(lane < NH, NEG, 0.0))
    gate_s[...] = gates

    n_ext = lb + 16
    s2_s[16:16 + n_ext, :] = u_s[16:16 + n_ext, :] + u_s[15:15 + n_ext, :]
    s4_s[16:16 + n_ext, :] = s2_s[16:16 + n_ext, :] + s2_s[14:14 + n_ext, :]
    s8_s[16:16 + n_ext, :] = s4_s[16:16 + n_ext, :] + s4_s[12:12 + n_ext, :]
    u_cur = u_s[POOL_HDR:POOL_HDR + lb, :]
    w2 = s2_s[POOL_HDR:POOL_HDR + lb, :]
    w4 = s4_s[POOL_HDR:POOL_HDR + lb, :]
    w8 = s8_s[POOL_HDR:POOL_HDR + lb, :]
    w16 = w8 + s8_s[POOL_HDR - 8:POOL_HDR - 8 + lb, :]
    lane_p = lax.broadcasted_iota(jnp.int32, (lb, P_W), 1)
    wsum = jnp.where(lane_p < P_GW, w2, jnp.where(lane_p < 2 * P_GW, w4, jnp.where(lane_p < 3 * P_GW, w8, w16)))
    if pos0 >= POOL_WINDOWS[-1] - 1:
        inv = jnp.where(lane_p < P_GW, 0.5, jnp.where(lane_p < 2 * P_GW, 0.25,
                                                       jnp.where(lane_p < 3 * P_GW, 0.125, 0.0625)))
        mean = wsum * inv
    else:
        row_p = lax.broadcasted_iota(jnp.int32, (lb, P_W), 0)
        posn = (row_p + (pos0 + 1 - pad + t * tv)).astype(F32)
        wlen = jnp.where(lane_p < P_GW, 2.0, jnp.where(lane_p < 2 * P_GW, 4.0,
                                                        jnp.where(lane_p < 3 * P_GW, 8.0, 16.0)))
        mean = wsum / jnp.maximum(jnp.minimum(wlen, posn), 1.0)
    pooled = mm(mean - u_cur)
    y_s[:, OFF_YP:OFF_YH] = mm(dot(pooled, pw_ref[...]) * ps_ref[...])
    pool_tail = u_s[lb:lb + POOL_HDR, :]
    u_s[0:POOL_HDR, :] = pool_tail

    lbl = lbl_ref[...]
    e = jnp.exp(lbl - jnp.max(lbl, axis=0, keepdims=True))
    p = e / jnp.sum(e, axis=0, keepdims=True)
    lbv = jnp.sum(p[0:layer + 1, :], axis=0, keepdims=True) - p[0:1, :]
    hq = proj_s[:, OFF_HQ:OFF_HF]
    hf = proj_s[:, OFF_HF:OFF_HI]
    fg = lbv + (1.0 - lbv) * _sigmoid(hf)
    kh = 1.0 - fg
    lg = jnp.log(fg)
    if pad:
        row_h = lax.broadcasted_iota(jnp.int32, (lb, HW), 0)
        kh = jnp.where(row_h >= pad, kh, 0.0)
        lg = jnp.where(row_h >= pad, lg, 0.0)
    proj_s[:, OFF_HQ:OFF_HF] = hq * _sigmoid(hq)
    proj_s[:, OFF_HF:OFF_HI] = lg
    hk_s[...] = kh

    rr = lax.broadcasted_iota(jnp.int32, (M_CHUNK, M_CHUNK), 0)
    cc = lax.broadcasted_iota(jnp.int32, (M_CHUNK, M_CHUNK), 1)
    causal = rr >= cc
    tril_m = jnp.where(causal, 1.0, 0.0).astype(BF16)
    lane_c = lax.broadcasted_iota(jnp.int32, (M_CHUNK, HP), 1)

    for c in range(lb // M_CHUNK):
        rows = slice(c * M_CHUNK, (c + 1) * M_CHUNK)
        gt = gate_s[rows, :]
        bcum = _cumsum_rows(tril_m, jnp.where(lane_c >= NH, gt, 0.0), precise)
        cg = gt - pltpu.roll(bcum, HP - NH, axis=1)
        cg_t = cg.T
        for h in range(NH):
            q = proj_s[rows, OFF_MQ + h * HP:OFF_MQ + (h + 1) * HP]
            k = proj_s[rows, OFF_MK + h * HP:OFF_MK + (h + 1) * HP]
            v = proj_s[rows, OFF_MV + h * HP:OFF_MV + (h + 1) * HP]
            c_row = cg_t[h:h + 1, :]
            c_col = cg[:, h:h + 1]
            b_col = bcum[:, NH + h:NH + h + 1]
            m_prev = m_s[h:h + 1, 0:1]
            mx = jnp.maximum(jnp.max(jnp.where(causal, c_row, NEG), axis=1, keepdims=True), m_prev)
            w = jnp.exp(jnp.where(causal, c_row - mx, NEG))
            w_int = jnp.exp(m_prev - mx)
            mx_last = mx[M_CHUNK - 1:M_CHUNK, :]
            s = dot_nt(mm(q), mm(k)) * w
            vaug = jnp.where(lane_c == DH, 1.0, v)
            caug = c_s[h]
            lhs = mm(jnp.concatenate([s, q * w_int], axis=1))
            rhs = mm(jnp.concatenate([vaug, caug], axis=0))
            nd = dot(lhs, rhs)
            den = nd[:, DH:DH + 1]
            rden = 1.0 / jnp.maximum(jnp.abs(den), jnp.exp(-(b_col + mx)))
            wl = jnp.exp(c_col - mx_last)
            decay = jnp.exp(m_prev - mx_last)
            c_s[h] = decay * caug + dot_tn(mm(k * wl), mm(vaug))
            m_s[h:h + 1, :] = jnp.broadcast_to(b_col[M_CHUNK - 1:M_CHUNK, :] + mx_last, (1, HP))
            mo = proj_s[rows, OFF_MO + h * HP:OFF_MO + (h + 1) * HP]
            z = jnp.where(lane_c < DH, nd * _sigmoid(mo), 0.0)
            ssq = jnp.sum(z * z, axis=1, keepdims=True) * (1.0 / DH)
            fac = rden * lax.rsqrt(rden * rden * ssq + EPS)
            y_s[rows, OFF_YM + h * HP:OFF_YM + (h + 1) * HP] = mm(z * fac * mng_ref[:, h * HP:(h + 1) * HP])

    r64 = lax.broadcasted_iota(jnp.int32, (H_CHUNK, H_CHUNK), 0)
    c64 = lax.broadcasted_iota(jnp.int32, (H_CHUNK, H_CHUNK), 1)
    tril_h = jnp.where(r64 >= c64, 1.0, 0.0).astype(BF16)
    bdiff = r64 // H_SUB - c64 // H_SUB
    mask_intra = (bdiff == 0) & (r64 >= c64)
    n_sub = H_CHUNK // H_SUB

    def h_chunk(c, factored):
        rows = slice(c * H_CHUNK, (c + 1) * H_CHUNK)
        g = _cumsum_rows(tril_h, proj_s[rows, OFF_HF:OFF_HI], precise)
        qh = proj_s[rows, OFF_HQ:OFF_HF]
        khc = hk_s[rows, :]
        if not factored:
            gx_s[...] = g
        bnd = [jnp.zeros((1, HW), F32)] + [g[(j + 1) * H_SUB - 1:(j + 1) * H_SUB, :] for j in range(n_sub)]
        g_start = jnp.concatenate([_bcast_rows(bnd[j], H_SUB) for j in range(n_sub)], axis=0)
        g_end = jnp.concatenate([_bcast_rows(bnd[j + 1], H_SUB) for j in range(n_sub)], axis=0)
        g_last = bnd[n_sub]
        qt = qh * jnp.exp(g - g_start)
        khat = khc * jnp.exp(g_end - g)
        kbar = khc * jnp.exp(g_start - g)
        qg = qh * jnp.exp(g)
        kend = khc * jnp.exp(g_last - g)
        dsub = [jnp.exp(bnd[j + 1] - bnd[j]) for j in range(n_sub - 1)]
        ones = jnp.ones((H_SUB, HW), F32)
        qlev = [qt]
        for d in range(1, n_sub - 1):
            fac = jnp.concatenate([ones] * d + [_bcast_rows(dsub[j - d], H_SUB) for j in range(d, n_sub)], axis=0)
            qlev.append(qlev[-1] * fac)
        for h in range(NH):
            sl = slice(h * HP, (h + 1) * HP)
            if factored:
                lhs = mm(jnp.concatenate([ql[:, sl] for ql in qlev], axis=0))
                inter = dot_nt(lhs, mm(khat[:, sl]))
                intra = dot_nt(mm(qt[:, sl]), mm(kbar[:, sl]))
                att = jnp.where(mask_intra, intra, 0.0)
                for d in range(n_sub - 1):
                    att = att + jnp.where(bdiff == d + 1, inter[d * H_CHUNK:(d + 1) * H_CHUNK, :], 0.0)
            else:
                def cols(j, att_acc, sl=sl, gh=g[:, sl], qhh=qh[:, sl]):
                    r0 = pl.multiple_of(j * 8, 8)
                    g8 = gx_s[pl.ds(r0, 8), sl]
                    k8 = hk_s[pl.ds(c * H_CHUNK + r0, 8), sl]
                    for r in range(8):
                        wgt = jnp.exp(jnp.minimum(gh - g8[r:r + 1, :], 0.0))
                        pcol = jnp.sum(qhh * wgt * k8[r:r + 1, :], axis=1, keepdims=True)
                        att_acc = att_acc + jnp.where((c64 == r0 + r) & (r64 >= r0 + r), pcol, 0.0)
                    return att_acc
                att = lax.fori_loop(0, H_CHUNK // 8, cols, jnp.zeros((H_CHUNK, H_CHUNK), F32))
            iv = mm(proj_s[rows, OFF_HI + h * HP:OFF_HI + (h + 1) * HP])
            st = st_s[h]
            o = dot(mm(att), iv) + dot_nt(mm(qg[:, sl]), mm(st))
            st_s[h] = st * jnp.exp(g_last[:, sl]) + dot_tn(iv, mm(kend[:, sl]))
            msq = jnp.sum(o * o, axis=1, keepdims=True) * (1.0 / DH)
            on = o * lax.rsqrt(msq + EPS) * hng_ref[:, sl]
            hg = proj_s[rows, OFF_HG + h * HP:OFF_HG + (h + 1) * HP]
            y_s[rows, OFF_YH + h * HP:OFF_YH + (h + 1) * HP] = mm(on * (hg * _sigmoid(hg)))

    st_old_s[...] = st_s[...]
    for c in range(lb // H_CHUNK):
        h_chunk(c, True)

    def out_proj():
        out = dot(y_s[...], wout_ref[...])
        xo_ref[...] = x + out[pad:lb, :]

    out_proj()
    lg_sub = proj_s[:, OFF_HF:OFF_HI].reshape(lb // H_SUB, H_SUB, HW)
    factor_ok = jnp.min(jnp.sum(lg_sub, axis=1)) > H_SAFE_LOG_DECAY

    @pl.when(jnp.logical_not(factor_ok))
    def _redo_direct():
        st_s[...] = st_old_s[...]
        for c in range(lb // H_CHUNK):
            h_chunk(c, False)
        out_proj()

    @pl.when(t == n_t - 1)
    def _final():
        cf_ref[...] = c_s[...]
        mf_ref[...] = m_s[...]
        convf_ref[...] = conv_tail
        poolf_ref[...] = pool_tail[16:32, :]
        sf_ref[...] = st_s[...]


def _mixer_call(xall, states, shared_init, w, layer, *, row_off, seq_stride, nb, seq, tv, pos0, precise, name,
                stacked_states=False):
    lb = max(tv, M_CHUNK)
    n_t = seq // tv
    assert row_off % tv == 0 and seq_stride % tv == 0 and seq % tv == 0
    blk0, blk_stride = row_off // tv, seq_stride // tv
    c0, m0, conv0, pool0, s0 = states
    lead_blk = (None,) if stacked_states else ()
    lead_idx = (layer,) if stacked_states else ()

    def x_map(b, t):
        return (blk0 + b * blk_stride + t, 0)

    def st_map(b, t):
        return lead_idx + (0 if shared_init else b, 0, 0, 0)

    def st_map3(b, t):
        return lead_idx + (0 if shared_init else b, 0, 0)

    def layer_spec(shape):
        return pl.BlockSpec((None,) + shape, lambda b, t: (layer,) + (0,) * len(shape),
                            pipeline_mode=pl.Buffered(1))

    in_specs = [
        pl.BlockSpec((tv, D_MODEL), x_map),
        pl.BlockSpec(lead_blk + (None, NH, HP, HP), st_map),
        pl.BlockSpec(lead_blk + (None, 8, HP), st_map3),
        pl.BlockSpec(lead_blk + (None, CONV_HDR, 2 * HW), st_map3),
        pl.BlockSpec(lead_blk + (None, 16, P_W), st_map3),
        pl.BlockSpec(lead_blk + (None, NH, HP, HP), st_map),
        layer_spec((1, D_MODEL)),
        layer_spec((D_MODEL, D_INP)),
        layer_spec((1, HP)),
        layer_spec((4, 2 * HW)),
        layer_spec((1, HW)),
        layer_spec((P_W, P_W)),
        layer_spec((1, P_W)),
        pl.BlockSpec((DEPTH, HW), lambda b, t: (0, 0), pipeline_mode=pl.Buffered(1)),
        layer_spec((1, HW)),
        layer_spec((D_MIXP, D_MODEL)),
    ]
    out_specs = [
        pl.BlockSpec((tv, D_MODEL), x_map),
        pl.BlockSpec((None, NH, HP, HP), lambda b, t: (b, 0, 0, 0)),
        pl.BlockSpec((None, 8, HP), lambda b, t: (b, 0, 0)),
        pl.BlockSpec((None, CONV_HDR, 2 * HW), lambda b, t: (b, 0, 0)),
        pl.BlockSpec((None, 16, P_W), lambda b, t: (b, 0, 0)),
        pl.BlockSpec((None, NH, HP, HP), lambda b, t: (b, 0, 0, 0)),
    ]
    out_shape = [
        jax.ShapeDtypeStruct(xall.shape, F32),
        jax.ShapeDtypeStruct((nb, NH, HP, HP), F32),
        jax.ShapeDtypeStruct((nb, 8, HP), F32),
        jax.ShapeDtypeStruct((nb, CONV_HDR, 2 * HW), F32),
        jax.ShapeDtypeStruct((nb, 16, P_W), F32),
        jax.ShapeDtypeStruct((nb, NH, HP, HP), F32),
    ]
    scratch = [
        pltpu.VMEM((lb, D_INP), F32),
        pltpu.VMEM((CONV_HDR + lb, 2 * HW), F32),
        pltpu.VMEM((POOL_HDR + lb, P_W), F32),
        pltpu.VMEM((POOL_HDR + lb, P_W), F32),
        pltpu.VMEM((POOL_HDR + lb, P_W), F32),
        pltpu.VMEM((POOL_HDR + lb, P_W), F32),
        pltpu.VMEM((lb, HW), F32),
        pltpu.VMEM((lb, HP), F32),
        pltpu.VMEM((lb, D_MIXP), F32 if precise else BF16),
        pltpu.VMEM((H_CHUNK, HW), F32),
        pltpu.VMEM((NH, HP, HP), F32),
        pltpu.VMEM((NH, HP, HP), F32),
        pltpu.VMEM((8, HP), F32),
        pltpu.VMEM((NH, HP, HP), F32),
    ]
    sfx = "_f32" if precise else ""
    kern = functools.partial(_mixer_kernel, layer=layer, tv=tv, lb=lb, pos0=pos0, n_t=n_t, precise=precise)
    outs = pl.pallas_call(
        kern,
        grid=(nb, n_t),
        in_specs=in_specs,
        out_specs=out_specs,
        out_shape=out_shape,
        scratch_shapes=scratch,
        input_output_aliases={0: 0},
        compiler_params=pltpu.CompilerParams(dimension_semantics=("arbitrary", "arbitrary"),
                                             vmem_limit_bytes=VMEM_LIMIT),
        name=name,
    )(xall, c0, m0, conv0, pool0, s0,
      w["g1"], w["w_in" + sfx], w["gbias"], w["conv_w"], w["mnorm"],
      w["pool_w" + sfx], w["pool_scale"], w["lb_logits"], w["hnorm"], w["w_out" + sfx])
    return outs[0], tuple(outs[1:])


def _ffn_kernel(x_ref, g_ref, w1_ref, w3_ref, w2_ref, o_ref, hn_s, *, precise):
    f = pl.program_id(1)
    prec = X3 if precise else None

    @pl.when(f == 0)
    def _start():
        x = x_ref[...]
        ms = jnp.mean(x * x, axis=-1, keepdims=True)
        hn_s[...] = (x * lax.rsqrt(ms + EPS) * g_ref[...]).astype(hn_s.dtype)
        o_ref[...] = x

    hn = hn_s[...]
    h1 = _dot(hn, w1_ref[...], prec)
    h3 = _dot(hn, w3_ref[...], prec)
    a = (h1 * _sigmoid(h1) * h3).astype(hn_s.dtype)
    o_ref[...] += _dot(a, w2_ref[...], prec)


def _ffn_rows_f32_call(xall, blocks, g, w1, w3, w2, slot, *, tm, tf, name):
    n_f = D_FF // tf

    def x_map(i, f):
        idx = blocks[-1]
        for j in range(len(blocks) - 2, -1, -1):
            idx = jnp.where(i == j, blocks[j], idx)
        return (idx, 0)

    return pl.pallas_call(
        functools.partial(_ffn_kernel, precise=True),
        grid=(len(blocks), n_f),
        in_specs=[
            pl.BlockSpec((tm, D_MODEL), x_map),
            pl.BlockSpec((1, D_MODEL), lambda i, f: (0, 0)),
            pl.BlockSpec((None, D_MODEL, tf), lambda i, f: (slot, 0, f)),
            pl.BlockSpec((None, D_MODEL, tf), lambda i, f: (slot, 0, f)),
            pl.BlockSpec((None, tf, D_MODEL), lambda i, f: (slot, f, 0)),
        ],
        out_specs=pl.BlockSpec((tm, D_MODEL), lambda i, f: (i, 0)),
        out_shape=jax.ShapeDtypeStruct((len(blocks) * tm, D_MODEL), F32),
        scratch_shapes=[pltpu.VMEM((tm, D_MODEL), F32)],
        compiler_params=pltpu.CompilerParams(dimension_semantics=("arbitrary", "arbitrary"),
                                             vmem_limit_bytes=VMEM_LIMIT),
        name=name,
    )(xall, g, w1, w3, w2)


def _ffn_call(xall, g, w1, w3, w2, slot, *, tm, tf, name):
    n = xall.shape[0]
    n_f = D_FF // tf
    return pl.pallas_call(
        functools.partial(_ffn_kernel, precise=False),
        grid=(n // tm, n_f),
        in_specs=[
            pl.BlockSpec((tm, D_MODEL), lambda i, f: (i, 0)),
            pl.BlockSpec((1, D_MODEL), lambda i, f: (0, 0)),
            pl.BlockSpec((None, D_MODEL, tf), lambda i, f: (slot, 0, f)),
            pl.BlockSpec((None, D_MODEL, tf), lambda i, f: (slot, 0, f)),
            pl.BlockSpec((None, tf, D_MODEL), lambda i, f: (slot, f, 0)),
        ],
        out_specs=pl.BlockSpec((tm, D_MODEL), lambda i, f: (i, 0)),
        out_shape=jax.ShapeDtypeStruct(xall.shape, F32),
        scratch_shapes=[pltpu.VMEM((tm, D_MODEL), BF16)],
        input_output_aliases={0: 0},
        compiler_params=pltpu.CompilerParams(dimension_semantics=("arbitrary", "arbitrary"),
                                             vmem_limit_bytes=VMEM_LIMIT),
        name=name,
    )(xall, g, w1, w3, w2)


GATHER_TILE = 256
NO_SLOT = -1e9


def _route_kernel(x_ref, g_ref, wr_ref, hn_ref, rank_t_ref, rank_c_ref, comb_ref, cnt_ref, *, tb):
    x = x_ref[...]
    ms = jnp.mean(x * x, axis=-1, keepdims=True)
    hn = x * lax.rsqrt(ms + EPS) * g_ref[...]
    hn_ref[...] = hn.astype(BF16)
    logits = _dot(hn, wr_ref[...], X3)
    lane = lax.broadcasted_iota(jnp.int32, (tb, HP), 1).astype(F32)
    lg = jnp.where(lane < N_EXPERTS, logits, NEG)
    v1 = jnp.max(lg, axis=1, keepdims=True)
    i1 = jnp.min(jnp.where(lg == v1, lane, float(HP)), axis=1, keepdims=True)
    mask1 = lane == i1
    lg2 = jnp.where(mask1, NEG, lg)
    v2 = jnp.max(lg2, axis=1, keepdims=True)
    i2 = jnp.min(jnp.where(lg2 == v2, lane, float(HP)), axis=1, keepdims=True)
    mask2 = lane == i2
    ex = jnp.exp(v2 - v1)
    ga = 1.0 / (1.0 + ex)
    comb_ref[...] = jnp.where(mask1, ga, 0.0) + jnp.where(mask2, ex * ga, 0.0)
    sel = mask1 | mask2
    rr = lax.broadcasted_iota(jnp.int32, (tb, tb), 0)
    cc = lax.broadcasted_iota(jnp.int32, (tb, tb), 1)
    tril_strict = jnp.where(rr > cc, 1.0, 0.0).astype(BF16)
    selb = jnp.where(sel, 1.0, 0.0)
    rank = jnp.where(sel, _dot(tril_strict, selb.astype(BF16)), NO_SLOT)
    rank_c_ref[...] = rank
    rank_t_ref[...] = rank.T[0:N_EXPERTS, :]
    cnt_ref[...] = jnp.broadcast_to(jnp.sum(selb, axis=0, keepdims=True), (8, HP))


def _route_call(xall, g, wr, *, tb, name):
    n = xall.shape[0]
    nb = n // tb
    return pl.pallas_call(
        functools.partial(_route_kernel, tb=tb),
        grid=(nb,),
        in_specs=[
            pl.BlockSpec((tb, D_MODEL), lambda i: (i, 0)),
            pl.BlockSpec((1, D_MODEL), lambda i: (0, 0)),
            pl.BlockSpec((D_MODEL, HP), lambda i: (0, 0)),
        ],
        out_specs=[
            pl.BlockSpec((tb, D_MODEL), lambda i: (i, 0)),
            pl.BlockSpec((None, N_EXPERTS, tb), lambda i: (i, 0, 0)),
            pl.BlockSpec((tb, HP), lambda i: (i, 0)),
            pl.BlockSpec((tb, HP), lambda i: (i, 0)),
            pl.BlockSpec((None, 8, HP), lambda i: (i, 0, 0)),
        ],
        out_shape=[
            jax.ShapeDtypeStruct((n, D_MODEL), BF16),
            jax.ShapeDtypeStruct((nb, N_EXPERTS, tb), F32),
            jax.ShapeDtypeStruct((n, HP), F32),
            jax.ShapeDtypeStruct((n, HP), F32),
            jax.ShapeDtypeStruct((nb, 8, HP), F32),
        ],
        compiler_params=pltpu.CompilerParams(dimension_semantics=("arbitrary",), vmem_limit_bytes=VMEM_LIMIT),
        name=name,
    )(xall, g, wr)


def _gather_kernel(tile_ref, blk_ref, exp_ref, off_ref, first_ref, act_ref, rank_t_ref, hn_ref, init_ref, xs_ref,
                   *, tb):
    del tile_ref, blk_ref, init_ref
    p = pl.program_id(0)

    @pl.when(act_ref[p] == 1)
    def _():
        rrow = rank_t_ref[pl.ds(exp_ref[p], 1), :]
        slot = (lax.broadcasted_iota(jnp.int32, (GATHER_TILE, tb), 0) - off_ref[p]).astype(F32)
        onehot = jnp.where(rrow == slot, 1.0, 0.0).astype(BF16)
        val = _dot(onehot, hn_ref[...]).astype(BF16)

        @pl.when(first_ref[p] == 1)
        def _set():
            xs_ref[...] = val

        @pl.when(first_ref[p] == 0)
        def _add():
            xs_ref[...] += val


def _gather_call(pairs, rank_t, hn, init, *, tb, name):
    n_rows = init.shape[0]
    n_pairs = pairs[0].shape[0]
    grid_spec = pltpu.PrefetchScalarGridSpec(
        num_scalar_prefetch=6,
        grid=(n_pairs,),
        in_specs=[
            pl.BlockSpec((None, N_EXPERTS, tb), lambda p, tile, blk, *_: (blk[p], 0, 0)),
            pl.BlockSpec((tb, D_MODEL), lambda p, tile, blk, *_: (blk[p], 0)),
            pl.BlockSpec(memory_space=pl.ANY),
        ],
        out_specs=pl.BlockSpec((GATHER_TILE, D_MODEL), lambda p, tile, *_: (tile[p], 0)),
    )
    return pl.pallas_call(
        functools.partial(_gather_kernel, tb=tb),
        grid_spec=grid_spec,
        out_shape=jax.ShapeDtypeStruct((n_rows, D_MODEL), BF16),
        input_output_aliases={8: 0},
        compiler_params=pltpu.CompilerParams(dimension_semantics=("arbitrary",), vmem_limit_bytes=VMEM_LIMIT),
        name=name,
    )(*pairs, rank_t, hn, init)


def _experts_kernel(exp_ref, act_ref, xs_ref, w1_ref, w3_ref, w2_ref, ys_ref, acc_s, *, n_f):
    del exp_ref
    i = pl.program_id(0)
    f = pl.program_id(1)

    @pl.when(f == 0)
    def _zero():
        acc_s[...] = jnp.zeros(acc_s.shape, F32)

    @pl.when(act_ref[i] == 1)
    def _():
        xe = xs_ref[...]
        h1 = _dot(xe, w1_ref[...].astype(BF16))
        h3 = _dot(xe, w3_ref[...].astype(BF16))
        a = (h1 * _sigmoid(h1) * h3).astype(BF16)
        acc_s[...] += _dot(a, w2_ref[...].astype(BF16))

    @pl.when(f == n_f - 1)
    def _out():
        ys_ref[...] = acc_s[...].astype(BF16)


def _experts_call(tile_exp, tile_act, xs, w1, w3, w2, slot, *, tm, tf, name):
    n_rows = xs.shape[0]
    n_f = D_FF // tf
    grid_spec = pltpu.PrefetchScalarGridSpec(
        num_scalar_prefetch=2,
        grid=(n_rows // tm, n_f),
        in_specs=[
            pl.BlockSpec((tm, D_MODEL), lambda i, f, ex, act: (i, 0)),
            pl.BlockSpec((None, None, D_MODEL, tf), lambda i, f, ex, act: (slot, ex[i], 0, f)),
            pl.BlockSpec((None, None, D_MODEL, tf), lambda i, f, ex, act: (slot, ex[i], 0, f)),
            pl.BlockSpec((None, None, tf, D_MODEL), lambda i, f, ex, act: (slot, ex[i], f, 0)),
        ],
        out_specs=pl.BlockSpec((tm, D_MODEL), lambda i, f, ex, act: (i, 0)),
        scratch_shapes=[pltpu.VMEM((tm, D_MODEL), F32)],
    )
    return pl.pallas_call(
        functools.partial(_experts_kernel, n_f=n_f),
        grid_spec=grid_spec,
        out_shape=jax.ShapeDtypeStruct((n_rows, D_MODEL), BF16),
        compiler_params=pltpu.CompilerParams(dimension_semantics=("arbitrary", "arbitrary"),
                                             vmem_limit_bytes=VMEM_LIMIT),
        name=name,
    )(tile_exp, tile_act, xs, w1, w3, w2)


def _combine_kernel(blk_ref, tile_ref, exp_ref, off_ref, first_ref, last_ref, act_ref,
                    x_ref, rank_c_ref, comb_ref, ys_ref, fg_ref, o_ref, *, tb, final_norm):
    del blk_ref, tile_ref
    p = pl.program_id(0)

    @pl.when(first_ref[p] == 1)
    def _start():
        o_ref[...] = x_ref[...]

    @pl.when(act_ref[p] == 1)
    def _():
        pick = lax.broadcasted_iota(jnp.int32, (tb, HP), 1) == exp_ref[p]
        rcol = jnp.sum(jnp.where(pick, rank_c_ref[...], 0.0), axis=1, keepdims=True)
        gcol = jnp.sum(jnp.where(pick, comb_ref[...], 0.0), axis=1, keepdims=True)
        slot = (lax.broadcasted_iota(jnp.int32, (tb, GATHER_TILE), 1) - off_ref[p]).astype(F32)
        onehot = jnp.where(rcol == slot, 1.0, 0.0).astype(BF16)
        o_ref[...] += gcol * _dot(onehot, ys_ref[...])

    if final_norm:
        @pl.when(last_ref[p] == 1)
        def _norm():
            y = o_ref[...]
            ms = jnp.mean(y * y, axis=-1, keepdims=True)
            o_ref[...] = y * lax.rsqrt(ms + EPS) * fg_ref[...]


def _combine_call(pairs, xall, rank_c, comb, ys, fg, *, tb, final_norm, name):
    n_pairs = pairs[0].shape[0]
    grid_spec = pltpu.PrefetchScalarGridSpec(
        num_scalar_prefetch=7,
        grid=(n_pairs,),
        in_specs=[
            pl.BlockSpec((tb, D_MODEL), lambda p, blk, *_: (blk[p], 0)),
            pl.BlockSpec((tb, HP), lambda p, blk, *_: (blk[p], 0)),
            pl.BlockSpec((tb, HP), lambda p, blk, *_: (blk[p], 0)),
            pl.BlockSpec((GATHER_TILE, D_MODEL), lambda p, blk, tile, *_: (tile[p], 0)),
            pl.BlockSpec((1, D_MODEL), lambda p, *_: (0, 0)),
        ],
        out_specs=pl.BlockSpec((tb, D_MODEL), lambda p, blk, *_: (blk[p], 0)),
    )
    return pl.pallas_call(
        functools.partial(_combine_kernel, tb=tb, final_norm=final_norm),
        grid_spec=grid_spec,
        out_shape=jax.ShapeDtypeStruct(xall.shape, F32),
        input_output_aliases={7: 0},
        compiler_params=pltpu.CompilerParams(dimension_semantics=("arbitrary",), vmem_limit_bytes=VMEM_LIMIT),
        name=name,
    )(*pairs, xall, rank_c, comb, ys, fg)


def _pair_lists(cnt, *, tm, n_rows):
    nb = cnt.shape[0]
    i32 = jnp.int32
    tot = jnp.sum(cnt, axis=0)
    grp_rows = (tot + tm - 1) // tm * tm
    grp_end = jnp.cumsum(grp_rows)
    grp_start = grp_end - grp_rows
    base = grp_start[None, :] + jnp.cumsum(cnt, axis=0) - cnt
    lo = base // GATHER_TILE
    hi = (base + cnt - 1) // GATHER_TILE
    npair = jnp.where(cnt > 0, hi - lo + 1, 0)
    n_pairs = nb * N_EXPERTS + n_rows // GATHER_TILE
    n_keys = nb * N_EXPERTS

    def expand(expert_major):
        def flat(a):
            return (a.T if expert_major else a).reshape(-1)
        np_k = flat(npair)
        cum = jnp.cumsum(np_k)
        total = cum[-1]
        p = jnp.arange(n_pairs, dtype=i32)
        pc = jnp.minimum(p, total - 1)
        k = jnp.sum((cum[None, :] <= pc[:, None]).astype(i32), axis=1)
        table = jnp.stack([flat(lo), flat(base), cum - np_k], axis=1)
        row = jnp.take(table, jnp.minimum(k, n_keys - 1), axis=0)
        tile = row[:, 0] + pc - row[:, 2]
        blk, exp = (k % nb, k // nb) if expert_major else (k // N_EXPERTS, k % N_EXPERTS)
        act = (p < total).astype(i32)
        return tile.astype(i32), blk.astype(i32), exp.astype(i32), (row[:, 1] - tile * GATHER_TILE).astype(i32), act

    g_tile, g_blk, g_exp, g_off, g_act = expand(True)
    g_first = jnp.concatenate([jnp.ones((1,), i32), (g_tile[1:] != g_tile[:-1]).astype(i32)])
    c_tile, c_blk, c_exp, c_off, c_act = expand(False)
    c_first = jnp.concatenate([jnp.ones((1,), i32), (c_blk[1:] != c_blk[:-1]).astype(i32)])
    c_last = jnp.concatenate([(c_blk[1:] != c_blk[:-1]).astype(i32), jnp.ones((1,), i32)])
    c_last = jnp.where(jnp.arange(n_pairs) == jnp.sum(c_act) - 1, 1, c_last) * c_act
    n_tiles = n_rows // tm
    t0 = jnp.arange(n_tiles, dtype=i32) * tm
    t_act = (t0 < grp_end[-1]).astype(i32)
    t_exp = jnp.sum((grp_end[None, :] <= jnp.minimum(t0, grp_end[-1] - 1)[:, None]).astype(i32), axis=1)
    return ((g_tile, g_blk, g_exp, g_off, g_first, g_act),
            (c_blk, c_tile, c_exp, c_off, c_first, c_last, c_act), (t_exp, t_act))


def _moe_layer(xall, g, wr, w1, w3, w2, slot, fg, xs_init, *, tb, tm, tf, final_norm, tag):
    n = xall.shape[0]
    n_rows = 2 * n + N_EXPERTS * tm
    if xs_init is None:
        xs_init = jnp.zeros((n_rows, D_MODEL), BF16)
    hn, rank_t, rank_c, comb, cnt = _route_call(xall, g, wr, tb=tb, name=f"moe_route_{tag}")
    cnt = cnt[:, 0, 0:N_EXPERTS].astype(jnp.int32)
    g_pairs, c_pairs, (t_exp, t_act) = _pair_lists(cnt, tm=tm, n_rows=n_rows)
    xs = _gather_call(g_pairs, rank_t, hn, xs_init, tb=tb, name=f"moe_gather_{tag}")
    ys = _experts_call(t_exp, t_act, xs, w1, w3, w2, slot, tm=tm, tf=tf, name=f"moe_experts_{tag}")
    out = _combine_call(c_pairs, xall, rank_c, comb, ys, fg, tb=tb, final_norm=final_norm,
                        name=f"moe_combine_{tag}")
    return out, xs


def _pad_heads(a, axis):
    axis = axis % a.ndim
    shp = a.shape
    a = a.reshape(shp[:axis] + (NH, DH) + shp[axis + 1:])
    padw = [(0, 0)] * a.ndim
    padw[axis + 1] = (0, HP - DH)
    a = jnp.pad(a, padw)
    return a.reshape(shp[:axis] + (HW,) + shp[axis + 1:])


def _unpad_heads(a, axis):
    axis = axis % a.ndim
    shp = a.shape
    a = a.reshape(shp[:axis] + (NH, HP) + shp[axis + 1:])
    a = lax.slice_in_dim(a, 0, DH, axis=axis + 1)
    return a.reshape(shp[:axis] + (NH * DH,) + shp[axis + 1:])


_M_W = NH * DH
SRC_MQ, SRC_MK, SRC_MV, SRC_MO = 0, _M_W, 2 * _M_W, 3 * _M_W
SRC_MI = 4 * _M_W
SRC_MF = SRC_MI + NH
SRC_PU = SRC_MF + NH
SRC_HQ = SRC_PU + P_W
SRC_HF = SRC_HQ + HW
SRC_HI = SRC_HF + HW
SRC_HG = SRC_HI + _M_W
D_IN = SRC_HG + _M_W


def _w_in_relayout_kernel(w_ref, o32_ref, o16_ref):
    rows = w_ref.shape[0]

    def put(dst, val):
        o32_ref[:, dst:dst + val.shape[1]] = val
        o16_ref[:, dst:dst + val.shape[1]] = val.astype(BF16)

    for src, dst in ((SRC_MQ, OFF_MQ), (SRC_MK, OFF_MK), (SRC_MV, OFF_MV), (SRC_MO, OFF_MO),
                     (SRC_HI, OFF_HI), (SRC_HG, OFF_HG)):
        for h in range(NH):
            put(dst + HP * h, w_ref[:, src + DH * h:src + DH * (h + 1)])
            put(dst + HP * h + DH, jnp.zeros((rows, HP - DH), F32))
    put(OFF_G, w_ref[:, SRC_MI:SRC_MI + NH])
    put(OFF_G + NH, w_ref[:, SRC_MF:SRC_MF + NH])
    put(OFF_G + 2 * NH, jnp.zeros((rows, HP - 2 * NH), F32))
    put(OFF_PU, w_ref[:, SRC_PU:SRC_PU + P_W])
    put(OFF_HQ, w_ref[:, SRC_HQ:SRC_HQ + HW])
    put(OFF_HF, w_ref[:, SRC_HF:SRC_HF + HW])


def _w_in_relayout(w_in, *, tr=256):
    d, k, n = w_in.shape
    assert n == D_IN and k % tr == 0
    return pl.pallas_call(
        _w_in_relayout_kernel,
        grid=(d, k // tr),
        in_specs=[pl.BlockSpec((None, tr, n), lambda l, i: (l, i, 0))],
        out_specs=[pl.BlockSpec((None, tr, D_INP), lambda l, i: (l, i, 0)),
                   pl.BlockSpec((None, tr, D_INP), lambda l, i: (l, i, 0))],
        out_shape=[jax.ShapeDtypeStruct((d, k, D_INP), F32), jax.ShapeDtypeStruct((d, k, D_INP), BF16)],
        compiler_params=pltpu.CompilerParams(dimension_semantics=("arbitrary", "arbitrary"),
                                             vmem_limit_bytes=VMEM_LIMIT),
        name="w_in_relayout",
    )(w_in)


def _prep_weights(norm1_g, w_in, b_igate, b_fgate, conv_w, mlstm_norm_g, pool_w, pool_scale, lb_logits,
                  hgrn_norm_g, w_out):
    m_w = _M_W
    w_in_p, w_in_b = _w_in_relayout(w_in)
    w_out_p = jnp.concatenate([_pad_heads(w_out[:, 0:m_w], 1), w_out[:, m_w:m_w + P_W],
                               _pad_heads(w_out[:, m_w + P_W:], 1)], axis=1)
    gbias = jnp.pad(jnp.concatenate([b_igate, b_fgate], axis=-1), ((0, 0), (0, HP - 2 * NH)))[:, None, :]
    conv_p = jnp.concatenate([_pad_heads(conv_w[..., 0:m_w], -1), _pad_heads(conv_w[..., m_w:], -1)], axis=-1)
    eye = jnp.eye(len(POOL_WINDOWS), dtype=F32)
    pool_bd = jnp.einsum('lgce,gh->lgche', pool_w, eye).reshape(DEPTH, P_W, P_W)
    return {
        "g1": norm1_g[:, None, :], "w_in": w_in_b, "w_in_f32": w_in_p, "gbias": gbias,
        "conv_w": conv_p, "mnorm": _pad_heads(mlstm_norm_g, -1)[:, None, :],
        "pool_w": pool_bd.astype(BF16), "pool_w_f32": pool_bd,
        "pool_scale": pool_scale[:, None, :], "lb_logits": lb_logits,
        "hnorm": _pad_heads(hgrn_norm_g, -1)[:, None, :], "w_out": w_out_p.astype(BF16), "w_out_f32": w_out_p,
    }


def _states_to_kernel(C, n, m, conv, pool, S):
    nb = C.shape[0]
    caug = jnp.concatenate([C, n[..., None]], axis=-1)
    caug = jnp.pad(caug, ((0, 0), (0, 0), (0, HP - DH), (0, HP - DH - 1)))
    mk = jnp.pad(jnp.broadcast_to(m[:, :, None], (nb, NH, HP)), ((0, 0), (0, 8 - NH), (0, 0)))
    m_w = NH * DH
    convk = jnp.concatenate([_pad_heads(conv[..., 0:m_w], -1), _pad_heads(conv[..., m_w:], -1)], axis=-1)
    convk = jnp.pad(convk, ((0, 0), (CONV_HDR - 3, 0), (0, 0)))
    poolk = jnp.pad(pool, ((0, 0), (1, 0), (0, 0)))
    sk = jnp.pad(jnp.swapaxes(S, -1, -2), ((0, 0), (0, 0), (0, HP - DH), (0, 0)))
    return caug, mk, convk, poolk, sk


def _states_from_kernel(st):
    caug, mk, convk, poolk, sk = st
    C = caug[:, :, 0:DH, 0:DH]
    n = caug[:, :, 0:DH, DH]
    m = mk[:, 0:NH, 0]
    conv = convk[:, CONV_HDR - 3:, :]
    conv = jnp.concatenate([_unpad_heads(conv[..., 0:HW], -1), _unpad_heads(conv[..., HW:], -1)], axis=-1)
    pool = poolk[:, 1:, :]
    S = jnp.swapaxes(sk[:, :, 0:DH, :], -1, -2)
    return C, n, m, conv, pool, S


def _pick_tile(n, candidates):
    for c in candidates:
        if n % c == 0:
            return c
    raise ValueError(f"no row tile for {n}")


def kernel(x_prompt, x_sample, state_mlstm_C, state_mlstm_n, state_mlstm_m, state_mlstm_conv, state_pool,
           state_hgrn, meta_tokens, norm1_g, norm2_g, final_g, w_in, b_igate, b_fgate, conv_w, mlstm_norm_g,
           pool_w, pool_scale, lb_logits, hgrn_norm_g, w_out, ffn_w1, ffn_w3, ffn_w2, router_w, moe_w1,
           moe_w3, moe_w2):
    B, T, _ = x_prompt.shape
    SB, ST, _ = x_sample.shape
    w = _prep_weights(norm1_g, w_in, b_igate, b_fgate, conv_w, mlstm_norm_g, pool_w, pool_scale, lb_logits,
                      hgrn_norm_g, w_out)
    ffn_w1b, ffn_w3b, ffn_w2b = ffn_w1.astype(BF16), ffn_w3.astype(BF16), ffn_w2.astype(BF16)
    router_p = jnp.pad(router_w, ((0, 0), (0, 0), (0, HP - N_EXPERTS)))

    n_main = B * T
    off_s = n_main
    off_m = off_s + SB * ST
    assert SB * ST == TAIL and n_main % TAIL == 0
    n_tok = off_m + META_TOKENS
    tile = 1280 if n_main >= 16384 else 256
    expert_tile = 1024 if n_main >= 16384 else 256
    n_pad = -(-n_tok // tile) * tile
    xall = jnp.concatenate([x_prompt.reshape(n_main, D_MODEL), x_sample.reshape(SB * ST, D_MODEL),
                            meta_tokens.astype(F32), jnp.zeros((n_pad - n_tok, D_MODEL), F32)], axis=0)
    tv_main = _pick_tile(T, (256, 128))

    zero_states = (jnp.zeros((1, NH, HP, HP), F32), jnp.zeros((1, 8, HP), F32),
                   jnp.zeros((1, CONV_HDR, 2 * HW), F32), jnp.zeros((1, 16, P_W), F32),
                   jnp.zeros((1, NH, HP, HP), F32))
    p_states, s_states = [], []
    xs_buf = None
    st_in = jax.vmap(_states_to_kernel)(state_mlstm_C, state_mlstm_n, state_mlstm_m, state_mlstm_conv,
                                        state_pool, state_hgrn)
    tail_blocks = tuple((b * T + T - TAIL) // TAIL for b in range(B)) + (off_s // TAIL,)
    for l in range(DEPTH):
        precise = l < PRECISE_LAYERS
        xall, st_meta = _mixer_call(xall, zero_states, True, w, l, row_off=off_m, seq_stride=META_TOKENS, nb=1,
                                    seq=META_TOKENS, tv=META_TOKENS, pos0=0, precise=False,
                                    name=f"mixer_meta_{l}")
        if precise:
            xall, st_p = _mixer_call(xall, st_meta, True, w, l, row_off=0, seq_stride=T, nb=B, seq=T - TAIL,
                                     tv=tv_main, pos0=META_TOKENS, precise=False, name=f"mixer_prompt_{l}")
            xall, st_p = _mixer_call(xall, st_p, False, w, l, row_off=T - TAIL, seq_stride=T, nb=B, seq=TAIL,
                                     tv=TAIL, pos0=META_TOKENS + T - TAIL, precise=True,
                                     name=f"mixer_prompt_tail_{l}")
        else:
            xall, st_p = _mixer_call(xall, st_meta, True, w, l, row_off=0, seq_stride=T, nb=B, seq=T,
                                     tv=tv_main, pos0=META_TOKENS, precise=False, name=f"mixer_prompt_{l}")
        xall, st_s = _mixer_call(xall, st_in, False, w, l, row_off=off_s, seq_stride=ST, nb=SB, seq=ST, tv=ST,
                                 pos0=META_TOKENS + PAST_LEN, precise=True, name=f"mixer_sample_{l}",
                                 stacked_states=True)
        p_states.append(st_p)
        s_states.append(st_s)
        i = l // 2
        if l % 2 == 0:
            f32_blocks = tail_blocks if precise else tail_blocks[-1:]
            rows = _ffn_rows_f32_call(xall, f32_blocks, norm2_g[l][None, :], ffn_w1, ffn_w3, ffn_w2, i,
                                      tm=TAIL, tf=512, name=f"ffn_rows_f32_{l}")
            xall = _ffn_call(xall, norm2_g[l][None, :], ffn_w1b, ffn_w3b, ffn_w2b, i,
                             tm=tile, tf=512, name=f"ffn_{l}")
            for j, blk in enumerate(f32_blocks):
                xall = lax.dynamic_update_slice(xall, rows[j * TAIL:(j + 1) * TAIL], (blk * TAIL, 0))
        else:
            xall, xs_buf = _moe_layer(xall, norm2_g[l][None, :], router_p[i], moe_w1, moe_w3, moe_w2, i,
                                      final_g[None, :], xs_buf, tb=tile, tm=expert_tile, tf=512,
                                      final_norm=(l == DEPTH - 1), tag=str(l))
    y_prompt = xall[0:n_main].reshape(B, T, D_MODEL)
    y_sample = xall[off_s:off_m].reshape(SB, ST, D_MODEL)
    p_out = jax.vmap(_states_from_kernel)(tuple(jnp.stack([s[j] for s in p_states], axis=0) for j in range(5)))
    s_out = jax.vmap(_states_from_kernel)(tuple(jnp.stack([s[j] for s in s_states], axis=0) for j in range(5)))
    return (y_prompt, y_sample) + tuple(p_out) + tuple(s_out)
```

```python
import functools

import jax
import jax.numpy as jnp
from jax import lax
from jax.experimental import pallas as pl
from jax.experimental.pallas import tpu as pltpu

F32 = jnp.float32
BF16 = jnp.bfloat16

D_MODEL = 1024
DEPTH = 4
META_TOKENS = 16
PAST_LEN = 4096
EPS = 1e-6
NH = 4
DH = 96
HP = 128
HW = NH * HP
P_W = 256
P_GW = 64
POOL_WINDOWS = (2, 4, 8, 16)
D_FF = 3584
N_EXPERTS = 8
NEG = -1e30

OFF_MQ, OFF_MK, OFF_MV, OFF_MO = 0, 512, 1024, 1536
OFF_G = 2048
OFF_PU = 2176
OFF_HQ, OFF_HF, OFF_HI, OFF_HG = 2432, 2944, 3456, 3968
D_INP = 4480
OFF_YM, OFF_YP, OFF_YH = 0, 512, 768
D_MIXP = 1280

MIN_ROWS = 128
M_CHUNK = 256
H_CHUNK = 128
H_SUB = 32
H_SAFE_LOG_DECAY = -60.0
CONV_HDR = 8
POOL_HDR = 32

TAIL = 256
PRECISE_LAYERS = 2

VMEM_LIMIT = 60 * 1024 * 1024


def _sigmoid(x):
    return 1.0 / (1.0 + jnp.exp(-x))


X3 = "bf16x3"


def _split_bf16(a):
    hi = a.astype(BF16)
    return hi, (a - hi.astype(F32)).astype(BF16)


def _dg(a, b, dims, precision=None):
    if precision == X3:
        ah, al = _split_bf16(a)
        bh, bl = _split_bf16(b)
        return (lax.dot_general(ah, bh, dims, preferred_element_type=F32)
                + lax.dot_general(al, bh, dims, preferred_element_type=F32)
                + lax.dot_general(ah, bl, dims, preferred_element_type=F32))
    return lax.dot_general(a, b, dims, preferred_element_type=F32, precision=precision)


def _dot(a, b, precision=None):
    return _dg(a, b, (((1,), (0,)), ((), ())), precision)


def _dot_nt(a, b, precision=None):
    return _dg(a, b, (((1,), (1,)), ((), ())), precision)


def _dot_tn(a, b, precision=None):
    return _dg(a, b, (((0,), (0,)), ((), ())), precision)


def _cumsum_rows(tril_bf, x, precise=False):
    n = x.shape[1]
    hi = x.astype(BF16)
    rest = x - hi.astype(F32)
    lo = rest.astype(BF16)
    parts = [hi, lo] + ([(rest - lo.astype(F32)).astype(BF16)] if precise else [])
    both = _dot(tril_bf, jnp.concatenate(parts, axis=1))
    out = both[:, :n] + both[:, n:2 * n]
    return out + both[:, 2 * n:] if precise else out


def _bcast_rows(row, n):
    return jnp.broadcast_to(row, (n, row.shape[1]))


def _mixer_kernel(x_ref, c0_ref, m0_ref, conv0_ref, pool0_ref, s0_ref,
                  g1_ref, win_ref, gb_ref, cw_ref, mng_ref, pw_ref, ps_ref, lbl_ref, hng_ref, wout_ref,
                  xo_ref, cf_ref, mf_ref, convf_ref, poolf_ref, sf_ref,
                  proj_s, qk_s, u_s, s2_s, s4_s, s8_s, hk_s, gate_s, y_s, gx_s, st_old_s, c_s, m_s, st_s,
                  *, layer, tv, lb, pos0, n_t, precise, m_chunk, h_chunk):
    pad = lb - tv
    t = pl.program_id(1)
    prec = X3 if precise else None
    dot = functools.partial(_dot, precision=prec)
    dot_nt = functools.partial(_dot_nt, precision=prec)
    dot_tn = functools.partial(_dot_tn, precision=prec)

    def mm(a):
        return a if precise else a.astype(BF16)

    @pl.when(t == 0)
    def _init():
        c_s[...] = c0_ref[...]
        m_s[...] = m0_ref[...]
        st_s[...] = s0_ref[...]
        qk_s[...] = jnp.zeros(qk_s.shape, F32)
        qk_s[pad:pad + CONV_HDR, :] = conv0_ref[...]
        u_s[...] = jnp.zeros(u_s.shape, F32)
        u_s[pad + 16:pad + 32, :] = pool0_ref[...]
        s2_s[0:POOL_HDR, :] = jnp.zeros((POOL_HDR, P_W), F32)
        s4_s[0:POOL_HDR, :] = jnp.zeros((POOL_HDR, P_W), F32)
        s8_s[0:POOL_HDR, :] = jnp.zeros((POOL_HDR, P_W), F32)
        if pad:
            proj_s[0:pad, :] = jnp.zeros((pad, D_INP), F32)

    x = x_ref[...]
    ms = jnp.mean(x * x, axis=-1, keepdims=True)
    hn = mm(x * lax.rsqrt(ms + EPS) * g1_ref[...])
    qk_s[CONV_HDR + pad:CONV_HDR + lb, :] = dot(hn, win_ref[:, 0:OFF_MV])
    proj_s[pad:lb, OFF_MV:OFF_PU] = dot(hn, win_ref[:, OFF_MV:OFF_PU])
    u_s[POOL_HDR + pad:POOL_HDR + lb, :] = dot(hn, win_ref[:, OFF_PU:OFF_HQ])
    proj_s[pad:lb, OFF_HQ:D_INP] = dot(hn, win_ref[:, OFF_HQ:D_INP])

    row = lax.broadcasted_iota(jnp.int32, (lb, HP), 0)
    lane = lax.broadcasted_iota(jnp.int32, (lb, HP), 1)

    acc = qk_s[5:5 + lb, :] * cw_ref[0:1, :]
    for j in range(1, 4):
        acc = acc + qk_s[5 + j:5 + j + lb, :] * cw_ref[j:j + 1, :]
    qk = acc * _sigmoid(acc)
    proj_s[:, OFF_MQ:OFF_MK] = qk[:, 0:HW]
    proj_s[:, OFF_MK:OFF_MV] = qk[:, HW:2 * HW] * (DH ** -0.5)
    conv_tail = qk_s[lb:lb + CONV_HDR, :]
    qk_s[0:CONV_HDR, :] = conv_tail

    gpre = proj_s[:, OFF_G:OFF_G + HP] + gb_ref[...]
    lsig = jnp.minimum(gpre, 0.0) - jnp.log(1.0 + jnp.exp(-jnp.abs(gpre)))
    gates = jnp.where(lane < NH, gpre, jnp.where(lane < 2 * NH, lsig, 0.0))
    if pad:
        gates = jnp.where(row >= pad, gates, jnp.where(lane < NH, NEG, 0.0))
    gate_s[...] = gates

    n_ext = lb + 16
    s2_s[16:16 + n_ext, :] = u_s[16:16 + n_ext, :] + u_s[15:15 + n_ext, :]
    s4_s[16:16 + n_ext, :] = s2_s[16:16 + n_ext, :] + s2_s[14:14 + n_ext, :]
    s8_s[16:16 + n_ext, :] = s4_s[16:16 + n_ext, :] + s4_s[12:12 + n_ext, :]
    u_cur = u_s[POOL_HDR:POOL_HDR + lb, :]
    w2 = s2_s[POOL_HDR:POOL_HDR + lb, :]
    w4 = s4_s[POOL_HDR:POOL_HDR + lb, :]
    w8 = s8_s[POOL_HDR:POOL_HDR + lb, :]
    w16 = w8 + s8_s[POOL_HDR - 8:POOL_HDR - 8 + lb, :]
    lane_p = lax.broadcasted_iota(jnp.int32, (lb, P_W), 1)
    wsum = jnp.where(lane_p < P_GW, w2, jnp.where(lane_p < 2 * P_GW, w4, jnp.where(lane_p < 3 * P_GW, w8, w16)))
    if pos0 >= POOL_WINDOWS[-1] - 1:
        inv = jnp.where(lane_p < P_GW, 0.5, jnp.where(lane_p < 2 * P_GW, 0.25,
                                                       jnp.where(lane_p < 3 * P_GW, 0.125, 0.0625)))
        mean = wsum * inv
    else:
        row_p = lax.broadcasted_iota(jnp.int32, (lb, P_W), 0)
        posn = (row_p + (pos0 + 1 - pad + t * tv)).astype(F32)
        wlen = jnp.where(lane_p < P_GW, 2.0, jnp.where(lane_p < 2 * P_GW, 4.0,
                                                        jnp.where(lane_p < 3 * P_GW, 8.0, 16.0)))
        mean = wsum / jnp.maximum(jnp.minimum(wlen, posn), 1.0)
    pooled = mm(mean - u_cur)
    y_s[:, OFF_YP:OFF_YH] = mm(dot(pooled, pw_ref[...]) * ps_ref[...])
    pool_tail = u_s[lb:lb + POOL_HDR, :]
    u_s[0:POOL_HDR, :] = pool_tail

    lbl = lbl_ref[...]
    e = jnp.exp(lbl - jnp.max(lbl, axis=0, keepdims=True))
    p = e / jnp.sum(e, axis=0, keepdims=True)
    lbv = jnp.sum(p[0:layer + 1, :], axis=0, keepdims=True) - p[0:1, :]
    hq = proj_s[:, OFF_HQ:OFF_HF]
    hf = proj_s[:, OFF_HF:OFF_HI]
    fg = lbv + (1.0 - lbv) * _sigmoid(hf)
    kh = 1.0 - fg
    lg = jnp.log(fg)
    if pad:
        row_h = lax.broadcasted_iota(jnp.int32, (lb, HW), 0)
        kh = jnp.where(row_h >= pad, kh, 0.0)
        lg = jnp.where(row_h >= pad, lg, 0.0)
    proj_s[:, OFF_HQ:OFF_HF] = hq * _sigmoid(hq)
    proj_s[:, OFF_HF:OFF_HI] = lg
    hk_s[...] = kh

    rr = lax.broadcasted_iota(jnp.int32, (m_chunk, m_chunk), 0)
    cc = lax.broadcasted_iota(jnp.int32, (m_chunk, m_chunk), 1)
    causal = rr >= cc
    tril_m = jnp.where(causal, 1.0, 0.0).astype(BF16)
    lane_c = lax.broadcasted_iota(jnp.int32, (m_chunk, HP), 1)

    for c in range(lb // m_chunk):
        rows = slice(c * m_chunk, (c + 1) * m_chunk)
        gt = gate_s[rows, :]
        bcum = _cumsum_rows(tril_m, jnp.where(lane_c >= NH, gt, 0.0), precise)
        cg = gt - pltpu.roll(bcum, HP - NH, axis=1)
        cg_t = cg.T
        for h in range(NH):
            q = proj_s[rows, OFF_MQ + h * HP:OFF_MQ + (h + 1) * HP]
            k = proj_s[rows, OFF_MK + h * HP:OFF_MK + (h + 1) * HP]
            v = proj_s[rows, OFF_MV + h * HP:OFF_MV + (h + 1) * HP]
            c_row = cg_t[h:h + 1, :]
            c_col = cg[:, h:h + 1]
            b_col = bcum[:, NH + h:NH + h + 1]
            m_prev = m_s[h:h + 1, 0:1]
            mx = jnp.maximum(jnp.max(jnp.where(causal, c_row, NEG), axis=1, keepdims=True), m_prev)
            w = jnp.exp(jnp.where(causal, c_row - mx, NEG))
            w_int = jnp.exp(m_prev - mx)
            mx_last = mx[m_chunk - 1:m_chunk, :]
            s = dot_nt(mm(q), mm(k)) * w
            vaug = jnp.where(lane_c == DH, 1.0, v)
            caug = c_s[h]
            lhs = mm(jnp.concatenate([s, q * w_int], axis=1))
            rhs = mm(jnp.concatenate([vaug, caug], axis=0))
            nd = dot(lhs, rhs)
            den = nd[:, DH:DH + 1]
            rden = 1.0 / jnp.maximum(jnp.abs(den), jnp.exp(-(b_col + mx)))
            wl = jnp.exp(c_col - mx_last)
            decay = jnp.exp(m_prev - mx_last)
            c_s[h] = decay * caug + dot_tn(mm(k * wl), mm(vaug))
            m_s[h:h + 1, :] = jnp.broadcast_to(b_col[m_chunk - 1:m_chunk, :] + mx_last, (1, HP))
            mo = proj_s[rows, OFF_MO + h * HP:OFF_MO + (h + 1) * HP]
            z = jnp.where(lane_c < DH, nd * _sigmoid(mo), 0.0)
            ssq = jnp.sum(z * z, axis=1, keepdims=True) * (1.0 / DH)
            fac = rden * lax.rsqrt(rden * rden * ssq + EPS)
            y_s[rows, OFF_YM + h * HP:OFF_YM + (h + 1) * HP] = mm(z * fac * mng_ref[:, h * HP:(h + 1) * HP])

    r64 = lax.broadcasted_iota(jnp.int32, (h_chunk, h_chunk), 0)
    c64 = lax.broadcasted_iota(jnp.int32, (h_chunk, h_chunk), 1)
    tril_h = jnp.where(r64 >= c64, 1.0, 0.0).astype(BF16)
    bdiff = r64 // H_SUB - c64 // H_SUB
    mask_intra = (bdiff == 0) & (r64 >= c64)
    n_sub = h_chunk // H_SUB

    def hgrn_block(c, factored):
        rows = slice(c * h_chunk, (c + 1) * h_chunk)
        g = _cumsum_rows(tril_h, proj_s[rows, OFF_HF:OFF_HI], precise)
        qh = proj_s[rows, OFF_HQ:OFF_HF]
        khc = hk_s[rows, :]
        if not factored:
            gx_s[...] = g
        bnd = [jnp.zeros((1, HW), F32)] + [g[(j + 1) * H_SUB - 1:(j + 1) * H_SUB, :] for j in range(n_sub)]
        g_start = jnp.concatenate([_bcast_rows(bnd[j], H_SUB) for j in range(n_sub)], axis=0)
        g_end = jnp.concatenate([_bcast_rows(bnd[j + 1], H_SUB) for j in range(n_sub)], axis=0)
        g_last = bnd[n_sub]
        qt = qh * jnp.exp(g - g_start)
        khat = khc * jnp.exp(g_end - g)
        kbar = khc * jnp.exp(g_start - g)
        qg = qh * jnp.exp(g)
        kend = khc * jnp.exp(g_last - g)
        dsub = [jnp.exp(bnd[j + 1] - bnd[j]) for j in range(n_sub - 1)]
        ones = jnp.ones((H_SUB, HW), F32)
        qlev = [qt]
        for d in range(1, n_sub - 1):
            fac = jnp.concatenate([ones] * d + [_bcast_rows(dsub[j - d], H_SUB) for j in range(d, n_sub)], axis=0)
            qlev.append(qlev[-1] * fac)
        for h in range(NH):
            sl = slice(h * HP, (h + 1) * HP)
            if factored:
                lhs = mm(jnp.concatenate([ql[:, sl] for ql in qlev], axis=0))
                inter = dot_nt(lhs, mm(khat[:, sl]))
                intra = dot_nt(mm(qt[:, sl]), mm(kbar[:, sl]))
                att = jnp.where(mask_intra, intra, 0.0)
                for d in range(n_sub - 1):
                    att = att + jnp.where(bdiff == d + 1, inter[d * h_chunk:(d + 1) * h_chunk, :], 0.0)
            else:
                def cols(j, att_acc, sl=sl, gh=g[:, sl], qhh=qh[:, sl]):
                    r0 = pl.multiple_of(j * 8, 8)
                    g8 = gx_s[pl.ds(r0, 8), sl]
                    k8 = hk_s[pl.ds(c * h_chunk + r0, 8), sl]
                    for r in range(8):
                        wgt = jnp.exp(jnp.minimum(gh - g8[r:r + 1, :], 0.0))
                        pcol = jnp.sum(qhh * wgt * k8[r:r + 1, :], axis=1, keepdims=True)
                        att_acc = att_acc + jnp.where((c64 == r0 + r) & (r64 >= r0 + r), pcol, 0.0)
                    return att_acc
                att = lax.fori_loop(0, h_chunk // 8, cols, jnp.zeros((h_chunk, h_chunk), F32))
            iv = mm(proj_s[rows, OFF_HI + h * HP:OFF_HI + (h + 1) * HP])
            st = st_s[h]
            o = dot(mm(att), iv) + dot_nt(mm(qg[:, sl]), mm(st))
            st_s[h] = st * jnp.exp(g_last[:, sl]) + dot_tn(iv, mm(kend[:, sl]))
            msq = jnp.sum(o * o, axis=1, keepdims=True) * (1.0 / DH)
            on = o * lax.rsqrt(msq + EPS) * hng_ref[:, sl]
            hg = proj_s[rows, OFF_HG + h * HP:OFF_HG + (h + 1) * HP]
            y_s[rows, OFF_YH + h * HP:OFF_YH + (h + 1) * HP] = mm(on * (hg * _sigmoid(hg)))

    st_old_s[...] = st_s[...]
    for c in range(lb // h_chunk):
        hgrn_block(c, True)

    def out_proj():
        out = dot(y_s[...], wout_ref[...])
        xo_ref[...] = x + out[pad:lb, :]

    out_proj()
    lg_sub = proj_s[:, OFF_HF:OFF_HI].reshape(lb // H_SUB, H_SUB, HW)
    factor_ok = jnp.min(jnp.sum(lg_sub, axis=1)) > H_SAFE_LOG_DECAY

    @pl.when(jnp.logical_not(factor_ok))
    def _redo_direct():
        st_s[...] = st_old_s[...]
        for c in range(lb // h_chunk):
            hgrn_block(c, False)
        out_proj()

    @pl.when(t == n_t - 1)
    def _final():
        cf_ref[...] = c_s[...]
        mf_ref[...] = m_s[...]
        convf_ref[...] = conv_tail
        poolf_ref[...] = pool_tail[16:32, :]
        sf_ref[...] = st_s[...]


def _mixer_call(xall, states, shared_init, w, layer, *, row_off, seq_stride, nb, seq, tv, pos0, precise, name,
                stacked_states=False):
    lb = max(tv, MIN_ROWS)
    m_chunk, h_chunk = min(lb, M_CHUNK), min(lb, H_CHUNK)
    n_t = seq // tv
    assert row_off % tv == 0 and seq_stride % tv == 0 and seq % tv == 0
    blk0, blk_stride = row_off // tv, seq_stride // tv
    c0, m0, conv0, pool0, s0 = states
    lead_blk = (None,) if stacked_states else ()
    lead_idx = (layer,) if stacked_states else ()

    def x_map(b, t):
        return (blk0 + b * blk_stride + t, 0)

    def st_map(b, t):
        return lead_idx + (0 if shared_init else b, 0, 0, 0)

    def st_map3(b, t):
        return lead_idx + (0 if shared_init else b, 0, 0)

    def layer_spec(shape):
        return pl.BlockSpec((None,) + shape, lambda b, t: (layer,) + (0,) * len(shape),
                            pipeline_mode=pl.Buffered(1))

    in_specs = [
        pl.BlockSpec((tv, D_MODEL), x_map),
        pl.BlockSpec(lead_blk + (None, NH, HP, HP), st_map),
        pl.BlockSpec(lead_blk + (None, 8, HP), st_map3),
        pl.BlockSpec(lead_blk + (None, CONV_HDR, 2 * HW), st_map3),
        pl.BlockSpec(lead_blk + (None, 16, P_W), st_map3),
        pl.BlockSpec(lead_blk + (None, NH, HP, HP), st_map),
        layer_spec((1, D_MODEL)),
        layer_spec((D_MODEL, D_INP)),
        layer_spec((1, HP)),
        layer_spec((4, 2 * HW)),
        layer_spec((1, HW)),
        layer_spec((P_W, P_W)),
        layer_spec((1, P_W)),
        pl.BlockSpec((DEPTH, HW), lambda b, t: (0, 0), pipeline_mode=pl.Buffered(1)),
        layer_spec((1, HW)),
        layer_spec((D_MIXP, D_MODEL)),
    ]
    out_specs = [
        pl.BlockSpec((tv, D_MODEL), x_map),
        pl.BlockSpec((None, NH, HP, HP), lambda b, t: (b, 0, 0, 0)),
        pl.BlockSpec((None, 8, HP), lambda b, t: (b, 0, 0)),
        pl.BlockSpec((None, CONV_HDR, 2 * HW), lambda b, t: (b, 0, 0)),
        pl.BlockSpec((None, 16, P_W), lambda b, t: (b, 0, 0)),
        pl.BlockSpec((None, NH, HP, HP), lambda b, t: (b, 0, 0, 0)),
    ]
    out_shape = [
        jax.ShapeDtypeStruct(xall.shape, F32),
        jax.ShapeDtypeStruct((nb, NH, HP, HP), F32),
        jax.ShapeDtypeStruct((nb, 8, HP), F32),
        jax.ShapeDtypeStruct((nb, CONV_HDR, 2 * HW), F32),
        jax.ShapeDtypeStruct((nb, 16, P_W), F32),
        jax.ShapeDtypeStruct((nb, NH, HP, HP), F32),
    ]
    scratch = [
        pltpu.VMEM((lb, D_INP), F32),
        pltpu.VMEM((CONV_HDR + lb, 2 * HW), F32),
        pltpu.VMEM((POOL_HDR + lb, P_W), F32),
        pltpu.VMEM((POOL_HDR + lb, P_W), F32),
        pltpu.VMEM((POOL_HDR + lb, P_W), F32),
        pltpu.VMEM((POOL_HDR + lb, P_W), F32),
        pltpu.VMEM((lb, HW), F32),
        pltpu.VMEM((lb, HP), F32),
        pltpu.VMEM((lb, D_MIXP), F32 if precise else BF16),
        pltpu.VMEM((h_chunk, HW), F32),
        pltpu.VMEM((NH, HP, HP), F32),
        pltpu.VMEM((NH, HP, HP), F32),
        pltpu.VMEM((8, HP), F32),
        pltpu.VMEM((NH, HP, HP), F32),
    ]
    sfx = "_f32" if precise else ""
    kern = functools.partial(_mixer_kernel, layer=layer, tv=tv, lb=lb, pos0=pos0, n_t=n_t, precise=precise,
                             m_chunk=m_chunk, h_chunk=h_chunk)
    outs = pl.pallas_call(
        kern,
        grid=(nb, n_t),
        in_specs=in_specs,
        out_specs=out_specs,
        out_shape=out_shape,
        scratch_shapes=scratch,
        input_output_aliases={0: 0},
        compiler_params=pltpu.CompilerParams(dimension_semantics=("arbitrary", "arbitrary"),
                                             vmem_limit_bytes=VMEM_LIMIT),
        name=name,
    )(xall, c0, m0, conv0, pool0, s0,
      w["g1"], w["w_in" + sfx], w["gbias"], w["conv_w"], w["mnorm"],
      w["pool_w" + sfx], w["pool_scale"], w["lb_logits"], w["hnorm"], w["w_out" + sfx])
    return outs[0], tuple(outs[1:])


def _ffn_kernel(x_ref, g_ref, w1_ref, w3_ref, w2_ref, o_ref, hn_s, *, precise):
    f = pl.program_id(1)
    prec = X3 if precise else None

    @pl.when(f == 0)
    def _start():
        x = x_ref[...]
        ms = jnp.mean(x * x, axis=-1, keepdims=True)
        hn_s[...] = (x * lax.rsqrt(ms + EPS) * g_ref[...]).astype(hn_s.dtype)
        o_ref[...] = x

    hn = hn_s[...]
    h1 = _dot(hn, w1_ref[...], prec)
    h3 = _dot(hn, w3_ref[...], prec)
    a = (h1 * _sigmoid(h1) * h3).astype(hn_s.dtype)
    o_ref[...] += _dot(a, w2_ref[...], prec)


def _ffn_rows_f32_call(xall, blocks, g, w1, w3, w2, slot, *, tm, tf, name):
    n_f = D_FF // tf

    def x_map(i, f):
        idx = blocks[-1]
        for j in range(len(blocks) - 2, -1, -1):
            idx = jnp.where(i == j, blocks[j], idx)
        return (idx, 0)

    return pl.pallas_call(
        functools.partial(_ffn_kernel, precise=True),
        grid=(len(blocks), n_f),
        in_specs=[
            pl.BlockSpec((tm, D_MODEL), x_map),
            pl.BlockSpec((1, D_MODEL), lambda i, f: (0, 0)),
            pl.BlockSpec((None, D_MODEL, tf), lambda i, f: (slot, 0, f)),
            pl.BlockSpec((None, D_MODEL, tf), lambda i, f: (slot, 0, f)),
            pl.BlockSpec((None, tf, D_MODEL), lambda i, f: (slot, f, 0)),
        ],
        out_specs=pl.BlockSpec((tm, D_MODEL), lambda i, f: (i, 0)),
        out_shape=jax.ShapeDtypeStruct((len(blocks) * tm, D_MODEL), F32),
        scratch_shapes=[pltpu.VMEM((tm, D_MODEL), F32)],
        compiler_params=pltpu.CompilerParams(dimension_semantics=("arbitrary", "arbitrary"),
                                             vmem_limit_bytes=VMEM_LIMIT),
        name=name,
    )(xall, g, w1, w3, w2)


def _ffn_call(xall, g, w1, w3, w2, slot, *, tm, tf, name):
    n = xall.shape[0]
    n_f = D_FF // tf
    return pl.pallas_call(
        functools.partial(_ffn_kernel, precise=False),
        grid=(n // tm, n_f),
        in_specs=[
            pl.BlockSpec((tm, D_MODEL), lambda i, f: (i, 0)),
            pl.BlockSpec((1, D_MODEL), lambda i, f: (0, 0)),
            pl.BlockSpec((None, D_MODEL, tf), lambda i, f: (slot, 0, f)),
            pl.BlockSpec((None, D_MODEL, tf), lambda i, f: (slot, 0, f)),
            pl.BlockSpec((None, tf, D_MODEL), lambda i, f: (slot, f, 0)),
        ],
        out_specs=pl.BlockSpec((tm, D_MODEL), lambda i, f: (i, 0)),
        out_shape=jax.ShapeDtypeStruct(xall.shape, F32),
        scratch_shapes=[pltpu.VMEM((tm, D_MODEL), BF16)],
        input_output_aliases={0: 0},
        compiler_params=pltpu.CompilerParams(dimension_semantics=("arbitrary", "arbitrary"),
                                             vmem_limit_bytes=VMEM_LIMIT),
        name=name,
    )(xall, g, w1, w3, w2)


GATHER_TILE = 256
NO_SLOT = -1e9


def _route_kernel(x_ref, g_ref, wr_ref, hn_ref, rank_t_ref, rank_c_ref, comb_ref, cnt_ref, *, tb):
    x = x_ref[...]
    ms = jnp.mean(x * x, axis=-1, keepdims=True)
    hn = x * lax.rsqrt(ms + EPS) * g_ref[...]
    hn_ref[...] = hn.astype(BF16)
    logits = _dot(hn, wr_ref[...], X3)
    lane = lax.broadcasted_iota(jnp.int32, (tb, HP), 1).astype(F32)
    lg = jnp.where(lane < N_EXPERTS, logits, NEG)
    v1 = jnp.max(lg, axis=1, keepdims=True)
    i1 = jnp.min(jnp.where(lg == v1, lane, float(HP)), axis=1, keepdims=True)
    mask1 = lane == i1
    lg2 = jnp.where(mask1, NEG, lg)
    v2 = jnp.max(lg2, axis=1, keepdims=True)
    i2 = jnp.min(jnp.where(lg2 == v2, lane, float(HP)), axis=1, keepdims=True)
    mask2 = lane == i2
    ex = jnp.exp(v2 - v1)
    ga = 1.0 / (1.0 + ex)
    comb_ref[...] = jnp.where(mask1, ga, 0.0) + jnp.where(mask2, ex * ga, 0.0)
    sel = mask1 | mask2
    rr = lax.broadcasted_iota(jnp.int32, (tb, tb), 0)
    cc = lax.broadcasted_iota(jnp.int32, (tb, tb), 1)
    tril_strict = jnp.where(rr > cc, 1.0, 0.0).astype(BF16)
    selb = jnp.where(sel, 1.0, 0.0)
    rank = jnp.where(sel, _dot(tril_strict, selb.astype(BF16)), NO_SLOT)
    rank_c_ref[...] = rank
    rank_t_ref[...] = rank.T[0:N_EXPERTS, :]
    cnt_ref[...] = jnp.broadcast_to(jnp.sum(selb, axis=0, keepdims=True), (8, HP))


def _route_call(xall, g, wr, *, tb, name):
    n = xall.shape[0]
    nb = n // tb
    return pl.pallas_call(
        functools.partial(_route_kernel, tb=tb),
        grid=(nb,),
        in_specs=[
            pl.BlockSpec((tb, D_MODEL), lambda i: (i, 0)),
            pl.BlockSpec((1, D_MODEL), lambda i: (0, 0)),
            pl.BlockSpec((D_MODEL, HP), lambda i: (0, 0)),
        ],
        out_specs=[
            pl.BlockSpec((tb, D_MODEL), lambda i: (i, 0)),
            pl.BlockSpec((None, N_EXPERTS, tb), lambda i: (i, 0, 0)),
            pl.BlockSpec((tb, HP), lambda i: (i, 0)),
            pl.BlockSpec((tb, HP), lambda i: (i, 0)),
            pl.BlockSpec((None, 8, HP), lambda i: (i, 0, 0)),
        ],
        out_shape=[
            jax.ShapeDtypeStruct((n, D_MODEL), BF16),
            jax.ShapeDtypeStruct((nb, N_EXPERTS, tb), F32),
            jax.ShapeDtypeStruct((n, HP), F32),
            jax.ShapeDtypeStruct((n, HP), F32),
            jax.ShapeDtypeStruct((nb, 8, HP), F32),
        ],
        compiler_params=pltpu.CompilerParams(dimension_semantics=("arbitrary",), vmem_limit_bytes=VMEM_LIMIT),
        name=name,
    )(xall, g, wr)


def _gather_kernel(tile_ref, blk_ref, exp_ref, off_ref, first_ref, act_ref, rank_t_ref, hn_ref, init_ref, xs_ref,
                   *, tb):
    del tile_ref, blk_ref, init_ref
    p = pl.program_id(0)

    @pl.when(act_ref[p] == 1)
    def _():
        rrow = rank_t_ref[pl.ds(exp_ref[p], 1), :]
        slot = (lax.broadcasted_iota(jnp.int32, (GATHER_TILE, tb), 0) - off_ref[p]).astype(F32)
        onehot = jnp.where(rrow == slot, 1.0, 0.0).astype(BF16)
        val = _dot(onehot, hn_ref[...]).astype(BF16)

        @pl.when(first_ref[p] == 1)
        def _set():
            xs_ref[...] = val

        @pl.when(first_ref[p] == 0)
        def _add():
            xs_ref[...] += val


def _gather_call(pairs, rank_t, hn, init, *, tb, name):
    n_rows = init.shape[0]
    n_pairs = pairs[0].shape[0]
    grid_spec = pltpu.PrefetchScalarGridSpec(
        num_scalar_prefetch=6,
        grid=(n_pairs,),
        in_specs=[
            pl.BlockSpec((None, N_EXPERTS, tb), lambda p, tile, blk, *_: (blk[p], 0, 0)),
            pl.BlockSpec((tb, D_MODEL), lambda p, tile, blk, *_: (blk[p], 0)),
            pl.BlockSpec(memory_space=pl.ANY),
        ],
        out_specs=pl.BlockSpec((GATHER_TILE, D_MODEL), lambda p, tile, *_: (tile[p], 0)),
    )
    return pl.pallas_call(
        functools.partial(_gather_kernel, tb=tb),
        grid_spec=grid_spec,
        out_shape=jax.ShapeDtypeStruct((n_rows, D_MODEL), BF16),
        input_output_aliases={8: 0},
        compiler_params=pltpu.CompilerParams(dimension_semantics=("arbitrary",), vmem_limit_bytes=VMEM_LIMIT),
        name=name,
    )(*pairs, rank_t, hn, init)


def _experts_kernel(exp_ref, act_ref, xs_ref, w1_ref, w3_ref, w2_ref, ys_ref, acc_s, *, n_f):
    del exp_ref
    i = pl.program_id(0)
    f = pl.program_id(1)

    @pl.when(f == 0)
    def _zero():
        acc_s[...] = jnp.zeros(acc_s.shape, F32)

    @pl.when(act_ref[i] == 1)
    def _():
        xe = xs_ref[...]
        h1 = _dot(xe, w1_ref[...].astype(BF16))
        h3 = _dot(xe, w3_ref[...].astype(BF16))
        a = (h1 * _sigmoid(h1) * h3).astype(BF16)
        acc_s[...] += _dot(a, w2_ref[...].astype(BF16))

    @pl.when(f == n_f - 1)
    def _out():
        ys_ref[...] = acc_s[...].astype(BF16)


def _experts_call(tile_exp, tile_act, xs, w1, w3, w2, slot, *, tm, tf, name):
    n_rows = xs.shape[0]
    n_f = D_FF // tf
    grid_spec = pltpu.PrefetchScalarGridSpec(
        num_scalar_prefetch=2,
        grid=(n_rows // tm, n_f),
        in_specs=[
            pl.BlockSpec((tm, D_MODEL), lambda i, f, ex, act: (i, 0)),
            pl.BlockSpec((None, None, D_MODEL, tf), lambda i, f, ex, act: (slot, ex[i], 0, f)),
            pl.BlockSpec((None, None, D_MODEL, tf), lambda i, f, ex, act: (slot, ex[i], 0, f)),
            pl.BlockSpec((None, None, tf, D_MODEL), lambda i, f, ex, act: (slot, ex[i], f, 0)),
        ],
        out_specs=pl.BlockSpec((tm, D_MODEL), lambda i, f, ex, act: (i, 0)),
        scratch_shapes=[pltpu.VMEM((tm, D_MODEL), F32)],
    )
    return pl.pallas_call(
        functools.partial(_experts_kernel, n_f=n_f),
        grid_spec=grid_spec,
        out_shape=jax.ShapeDtypeStruct((n_rows, D_MODEL), BF16),
        compiler_params=pltpu.CompilerParams(dimension_semantics=("arbitrary", "arbitrary"),
                                             vmem_limit_bytes=VMEM_LIMIT),
        name=name,
    )(tile_exp, tile_act, xs, w1, w3, w2)


def _combine_kernel(blk_ref, tile_ref, exp_ref, off_ref, first_ref, last_ref, act_ref,
                    x_ref, rank_c_ref, comb_ref, ys_ref, fg_ref, o_ref, *, tb, final_norm):
    del blk_ref, tile_ref
    p = pl.program_id(0)

    @pl.when(first_ref[p] == 1)
    def _start():
        o_ref[...] = x_ref[...]

    @pl.when(act_ref[p] == 1)
    def _():
        pick = lax.broadcasted_iota(jnp.int32, (tb, HP), 1) == exp_ref[p]
        rcol = jnp.sum(jnp.where(pick, rank_c_ref[...], 0.0), axis=1, keepdims=True)
        gcol = jnp.sum(jnp.where(pick, comb_ref[...], 0.0), axis=1, keepdims=True)
        slot = (lax.broadcasted_iota(jnp.int32, (tb, GATHER_TILE), 1) - off_ref[p]).astype(F32)
        onehot = jnp.where(rcol == slot, 1.0, 0.0).astype(BF16)
        o_ref[...] += gcol * _dot(onehot, ys_ref[...])

    if final_norm:
        @pl.when(last_ref[p] == 1)
        def _norm():
            y = o_ref[...]
            ms = jnp.mean(y * y, axis=-1, keepdims=True)
            o_ref[...] = y * lax.rsqrt(ms + EPS) * fg_ref[...]


def _combine_call(pairs, xall, rank_c, comb, ys, fg, *, tb, final_norm, name):
    n_pairs = pairs[0].shape[0]
    grid_spec = pltpu.PrefetchScalarGridSpec(
        num_scalar_prefetch=7,
        grid=(n_pairs,),
        in_specs=[
            pl.BlockSpec((tb, D_MODEL), lambda p, blk, *_: (blk[p], 0)),
            pl.BlockSpec((tb, HP), lambda p, blk, *_: (blk[p], 0)),
            pl.BlockSpec((tb, HP), lambda p, blk, *_: (blk[p], 0)),
            pl.BlockSpec((GATHER_TILE, D_MODEL), lambda p, blk, tile, *_: (tile[p], 0)),
            pl.BlockSpec((1, D_MODEL), lambda p, *_: (0, 0)),
        ],
        out_specs=pl.BlockSpec((tb, D_MODEL), lambda p, blk, *_: (blk[p], 0)),
    )
    return pl.pallas_call(
        functools.partial(_combine_kernel, tb=tb, final_norm=final_norm),
        grid_spec=grid_spec,
        out_shape=jax.ShapeDtypeStruct(xall.shape, F32),
        input_output_aliases={7: 0},
        compiler_params=pltpu.CompilerParams(dimension_semantics=("arbitrary",), vmem_limit_bytes=VMEM_LIMIT),
        name=name,
    )(*pairs, xall, rank_c, comb, ys, fg)


def _pair_lists(cnt, *, tm, n_rows):
    nb = cnt.shape[0]
    i32 = jnp.int32
    tot = jnp.sum(cnt, axis=0)
    grp_rows = (tot + tm - 1) // tm * tm
    grp_end = jnp.cumsum(grp_rows)
    grp_start = grp_end - grp_rows
    base = grp_start[None, :] + jnp.cumsum(cnt, axis=0) - cnt
    lo = base // GATHER_TILE
    hi = (base + cnt - 1) // GATHER_TILE
    npair = jnp.where(cnt > 0, hi - lo + 1, 0)
    n_pairs = nb * N_EXPERTS + n_rows // GATHER_TILE
    n_keys = nb * N_EXPERTS

    def expand(expert_major):
        def flat(a):
            return (a.T if expert_major else a).reshape(-1)
        np_k = flat(npair)
        cum = jnp.cumsum(np_k)
        total = cum[-1]
        p = jnp.arange(n_pairs, dtype=i32)
        pc = jnp.minimum(p, total - 1)
        k = jnp.sum((cum[None, :] <= pc[:, None]).astype(i32), axis=1)
        table = jnp.stack([flat(lo), flat(base), cum - np_k], axis=1)
        row = jnp.take(table, jnp.minimum(k, n_keys - 1), axis=0)
        tile = row[:, 0] + pc - row[:, 2]
        blk, exp = (k % nb, k // nb) if expert_major else (k // N_EXPERTS, k % N_EXPERTS)
        act = (p < total).astype(i32)
        return tile.astype(i32), blk.astype(i32), exp.astype(i32), (row[:, 1] - tile * GATHER_TILE).astype(i32), act

    g_tile, g_blk, g_exp, g_off, g_act = expand(True)
    g_first = jnp.concatenate([jnp.ones((1,), i32), (g_tile[1:] != g_tile[:-1]).astype(i32)])
    c_tile, c_blk, c_exp, c_off, c_act = expand(False)
    c_first = jnp.concatenate([jnp.ones((1,), i32), (c_blk[1:] != c_blk[:-1]).astype(i32)])
    c_last = jnp.concatenate([(c_blk[1:] != c_blk[:-1]).astype(i32), jnp.ones((1,), i32)])
    c_last = jnp.where(jnp.arange(n_pairs) == jnp.sum(c_act) - 1, 1, c_last) * c_act
    n_tiles = n_rows // tm
    t0 = jnp.arange(n_tiles, dtype=i32) * tm
    t_act = (t0 < grp_end[-1]).astype(i32)
    t_exp = jnp.sum((grp_end[None, :] <= jnp.minimum(t0, grp_end[-1] - 1)[:, None]).astype(i32), axis=1)
    return ((g_tile, g_blk, g_exp, g_off, g_first, g_act),
            (c_blk, c_tile, c_exp, c_off, c_first, c_last, c_act), (t_exp, t_act))


def _moe_layer(xall, g, wr, w1, w3, w2, slot, fg, xs_init, *, tb, tm, tf, final_norm, tag):
    n = xall.shape[0]
    n_rows = 2 * n + N_EXPERTS * tm
    if xs_init is None:
        xs_init = jnp.zeros((n_rows, D_MODEL), BF16)
    hn, rank_t, rank_c, comb, cnt = _route_call(xall, g, wr, tb=tb, name=f"moe_route_{tag}")
    cnt = cnt[:, 0, 0:N_EXPERTS].astype(jnp.int32)
    g_pairs, c_pairs, (t_exp, t_act) = _pair_lists(cnt, tm=tm, n_rows=n_rows)
    xs = _gather_call(g_pairs, rank_t, hn, xs_init, tb=tb, name=f"moe_gather_{tag}")
    ys = _experts_call(t_exp, t_act, xs, w1, w3, w2, slot, tm=tm, tf=tf, name=f"moe_experts_{tag}")
    out = _combine_call(c_pairs, xall, rank_c, comb, ys, fg, tb=tb, final_norm=final_norm,
                        name=f"moe_combine_{tag}")
    return out, xs


def _pad_heads(a, axis):
    axis = axis % a.ndim
    shp = a.shape
    a = a.reshape(shp[:axis] + (NH, DH) + shp[axis + 1:])
    padw = [(0, 0)] * a.ndim
    padw[axis + 1] = (0, HP - DH)
    a = jnp.pad(a, padw)
    return a.reshape(shp[:axis] + (HW,) + shp[axis + 1:])


def _unpad_heads(a, axis):
    axis = axis % a.ndim
    shp = a.shape
    a = a.reshape(shp[:axis] + (NH, HP) + shp[axis + 1:])
    a = lax.slice_in_dim(a, 0, DH, axis=axis + 1)
    return a.reshape(shp[:axis] + (NH * DH,) + shp[axis + 1:])


_M_W = NH * DH
SRC_MQ, SRC_MK, SRC_MV, SRC_MO = 0, _M_W, 2 * _M_W, 3 * _M_W
SRC_MI = 4 * _M_W
SRC_MF = SRC_MI + NH
SRC_PU = SRC_MF + NH
SRC_HQ = SRC_PU + P_W
SRC_HF = SRC_HQ + HW
SRC_HI = SRC_HF + HW
SRC_HG = SRC_HI + _M_W
D_IN = SRC_HG + _M_W


def _w_in_relayout_kernel(w_ref, o32_ref, o16_ref):
    rows = w_ref.shape[0]

    def put(dst, val):
        o32_ref[:, dst:dst + val.shape[1]] = val
        o16_ref[:, dst:dst + val.shape[1]] = val.astype(BF16)

    for src, dst in ((SRC_MQ, OFF_MQ), (SRC_MK, OFF_MK), (SRC_MV, OFF_MV), (SRC_MO, OFF_MO),
                     (SRC_HI, OFF_HI), (SRC_HG, OFF_HG)):
        for h in range(NH):
            put(dst + HP * h, w_ref[:, src + DH * h:src + DH * (h + 1)])
            put(dst + HP * h + DH, jnp.zeros((rows, HP - DH), F32))
    put(OFF_G, w_ref[:, SRC_MI:SRC_MI + NH])
    put(OFF_G + NH, w_ref[:, SRC_MF:SRC_MF + NH])
    put(OFF_G + 2 * NH, jnp.zeros((rows, HP - 2 * NH), F32))
    put(OFF_PU, w_ref[:, SRC_PU:SRC_PU + P_W])
    put(OFF_HQ, w_ref[:, SRC_HQ:SRC_HQ + HW])
    put(OFF_HF, w_ref[:, SRC_HF:SRC_HF + HW])


def _w_in_relayout(w_in, *, tr=256):
    d, k, n = w_in.shape
    assert n == D_IN and k % tr == 0
    return pl.pallas_call(
        _w_in_relayout_kernel,
        grid=(d, k // tr),
        in_specs=[pl.BlockSpec((None, tr, n), lambda l, i: (l, i, 0))],
        out_specs=[pl.BlockSpec((None, tr, D_INP), lambda l, i: (l, i, 0)),
                   pl.BlockSpec((None, tr, D_INP), lambda l, i: (l, i, 0))],
        out_shape=[jax.ShapeDtypeStruct((d, k, D_INP), F32), jax.ShapeDtypeStruct((d, k, D_INP), BF16)],
        compiler_params=pltpu.CompilerParams(dimension_semantics=("arbitrary", "arbitrary"),
                                             vmem_limit_bytes=VMEM_LIMIT),
        name="w_in_relayout",
    )(w_in)


def _prep_weights(norm1_g, w_in, b_igate, b_fgate, conv_w, mlstm_norm_g, pool_w, pool_scale, lb_logits,
                  hgrn_norm_g, w_out):
    m_w = _M_W
    w_in_p, w_in_b = _w_in_relayout(w_in)
    w_out_p = jnp.concatenate([_pad_heads(w_out[:, 0:m_w], 1), w_out[:, m_w:m_w + P_W],
                               _pad_heads(w_out[:, m_w + P_W:], 1)], axis=1)
    gbias = jnp.pad(jnp.concatenate([b_igate, b_fgate], axis=-1), ((0, 0), (0, HP - 2 * NH)))[:, None, :]
    conv_p = jnp.concatenate([_pad_heads(conv_w[..., 0:m_w], -1), _pad_heads(conv_w[..., m_w:], -1)], axis=-1)
    eye = jnp.eye(len(POOL_WINDOWS), dtype=F32)
    pool_bd = jnp.einsum('lgce,gh->lgche', pool_w, eye).reshape(DEPTH, P_W, P_W)
    return {
        "g1": norm1_g[:, None, :], "w_in": w_in_b, "w_in_f32": w_in_p, "gbias": gbias,
        "conv_w": conv_p, "mnorm": _pad_heads(mlstm_norm_g, -1)[:, None, :],
        "pool_w": pool_bd.astype(BF16), "pool_w_f32": pool_bd,
        "pool_scale": pool_scale[:, None, :], "lb_logits": lb_logits,
        "hnorm": _pad_heads(hgrn_norm_g, -1)[:, None, :], "w_out": w_out_p.astype(BF16), "w_out_f32": w_out_p,
    }


def _states_to_kernel(C, n, m, conv, pool, S):
    nb = C.shape[0]
    caug = jnp.concatenate([C, n[..., None]], axis=-1)
    caug = jnp.pad(caug, ((0, 0), (0, 0), (0, HP - DH), (0, HP - DH - 1)))
    mk = jnp.pad(jnp.broadcast_to(m[:, :, None], (nb, NH, HP)), ((0, 0), (0, 8 - NH), (0, 0)))
    m_w = NH * DH
    convk = jnp.concatenate([_pad_heads(conv[..., 0:m_w], -1), _pad_heads(conv[..., m_w:], -1)], axis=-1)
    convk = jnp.pad(convk, ((0, 0), (CONV_HDR - 3, 0), (0, 0)))
    poolk = jnp.pad(pool, ((0, 0), (1, 0), (0, 0)))
    sk = jnp.pad(jnp.swapaxes(S, -1, -2), ((0, 0), (0, 0), (0, HP - DH), (0, 0)))
    return caug, mk, convk, poolk, sk


def _states_from_kernel(st):
    caug, mk, convk, poolk, sk = st
    C = caug[:, :, 0:DH, 0:DH]
    n = caug[:, :, 0:DH, DH]
    m = mk[:, 0:NH, 0]
    conv = convk[:, CONV_HDR - 3:, :]
    conv = jnp.concatenate([_unpad_heads(conv[..., 0:HW], -1), _unpad_heads(conv[..., HW:], -1)], axis=-1)
    pool = poolk[:, 1:, :]
    S = jnp.swapaxes(sk[:, :, 0:DH, :], -1, -2)
    return C, n, m, conv, pool, S


def _pick_tile(n, candidates):
    for c in candidates:
        if n % c == 0:
            return c
    raise ValueError(f"no row tile for {n}")


def kernel(x_prompt, x_sample, state_mlstm_C, state_mlstm_n, state_mlstm_m, state_mlstm_conv, state_pool,
           state_hgrn, meta_tokens, norm1_g, norm2_g, final_g, w_in, b_igate, b_fgate, conv_w, mlstm_norm_g,
           pool_w, pool_scale, lb_logits, hgrn_norm_g, w_out, ffn_w1, ffn_w3, ffn_w2, router_w, moe_w1,
           moe_w3, moe_w2):
    B, T, _ = x_prompt.shape
    SB, ST, _ = x_sample.shape
    w = _prep_weights(norm1_g, w_in, b_igate, b_fgate, conv_w, mlstm_norm_g, pool_w, pool_scale, lb_logits,
                      hgrn_norm_g, w_out)
    ffn_w1b, ffn_w3b, ffn_w2b = ffn_w1.astype(BF16), ffn_w3.astype(BF16), ffn_w2.astype(BF16)
    router_p = jnp.pad(router_w, ((0, 0), (0, 0), (0, HP - N_EXPERTS)))

    n_main = B * T
    off_s = n_main
    off_m = off_s + SB * ST
    assert SB * ST == TAIL and n_main % TAIL == 0
    n_tok = off_m + META_TOKENS
    tile = 1280 if n_main >= 16384 else 256
    expert_tile = 1024 if n_main >= 16384 else 256
    n_pad = -(-n_tok // tile) * tile
    xall = jnp.concatenate([x_prompt.reshape(n_main, D_MODEL), x_sample.reshape(SB * ST, D_MODEL),
                            meta_tokens.astype(F32), jnp.zeros((n_pad - n_tok, D_MODEL), F32)], axis=0)
    tv_main = _pick_tile(T, (256, 128))

    zero_states = (jnp.zeros((1, NH, HP, HP), F32), jnp.zeros((1, 8, HP), F32),
                   jnp.zeros((1, CONV_HDR, 2 * HW), F32), jnp.zeros((1, 16, P_W), F32),
                   jnp.zeros((1, NH, HP, HP), F32))
    p_states, s_states = [], []
    xs_buf = None
    st_in = jax.vmap(_states_to_kernel)(state_mlstm_C, state_mlstm_n, state_mlstm_m, state_mlstm_conv,
                                        state_pool, state_hgrn)
    tail_blocks = tuple((b * T + T - TAIL) // TAIL for b in range(B)) + (off_s // TAIL,)
    for l in range(DEPTH):
        precise = l < PRECISE_LAYERS
        xall, st_meta = _mixer_call(xall, zero_states, True, w, l, row_off=off_m, seq_stride=META_TOKENS, nb=1,
                                    seq=META_TOKENS, tv=META_TOKENS, pos0=0, precise=False,
                                    name=f"mixer_meta_{l}")
        if precise:
            xall, st_p = _mixer_call(xall, st_meta, True, w, l, row_off=0, seq_stride=T, nb=B, seq=T - TAIL,
                                     tv=tv_main, pos0=META_TOKENS, precise=False, name=f"mixer_prompt_{l}")
            xall, st_p = _mixer_call(xall, st_p, False, w, l, row_off=T - TAIL, seq_stride=T, nb=B, seq=TAIL,
                                     tv=TAIL, pos0=META_TOKENS + T - TAIL, precise=True,
                                     name=f"mixer_prompt_tail_{l}")
        else:
            xall, st_p = _mixer_call(xall, st_meta, True, w, l, row_off=0, seq_stride=T, nb=B, seq=T,
                                     tv=tv_main, pos0=META_TOKENS, precise=False, name=f"mixer_prompt_{l}")
        xall, st_s = _mixer_call(xall, st_in, False, w, l, row_off=off_s, seq_stride=ST, nb=SB, seq=ST, tv=ST,
                                 pos0=META_TOKENS + PAST_LEN, precise=True, name=f"mixer_sample_{l}",
                                 stacked_states=True)
        p_states.append(st_p)
        s_states.append(st_s)
        i = l // 2
        if l % 2 == 0:
            f32_blocks = tail_blocks if precise else tail_blocks[-1:]
            rows = _ffn_rows_f32_call(xall, f32_blocks, norm2_g[l][None, :], ffn_w1, ffn_w3, ffn_w2, i,
                                      tm=TAIL, tf=512, name=f"ffn_rows_f32_{l}")
            xall = _ffn_call(xall, norm2_g[l][None, :], ffn_w1b, ffn_w3b, ffn_w2b, i,
                             tm=tile, tf=512, name=f"ffn_{l}")
            for j, blk in enumerate(f32_blocks):
                xall = lax.dynamic_update_slice(xall, rows[j * TAIL:(j + 1) * TAIL], (blk * TAIL, 0))
        else:
            xall, xs_buf = _moe_layer(xall, norm2_g[l][None, :], router_p[i], moe_w1, moe_w3, moe_w2, i,
                                      final_g[None, :], xs_buf, tb=tile, tm=expert_tile, tf=512,
                                      final_norm=(l == DEPTH - 1), tag=str(l))
    y_prompt = xall[0:n_main].reshape(B, T, D_MODEL)
    y_sample = xall[off_s:off_m].reshape(SB, ST, D_MODEL)
    p_out = jax.vmap(_states_from_kernel)(tuple(jnp.stack([s[j] for s in p_states], axis=0) for j in range(5)))
    s_out = jax.vmap(_states_from_kernel)(tuple(jnp.stack([s[j] for s in s_states], axis=0) for j in range(5)))
    return (y_prompt, y_sample) + tuple(p_out) + tuple(s_out)
```

```python
import functools

import jax
import jax.numpy as jnp
from jax import lax
from jax.experimental import pallas as pl
from jax.experimental.pallas import tpu as pltpu

F32 = jnp.float32
BF16 = jnp.bfloat16

D_MODEL = 1024
DEPTH = 4
META_TOKENS = 16
PAST_LEN = 4096
EPS = 1e-6
NH = 4
DH = 96
HP = 128
HW = NH * HP
P_W = 256
P_GW = 64
POOL_WINDOWS = (2, 4, 8, 16)
D_FF = 3584
N_EXPERTS = 8
NEG = -1e30

OFF_MQ, OFF_MK, OFF_MV, OFF_MO = 0, 512, 1024, 1536
OFF_G = 2048
OFF_PU = 2176
OFF_HQ, OFF_HF, OFF_HI, OFF_HG = 2432, 2944, 3456, 3968
D_INP = 4480
OFF_YM, OFF_YP, OFF_YH = 0, 512, 768
D_MIXP = 1280

MIN_ROWS = 128
M_CHUNK = 256
H_CHUNK = 128
H_SUB = 32
H_SAFE_LOG_DECAY = -60.0
CONV_HDR = 8
POOL_HDR = 32

TAIL = 256
PRECISE_LAYERS = 2

VMEM_LIMIT = 60 * 1024 * 1024


def _sigmoid(x):
    return 1.0 / (1.0 + jnp.exp(-x))


X3 = "bf16x3"


def _split_bf16(a):
    hi = a.astype(BF16)
    return hi, (a - hi.astype(F32)).astype(BF16)


def _dg(a, b, dims, precision=None):
    if precision == X3:
        ah, al = _split_bf16(a)
        bh, bl = _split_bf16(b)
        return (lax.dot_general(ah, bh, dims, preferred_element_type=F32)
                + lax.dot_general(al, bh, dims, preferred_element_type=F32)
                + lax.dot_general(ah, bl, dims, preferred_element_type=F32))
    return lax.dot_general(a, b, dims, preferred_element_type=F32, precision=precision)


def _dot(a, b, precision=None):
    return _dg(a, b, (((1,), (0,)), ((), ())), precision)


def _dot_nt(a, b, precision=None):
    return _dg(a, b, (((1,), (1,)), ((), ())), precision)


def _dot_tn(a, b, precision=None):
    return _dg(a, b, (((0,), (0,)), ((), ())), precision)


def _cumsum_rows(tril_bf, x, precise=False):
    n = x.shape[1]
    hi = x.astype(BF16)
    rest = x - hi.astype(F32)
    lo = rest.astype(BF16)
    parts = [hi, lo] + ([(rest - lo.astype(F32)).astype(BF16)] if precise else [])
    both = _dot(tril_bf, jnp.concatenate(parts, axis=1))
    out = both[:, :n] + both[:, n:2 * n]
    return out + both[:, 2 * n:] if precise else out


def _bcast_rows(row, n):
    return jnp.broadcast_to(row, (n, row.shape[1]))


def _mixer_kernel(x_ref, c0_ref, m0_ref, conv0_ref, pool0_ref, s0_ref,
                  g1_ref, win_ref, gb_ref, cw_ref, mng_ref, pw_ref, ps_ref, lbl_ref, hng_ref, wout_ref,
                  xo_ref, cf_ref, mf_ref, convf_ref, poolf_ref, sf_ref,
                  proj_s, qk_s, u_s, s2_s, s4_s, s8_s, hk_s, gate_s, y_s, gx_s, st_old_s, c_s, m_s, st_s,
                  *, layer, tv, lb, pos0, n_t, precise, m_chunk, h_chunk):
    pad = lb - tv
    t = pl.program_id(1)
    prec = X3 if precise else None
    dot = functools.partial(_dot, precision=prec)
    dot_nt = functools.partial(_dot_nt, precision=prec)
    dot_tn = functools.partial(_dot_tn, precision=prec)

    def mm(a):
        return a if precise else a.astype(BF16)

    @pl.when(t == 0)
    def _init():
        c_s[...] = c0_ref[...]
        m_s[...] = m0_ref[...]
        st_s[...] = s0_ref[...]
        qk_s[...] = jnp.zeros(qk_s.shape, F32)
        qk_s[pad:pad + CONV_HDR, :] = conv0_ref[...]
        u_s[...] = jnp.zeros(u_s.shape, F32)
        u_s[pad + 16:pad + 32, :] = pool0_ref[...]
        s2_s[0:POOL_HDR, :] = jnp.zeros((POOL_HDR, P_W), F32)
        s4_s[0:POOL_HDR, :] = jnp.zeros((POOL_HDR, P_W), F32)
        s8_s[0:POOL_HDR, :] = jnp.zeros((POOL_HDR, P_W), F32)
        if pad:
            proj_s[0:pad, :] = jnp.zeros((pad, D_INP), F32)

    x = x_ref[...]
    ms = jnp.mean(x * x, axis=-1, keepdims=True)
    hn = mm(x * lax.rsqrt(ms + EPS) * g1_ref[...])
    qk_s[CONV_HDR + pad:CONV_HDR + lb, :] = dot(hn, win_ref[:, 0:OFF_MV])
    proj_s[pad:lb, OFF_MV:OFF_PU] = dot(hn, win_ref[:, OFF_MV:OFF_PU])
    u_s[POOL_HDR + pad:POOL_HDR + lb, :] = dot(hn, win_ref[:, OFF_PU:OFF_HQ])
    proj_s[pad:lb, OFF_HQ:D_INP] = dot(hn, win_ref[:, OFF_HQ:D_INP])

    row = lax.broadcasted_iota(jnp.int32, (lb, HP), 0)
    lane = lax.broadcasted_iota(jnp.int32, (lb, HP), 1)

    acc = qk_s[5:5 + lb, :] * cw_ref[0:1, :]
    for j in range(1, 4):
        acc = acc + qk_s[5 + j:5 + j + lb, :] * cw_ref[j:j + 1, :]
    qk = acc * _sigmoid(acc)
    proj_s[:, OFF_MQ:OFF_MK] = qk[:, 0:HW]
    proj_s[:, OFF_MK:OFF_MV] = qk[:, HW:2 * HW] * (DH ** -0.5)
    conv_tail = qk_s[lb:lb + CONV_HDR, :]
    qk_s[0:CONV_HDR, :] = conv_tail

    gpre = proj_s[:, OFF_G:OFF_G + HP] + gb_ref[...]
    lsig = jnp.minimum(gpre, 0.0) - jnp.log(1.0 + jnp.exp(-jnp.abs(gpre)))
    gates = jnp.where(lane < NH, gpre, jnp.where(lane < 2 * NH, lsig, 0.0))
    if pad:
        gates = jnp.where(row >= pad, gates, jnp.where(lane < NH, NEG, 0.0))
    gate_s[...] = gates

    n_ext = lb + 16
    s2_s[16:16 + n_ext, :] = u_s[16:16 + n_ext, :] + u_s[15:15 + n_ext, :]
    s4_s[16:16 + n_ext, :] = s2_s[16:16 + n_ext, :] + s2_s[14:14 + n_ext, :]
    s8_s[16:16 + n_ext, :] = s4_s[16:16 + n_ext, :] + s4_s[12:12 + n_ext, :]
    u_cur = u_s[POOL_HDR:POOL_HDR + lb, :]
    w2 = s2_s[POOL_HDR:POOL_HDR + lb, :]
    w4 = s4_s[POOL_HDR:POOL_HDR + lb, :]
    w8 = s8_s[POOL_HDR:POOL_HDR + lb, :]
    w16 = w8 + s8_s[POOL_HDR - 8:POOL_HDR - 8 + lb, :]
    lane_p = lax.broadcasted_iota(jnp.int32, (lb, P_W), 1)
    wsum = jnp.where(lane_p < P_GW, w2, jnp.where(lane_p < 2 * P_GW, w4, jnp.where(lane_p < 3 * P_GW, w8, w16)))
    if pos0 >= POOL_WINDOWS[-1] - 1:
        inv = jnp.where(lane_p < P_GW, 0.5, jnp.where(lane_p < 2 * P_GW, 0.25,
                                                       jnp.where(lane_p < 3 * P_GW, 0.125, 0.0625)))
        mean = wsum * inv
    else:
        row_p = lax.broadcasted_iota(jnp.int32, (lb, P_W), 0)
        posn = (row_p + (pos0 + 1 - pad + t * tv)).astype(F32)
        wlen = jnp.where(lane_p < P_GW, 2.0, jnp.where(lane_p < 2 * P_GW, 4.0,
                                                        jnp.where(lane_p < 3 * P_GW, 8.0, 16.0)))
        mean = wsum / jnp.maximum(jnp.minimum(wlen, posn), 1.0)
    pooled = mm(mean - u_cur)
    y_s[:, OFF_YP:OFF_YH] = mm(dot(pooled, pw_ref[...]) * ps_ref[...])
    pool_tail = u_s[lb:lb + POOL_HDR, :]
    u_s[0:POOL_HDR, :] = pool_tail

    lbl = lbl_ref[...]
    e = jnp.exp(lbl - jnp.max(lbl, axis=0, keepdims=True))
    p = e / jnp.sum(e, axis=0, keepdims=True)
    lbv = jnp.sum(p[0:layer + 1, :], axis=0, keepdims=True) - p[0:1, :]
    hq = proj_s[:, OFF_HQ:OFF_HF]
    hf = proj_s[:, OFF_HF:OFF_HI]
    fg = lbv + (1.0 - lbv) * _sigmoid(hf)
    kh = 1.0 - fg
    lg = jnp.log(fg)
    if pad:
        row_h = lax.broadcasted_iota(jnp.int32, (lb, HW), 0)
        kh = jnp.where(row_h >= pad, kh, 0.0)
        lg = jnp.where(row_h >= pad, lg, 0.0)
    proj_s[:, OFF_HQ:OFF_HF] = hq * _sigmoid(hq)
    proj_s[:, OFF_HF:OFF_HI] = lg
    hk_s[...] = kh

    rr = lax.broadcasted_iota(jnp.int32, (m_chunk, m_chunk), 0)
    cc = lax.broadcasted_iota(jnp.int32, (m_chunk, m_chunk), 1)
    causal = rr >= cc
    tril_m = jnp.where(causal, 1.0, 0.0).astype(BF16)
    lane_c = lax.broadcasted_iota(jnp.int32, (m_chunk, HP), 1)

    for c in range(lb // m_chunk):
        rows = slice(c * m_chunk, (c + 1) * m_chunk)
        gt = gate_s[rows, :]
        bcum = _cumsum_rows(tril_m, jnp.where(lane_c >= NH, gt, 0.0), precise)
        cg = gt - pltpu.roll(bcum, HP - NH, axis=1)
        cg_t = cg.T
        for h in range(NH):
            q = proj_s[rows, OFF_MQ + h * HP:OFF_MQ + (h + 1) * HP]
            k = proj_s[rows, OFF_MK + h * HP:OFF_MK + (h + 1) * HP]
            v = proj_s[rows, OFF_MV + h * HP:OFF_MV + (h + 1) * HP]
            c_row = cg_t[h:h + 1, :]
            c_col = cg[:, h:h + 1]
            b_col = bcum[:, NH + h:NH + h + 1]
            m_prev = m_s[h:h + 1, 0:1]
            mx = jnp.maximum(jnp.max(jnp.where(causal, c_row, NEG), axis=1, keepdims=True), m_prev)
            w = jnp.exp(jnp.where(causal, c_row - mx, NEG))
            w_int = jnp.exp(m_prev - mx)
            mx_last = mx[m_chunk - 1:m_chunk, :]
            s = dot_nt(mm(q), mm(k)) * w
            vaug = jnp.where(lane_c == DH, 1.0, v)
            caug = c_s[h]
            lhs = mm(jnp.concatenate([s, q * w_int], axis=1))
            rhs = mm(jnp.concatenate([vaug, caug], axis=0))
            nd = dot(lhs, rhs)
            den = nd[:, DH:DH + 1]
            rden = 1.0 / jnp.maximum(jnp.abs(den), jnp.exp(-(b_col + mx)))
            wl = jnp.exp(c_col - mx_last)
            decay = jnp.exp(m_prev - mx_last)
            c_s[h] = decay * caug + dot_tn(mm(k * wl), mm(vaug))
            m_s[h:h + 1, :] = jnp.broadcast_to(b_col[m_chunk - 1:m_chunk, :] + mx_last, (1, HP))
            mo = proj_s[rows, OFF_MO + h * HP:OFF_MO + (h + 1) * HP]
            z = jnp.where(lane_c < DH, nd * _sigmoid(mo), 0.0)
            ssq = jnp.sum(z * z, axis=1, keepdims=True) * (1.0 / DH)
            fac = rden * lax.rsqrt(rden * rden * ssq + EPS)
            y_s[rows, OFF_YM + h * HP:OFF_YM + (h + 1) * HP] = mm(z * fac * mng_ref[:, h * HP:(h + 1) * HP])

    r64 = lax.broadcasted_iota(jnp.int32, (h_chunk, h_chunk), 0)
    c64 = lax.broadcasted_iota(jnp.int32, (h_chunk, h_chunk), 1)
    tril_h = jnp.where(r64 >= c64, 1.0, 0.0).astype(BF16)
    bdiff = r64 // H_SUB - c64 // H_SUB
    mask_intra = (bdiff == 0) & (r64 >= c64)
    n_sub = h_chunk // H_SUB

    def hgrn_block(c, factored):
        rows = slice(c * h_chunk, (c + 1) * h_chunk)
        g = _cumsum_rows(tril_h, proj_s[rows, OFF_HF:OFF_HI], precise)
        qh = proj_s[rows, OFF_HQ:OFF_HF]
        khc = hk_s[rows, :]
        if not factored:
            gx_s[...] = g
        bnd = [jnp.zeros((1, HW), F32)] + [g[(j + 1) * H_SUB - 1:(j + 1) * H_SUB, :] for j in range(n_sub)]
        g_start = jnp.concatenate([_bcast_rows(bnd[j], H_SUB) for j in range(n_sub)], axis=0)
        g_end = jnp.concatenate([_bcast_rows(bnd[j + 1], H_SUB) for j in range(n_sub)], axis=0)
        g_last = bnd[n_sub]
        qt = qh * jnp.exp(g - g_start)
        khat = khc * jnp.exp(g_end - g)
        kbar = khc * jnp.exp(g_start - g)
        qg = qh * jnp.exp(g)
        kend = khc * jnp.exp(g_last - g)
        dsub = [jnp.exp(bnd[j + 1] - bnd[j]) for j in range(n_sub - 1)]
        ones = jnp.ones((H_SUB, HW), F32)
        qlev = [qt]
        for d in range(1, n_sub - 1):
            fac = jnp.concatenate([ones] * d + [_bcast_rows(dsub[j - d], H_SUB) for j in range(d, n_sub)], axis=0)
            qlev.append(qlev[-1] * fac)
        for h in range(NH):
            sl = slice(h * HP, (h + 1) * HP)
            if factored:
                lhs = mm(jnp.concatenate([ql[:, sl] for ql in qlev], axis=0))
                inter = dot_nt(lhs, mm(khat[:, sl]))
                intra = dot_nt(mm(qt[:, sl]), mm(kbar[:, sl]))
                att = jnp.where(mask_intra, intra, 0.0)
                for d in range(n_sub - 1):
                    att = att + jnp.where(bdiff == d + 1, inter[d * h_chunk:(d + 1) * h_chunk, :], 0.0)
            else:
                def cols(j, att_acc, sl=sl, gh=g[:, sl], qhh=qh[:, sl]):
                    r0 = pl.multiple_of(j * 8, 8)
                    g8 = gx_s[pl.ds(r0, 8), sl]
                    k8 = hk_s[pl.ds(c * h_chunk + r0, 8), sl]
                    for r in range(8):
                        wgt = jnp.exp(jnp.minimum(gh - g8[r:r + 1, :], 0.0))
                        pcol = jnp.sum(qhh * wgt * k8[r:r + 1, :], axis=1, keepdims=True)
                        att_acc = att_acc + jnp.where((c64 == r0 + r) & (r64 >= r0 + r), pcol, 0.0)
                    return att_acc
                att = lax.fori_loop(0, h_chunk // 8, cols, jnp.zeros((h_chunk, h_chunk), F32))
            iv = mm(proj_s[rows, OFF_HI + h * HP:OFF_HI + (h + 1) * HP])
            st = st_s[h]
            o = dot(mm(att), iv) + dot_nt(mm(qg[:, sl]), mm(st))
            st_s[h] = st * jnp.exp(g_last[:, sl]) + dot_tn(iv, mm(kend[:, sl]))
            msq = jnp.sum(o * o, axis=1, keepdims=True) * (1.0 / DH)
            on = o * lax.rsqrt(msq + EPS) * hng_ref[:, sl]
            hg = proj_s[rows, OFF_HG + h * HP:OFF_HG + (h + 1) * HP]
            y_s[rows, OFF_YH + h * HP:OFF_YH + (h + 1) * HP] = mm(on * (hg * _sigmoid(hg)))

    st_old_s[...] = st_s[...]
    for c in range(lb // h_chunk):
        hgrn_block(c, True)

    def out_proj():
        out = dot(y_s[...], wout_ref[...])
        xo_ref[...] = x + out[pad:lb, :]

    out_proj()
    lg_sub = proj_s[:, OFF_HF:OFF_HI].reshape(lb // H_SUB, H_SUB, HW)
    factor_ok = jnp.min(jnp.sum(lg_sub, axis=1)) > H_SAFE_LOG_DECAY

    @pl.when(jnp.logical_not(factor_ok))
    def _redo_direct():
        st_s[...] = st_old_s[...]
        for c in range(lb // h_chunk):
            hgrn_block(c, False)
        out_proj()

    @pl.when(t == n_t - 1)
    def _final():
        cf_ref[...] = c_s[...]
        mf_ref[...] = m_s[...]
        convf_ref[...] = conv_tail
        poolf_ref[...] = pool_tail[16:32, :]
        sf_ref[...] = st_s[...]


def _mixer_call(xall, states, shared_init, w, layer, *, row_off, seq_stride, nb, seq, tv, pos0, precise, name,
                stacked_states=False):
    lb = max(tv, MIN_ROWS)
    m_chunk, h_chunk = min(lb, M_CHUNK), min(lb, H_CHUNK)
    n_t = seq // tv
    assert row_off % tv == 0 and seq_stride % tv == 0 and seq % tv == 0
    blk0, blk_stride = row_off // tv, seq_stride // tv
    c0, m0, conv0, pool0, s0 = states
    lead_blk = (None,) if stacked_states else ()
    lead_idx = (layer,) if stacked_states else ()

    def x_map(b, t):
        return (blk0 + b * blk_stride + t, 0)

    def st_map(b, t):
        return lead_idx + (0 if shared_init else b, 0, 0, 0)

    def st_map3(b, t):
        return lead_idx + (0 if shared_init else b, 0, 0)

    def layer_spec(shape):
        return pl.BlockSpec((None,) + shape, lambda b, t: (layer,) + (0,) * len(shape),
                            pipeline_mode=pl.Buffered(1))

    in_specs = [
        pl.BlockSpec((tv, D_MODEL), x_map),
        pl.BlockSpec(lead_blk + (None, NH, HP, HP), st_map),
        pl.BlockSpec(lead_blk + (None, 8, HP), st_map3),
        pl.BlockSpec(lead_blk + (None, CONV_HDR, 2 * HW), st_map3),
        pl.BlockSpec(lead_blk + (None, 16, P_W), st_map3),
        pl.BlockSpec(lead_blk + (None, NH, HP, HP), st_map),
        layer_spec((1, D_MODEL)),
        layer_spec((D_MODEL, D_INP)),
        layer_spec((1, HP)),
        layer_spec((4, 2 * HW)),
        layer_spec((1, HW)),
        layer_spec((P_W, P_W)),
        layer_spec((1, P_W)),
        pl.BlockSpec((DEPTH, HW), lambda b, t: (0, 0), pipeline_mode=pl.Buffered(1)),
        layer_spec((1, HW)),
        layer_spec((D_MIXP, D_MODEL)),
    ]
    out_specs = [
        pl.BlockSpec((tv, D_MODEL), x_map),
        pl.BlockSpec((None, NH, HP, HP), lambda b, t: (b, 0, 0, 0)),
        pl.BlockSpec((None, 8, HP), lambda b, t: (b, 0, 0)),
        pl.BlockSpec((None, CONV_HDR, 2 * HW), lambda b, t: (b, 0, 0)),
        pl.BlockSpec((None, 16, P_W), lambda b, t: (b, 0, 0)),
        pl.BlockSpec((None, NH, HP, HP), lambda b, t: (b, 0, 0, 0)),
    ]
    out_shape = [
        jax.ShapeDtypeStruct(xall.shape, F32),
        jax.ShapeDtypeStruct((nb, NH, HP, HP), F32),
        jax.ShapeDtypeStruct((nb, 8, HP), F32),
        jax.ShapeDtypeStruct((nb, CONV_HDR, 2 * HW), F32),
        jax.ShapeDtypeStruct((nb, 16, P_W), F32),
        jax.ShapeDtypeStruct((nb, NH, HP, HP), F32),
    ]
    scratch = [
        pltpu.VMEM((lb, D_INP), F32),
        pltpu.VMEM((CONV_HDR + lb, 2 * HW), F32),
        pltpu.VMEM((POOL_HDR + lb, P_W), F32),
        pltpu.VMEM((POOL_HDR + lb, P_W), F32),
        pltpu.VMEM((POOL_HDR + lb, P_W), F32),
        pltpu.VMEM((POOL_HDR + lb, P_W), F32),
        pltpu.VMEM((lb, HW), F32),
        pltpu.VMEM((lb, HP), F32),
        pltpu.VMEM((lb, D_MIXP), F32 if precise else BF16),
        pltpu.VMEM((h_chunk, HW), F32),
        pltpu.VMEM((NH, HP, HP), F32),
        pltpu.VMEM((NH, HP, HP), F32),
        pltpu.VMEM((8, HP), F32),
        pltpu.VMEM((NH, HP, HP), F32),
    ]
    sfx = "_f32" if precise else ""
    kern = functools.partial(_mixer_kernel, layer=layer, tv=tv, lb=lb, pos0=pos0, n_t=n_t, precise=precise,
                             m_chunk=m_chunk, h_chunk=h_chunk)
    outs = pl.pallas_call(
        kern,
        grid=(nb, n_t),
        in_specs=in_specs,
        out_specs=out_specs,
        out_shape=out_shape,
        scratch_shapes=scratch,
        input_output_aliases={0: 0},
        compiler_params=pltpu.CompilerParams(dimension_semantics=("arbitrary", "arbitrary"),
                                             vmem_limit_bytes=VMEM_LIMIT),
        name=name,
    )(xall, c0, m0, conv0, pool0, s0,
      w["g1"], w["w_in" + sfx], w["gbias"], w["conv_w"], w["mnorm"],
      w["pool_w" + sfx], w["pool_scale"], w["lb_logits"], w["hnorm"], w["w_out" + sfx])
    return outs[0], tuple(outs[1:])


def _ffn_kernel(x_ref, g_ref, w1_ref, w3_ref, w2_ref, o_ref, hn_s, *, precise):
    f = pl.program_id(1)
    prec = X3 if precise else None

    @pl.when(f == 0)
    def _start():
        x = x_ref[...]
        ms = jnp.mean(x * x, axis=-1, keepdims=True)
        hn_s[...] = (x * lax.rsqrt(ms + EPS) * g_ref[...]).astype(hn_s.dtype)
        o_ref[...] = x

    def wt(ref):
        return ref[...] if precise else ref[...].astype(BF16)

    hn = hn_s[...]
    h1 = _dot(hn, wt(w1_ref), prec)
    h3 = _dot(hn, wt(w3_ref), prec)
    a = (h1 * _sigmoid(h1) * h3).astype(hn_s.dtype)
    o_ref[...] += _dot(a, wt(w2_ref), prec)


def _ffn_rows_f32_call(xall, blocks, g, w1, w3, w2, slot, *, tm, tf, name):
    n_f = D_FF // tf

    def x_map(i, f):
        idx = blocks[-1]
        for j in range(len(blocks) - 2, -1, -1):
            idx = jnp.where(i == j, blocks[j], idx)
        return (idx, 0)

    return pl.pallas_call(
        functools.partial(_ffn_kernel, precise=True),
        grid=(len(blocks), n_f),
        in_specs=[
            pl.BlockSpec((tm, D_MODEL), x_map),
            pl.BlockSpec((1, D_MODEL), lambda i, f: (0, 0)),
            pl.BlockSpec((None, D_MODEL, tf), lambda i, f: (slot, 0, f)),
            pl.BlockSpec((None, D_MODEL, tf), lambda i, f: (slot, 0, f)),
            pl.BlockSpec((None, tf, D_MODEL), lambda i, f: (slot, f, 0)),
        ],
        out_specs=pl.BlockSpec((tm, D_MODEL), lambda i, f: (i, 0)),
        out_shape=jax.ShapeDtypeStruct((len(blocks) * tm, D_MODEL), F32),
        scratch_shapes=[pltpu.VMEM((tm, D_MODEL), F32)],
        compiler_params=pltpu.CompilerParams(dimension_semantics=("arbitrary", "arbitrary"),
                                             vmem_limit_bytes=VMEM_LIMIT),
        name=name,
    )(xall, g, w1, w3, w2)


def _ffn_call(xall, g, w1, w3, w2, slot, *, tm, tf, name):
    n = xall.shape[0]
    n_f = D_FF // tf
    return pl.pallas_call(
        functools.partial(_ffn_kernel, precise=False),
        grid=(n // tm, n_f),
        in_specs=[
            pl.BlockSpec((tm, D_MODEL), lambda i, f: (i, 0)),
            pl.BlockSpec((1, D_MODEL), lambda i, f: (0, 0)),
            pl.BlockSpec((None, D_MODEL, tf), lambda i, f: (slot, 0, f)),
            pl.BlockSpec((None, D_MODEL, tf), lambda i, f: (slot, 0, f)),
            pl.BlockSpec((None, tf, D_MODEL), lambda i, f: (slot, f, 0)),
        ],
        out_specs=pl.BlockSpec((tm, D_MODEL), lambda i, f: (i, 0)),
        out_shape=jax.ShapeDtypeStruct(xall.shape, F32),
        scratch_shapes=[pltpu.VMEM((tm, D_MODEL), BF16)],
        input_output_aliases={0: 0},
        compiler_params=pltpu.CompilerParams(dimension_semantics=("arbitrary", "arbitrary"),
                                             vmem_limit_bytes=VMEM_LIMIT),
        name=name,
    )(xall, g, w1, w3, w2)


GATHER_TILE = 256
NO_SLOT = -1e9


def _route_kernel(x_ref, g_ref, wr_ref, hn_ref, rank_t_ref, rank_c_ref, comb_ref, cnt_ref, *, tb):
    x = x_ref[...]
    ms = jnp.mean(x * x, axis=-1, keepdims=True)
    hn = x * lax.rsqrt(ms + EPS) * g_ref[...]
    hn_ref[...] = hn.astype(BF16)
    logits = _dot(hn, wr_ref[...], X3)
    lane = lax.broadcasted_iota(jnp.int32, (tb, HP), 1).astype(F32)
    lg = jnp.where(lane < N_EXPERTS, logits, NEG)
    v1 = jnp.max(lg, axis=1, keepdims=True)
    i1 = jnp.min(jnp.where(lg == v1, lane, float(HP)), axis=1, keepdims=True)
    mask1 = lane == i1
    lg2 = jnp.where(mask1, NEG, lg)
    v2 = jnp.max(lg2, axis=1, keepdims=True)
    i2 = jnp.min(jnp.where(lg2 == v2, lane, float(HP)), axis=1, keepdims=True)
    mask2 = lane == i2
    ex = jnp.exp(v2 - v1)
    ga = 1.0 / (1.0 + ex)
    comb_ref[...] = jnp.where(mask1, ga, 0.0) + jnp.where(mask2, ex * ga, 0.0)
    sel = mask1 | mask2
    rr = lax.broadcasted_iota(jnp.int32, (tb, tb), 0)
    cc = lax.broadcasted_iota(jnp.int32, (tb, tb), 1)
    tril_strict = jnp.where(rr > cc, 1.0, 0.0).astype(BF16)
    selb = jnp.where(sel, 1.0, 0.0)
    rank = jnp.where(sel, _dot(tril_strict, selb.astype(BF16)), NO_SLOT)
    rank_c_ref[...] = rank
    rank_t_ref[...] = rank.T[0:N_EXPERTS, :]
    cnt_ref[...] = jnp.broadcast_to(jnp.sum(selb, axis=0, keepdims=True), (8, HP))


def _route_call(xall, g, wr, *, tb, name):
    n = xall.shape[0]
    nb = n // tb
    return pl.pallas_call(
        functools.partial(_route_kernel, tb=tb),
        grid=(nb,),
        in_specs=[
            pl.BlockSpec((tb, D_MODEL), lambda i: (i, 0)),
            pl.BlockSpec((1, D_MODEL), lambda i: (0, 0)),
            pl.BlockSpec((D_MODEL, HP), lambda i: (0, 0)),
        ],
        out_specs=[
            pl.BlockSpec((tb, D_MODEL), lambda i: (i, 0)),
            pl.BlockSpec((None, N_EXPERTS, tb), lambda i: (i, 0, 0)),
            pl.BlockSpec((tb, HP), lambda i: (i, 0)),
            pl.BlockSpec((tb, HP), lambda i: (i, 0)),
            pl.BlockSpec((None, 8, HP), lambda i: (i, 0, 0)),
        ],
        out_shape=[
            jax.ShapeDtypeStruct((n, D_MODEL), BF16),
            jax.ShapeDtypeStruct((nb, N_EXPERTS, tb), F32),
            jax.ShapeDtypeStruct((n, HP), F32),
            jax.ShapeDtypeStruct((n, HP), F32),
            jax.ShapeDtypeStruct((nb, 8, HP), F32),
        ],
        compiler_params=pltpu.CompilerParams(dimension_semantics=("arbitrary",), vmem_limit_bytes=VMEM_LIMIT),
        name=name,
    )(xall, g, wr)


def _gather_kernel(tile_ref, blk_ref, exp_ref, off_ref, first_ref, act_ref, rank_t_ref, hn_ref, init_ref, xs_ref,
                   *, tb):
    del tile_ref, blk_ref, init_ref
    p = pl.program_id(0)

    @pl.when(act_ref[p] == 1)
    def _():
        rrow = rank_t_ref[pl.ds(exp_ref[p], 1), :]
        slot = (lax.broadcasted_iota(jnp.int32, (GATHER_TILE, tb), 0) - off_ref[p]).astype(F32)
        onehot = jnp.where(rrow == slot, 1.0, 0.0).astype(BF16)
        val = _dot(onehot, hn_ref[...]).astype(BF16)

        @pl.when(first_ref[p] == 1)
        def _set():
            xs_ref[...] = val

        @pl.when(first_ref[p] == 0)
        def _add():
            xs_ref[...] += val


def _gather_call(pairs, rank_t, hn, init, *, tb, name):
    n_rows = init.shape[0]
    n_pairs = pairs[0].shape[0]
    grid_spec = pltpu.PrefetchScalarGridSpec(
        num_scalar_prefetch=6,
        grid=(n_pairs,),
        in_specs=[
            pl.BlockSpec((None, N_EXPERTS, tb), lambda p, tile, blk, *_: (blk[p], 0, 0)),
            pl.BlockSpec((tb, D_MODEL), lambda p, tile, blk, *_: (blk[p], 0)),
            pl.BlockSpec(memory_space=pl.ANY),
        ],
        out_specs=pl.BlockSpec((GATHER_TILE, D_MODEL), lambda p, tile, *_: (tile[p], 0)),
    )
    return pl.pallas_call(
        functools.partial(_gather_kernel, tb=tb),
        grid_spec=grid_spec,
        out_shape=jax.ShapeDtypeStruct((n_rows, D_MODEL), BF16),
        input_output_aliases={8: 0},
        compiler_params=pltpu.CompilerParams(dimension_semantics=("arbitrary",), vmem_limit_bytes=VMEM_LIMIT),
        name=name,
    )(*pairs, rank_t, hn, init)


def _experts_kernel(exp_ref, act_ref, xs_ref, w1_ref, w3_ref, w2_ref, ys_ref, acc_s, *, n_f):
    del exp_ref
    i = pl.program_id(0)
    f = pl.program_id(1)

    @pl.when(f == 0)
    def _zero():
        acc_s[...] = jnp.zeros(acc_s.shape, F32)

    @pl.when(act_ref[i] == 1)
    def _():
        xe = xs_ref[...]
        h1 = _dot(xe, w1_ref[...].astype(BF16))
        h3 = _dot(xe, w3_ref[...].astype(BF16))
        a = (h1 * _sigmoid(h1) * h3).astype(BF16)
        acc_s[...] += _dot(a, w2_ref[...].astype(BF16))

    @pl.when(f == n_f - 1)
    def _out():
        ys_ref[...] = acc_s[...].astype(BF16)


def _experts_call(tile_exp, tile_act, xs, w1, w3, w2, slot, *, tm, tf, name):
    n_rows = xs.shape[0]
    n_f = D_FF // tf
    grid_spec = pltpu.PrefetchScalarGridSpec(
        num_scalar_prefetch=2,
        grid=(n_rows // tm, n_f),
        in_specs=[
            pl.BlockSpec((tm, D_MODEL), lambda i, f, ex, act: (i, 0)),
            pl.BlockSpec((None, None, D_MODEL, tf), lambda i, f, ex, act: (slot, ex[i], 0, f)),
            pl.BlockSpec((None, None, D_MODEL, tf), lambda i, f, ex, act: (slot, ex[i], 0, f)),
            pl.BlockSpec((None, None, tf, D_MODEL), lambda i, f, ex, act: (slot, ex[i], f, 0)),
        ],
        out_specs=pl.BlockSpec((tm, D_MODEL), lambda i, f, ex, act: (i, 0)),
        scratch_shapes=[pltpu.VMEM((tm, D_MODEL), F32)],
    )
    return pl.pallas_call(
        functools.partial(_experts_kernel, n_f=n_f),
        grid_spec=grid_spec,
        out_shape=jax.ShapeDtypeStruct((n_rows, D_MODEL), BF16),
        compiler_params=pltpu.CompilerParams(dimension_semantics=("arbitrary", "arbitrary"),
                                             vmem_limit_bytes=VMEM_LIMIT),
        name=name,
    )(tile_exp, tile_act, xs, w1, w3, w2)


def _combine_kernel(blk_ref, tile_ref, exp_ref, off_ref, first_ref, last_ref, act_ref,
                    x_ref, rank_c_ref, comb_ref, ys_ref, fg_ref, o_ref, *, tb, final_norm):
    del blk_ref, tile_ref
    p = pl.program_id(0)

    @pl.when(first_ref[p] == 1)
    def _start():
        o_ref[...] = x_ref[...]

    @pl.when(act_ref[p] == 1)
    def _():
        pick = lax.broadcasted_iota(jnp.int32, (tb, HP), 1) == exp_ref[p]
        rcol = jnp.sum(jnp.where(pick, rank_c_ref[...], 0.0), axis=1, keepdims=True)
        gcol = jnp.sum(jnp.where(pick, comb_ref[...], 0.0), axis=1, keepdims=True)
        slot = (lax.broadcasted_iota(jnp.int32, (tb, GATHER_TILE), 1) - off_ref[p]).astype(F32)
        onehot = jnp.where(rcol == slot, 1.0, 0.0).astype(BF16)
        o_ref[...] += gcol * _dot(onehot, ys_ref[...])

    if final_norm:
        @pl.when(last_ref[p] == 1)
        def _norm():
            y = o_ref[...]
            ms = jnp.mean(y * y, axis=-1, keepdims=True)
            o_ref[...] = y * lax.rsqrt(ms + EPS) * fg_ref[...]


def _combine_call(pairs, xall, rank_c, comb, ys, fg, *, tb, final_norm, name):
    n_pairs = pairs[0].shape[0]
    grid_spec = pltpu.PrefetchScalarGridSpec(
        num_scalar_prefetch=7,
        grid=(n_pairs,),
        in_specs=[
            pl.BlockSpec((tb, D_MODEL), lambda p, blk, *_: (blk[p], 0)),
            pl.BlockSpec((tb, HP), lambda p, blk, *_: (blk[p], 0)),
            pl.BlockSpec((tb, HP), lambda p, blk, *_: (blk[p], 0)),
            pl.BlockSpec((GATHER_TILE, D_MODEL), lambda p, blk, tile, *_: (tile[p], 0)),
            pl.BlockSpec((1, D_MODEL), lambda p, *_: (0, 0)),
        ],
        out_specs=pl.BlockSpec((tb, D_MODEL), lambda p, blk, *_: (blk[p], 0)),
    )
    return pl.pallas_call(
        functools.partial(_combine_kernel, tb=tb, final_norm=final_norm),
        grid_spec=grid_spec,
        out_shape=jax.ShapeDtypeStruct(xall.shape, F32),
        input_output_aliases={7: 0},
        compiler_params=pltpu.CompilerParams(dimension_semantics=("arbitrary",), vmem_limit_bytes=VMEM_LIMIT),
        name=name,
    )(*pairs, xall, rank_c, comb, ys, fg)


def _pair_lists(cnt, *, tm, n_rows):
    nb = cnt.shape[0]
    i32 = jnp.int32
    tot = jnp.sum(cnt, axis=0)
    grp_rows = (tot + tm - 1) // tm * tm
    grp_end = jnp.cumsum(grp_rows)
    grp_start = grp_end - grp_rows
    base = grp_start[None, :] + jnp.cumsum(cnt, axis=0) - cnt
    lo = base // GATHER_TILE
    hi = (base + cnt - 1) // GATHER_TILE
    npair = jnp.where(cnt > 0, hi - lo + 1, 0)
    n_pairs = nb * N_EXPERTS + n_rows // GATHER_TILE
    n_keys = nb * N_EXPERTS

    def expand(expert_major):
        def flat(a):
            return (a.T if expert_major else a).reshape(-1)
        np_k = flat(npair)
        cum = jnp.cumsum(np_k)
        total = cum[-1]
        p = jnp.arange(n_pairs, dtype=i32)
        pc = jnp.minimum(p, total - 1)
        k = jnp.sum((cum[None, :] <= pc[:, None]).astype(i32), axis=1)
        table = jnp.stack([flat(lo), flat(base), cum - np_k], axis=1)
        row = jnp.take(table, jnp.minimum(k, n_keys - 1), axis=0)
        tile = row[:, 0] + pc - row[:, 2]
        blk, exp = (k % nb, k // nb) if expert_major else (k // N_EXPERTS, k % N_EXPERTS)
        act = (p < total).astype(i32)
        return tile.astype(i32), blk.astype(i32), exp.astype(i32), (row[:, 1] - tile * GATHER_TILE).astype(i32), act

    g_tile, g_blk, g_exp, g_off, g_act = expand(True)
    g_first = jnp.concatenate([jnp.ones((1,), i32), (g_tile[1:] != g_tile[:-1]).astype(i32)])
    c_tile, c_blk, c_exp, c_off, c_act = expand(False)
    c_first = jnp.concatenate([jnp.ones((1,), i32), (c_blk[1:] != c_blk[:-1]).astype(i32)])
    c_last = jnp.concatenate([(c_blk[1:] != c_blk[:-1]).astype(i32), jnp.ones((1,), i32)])
    c_last = jnp.where(jnp.arange(n_pairs) == jnp.sum(c_act) - 1, 1, c_last) * c_act
    n_tiles = n_rows // tm
    t0 = jnp.arange(n_tiles, dtype=i32) * tm
    t_act = (t0 < grp_end[-1]).astype(i32)
    t_exp = jnp.sum((grp_end[None, :] <= jnp.minimum(t0, grp_end[-1] - 1)[:, None]).astype(i32), axis=1)
    return ((g_tile, g_blk, g_exp, g_off, g_first, g_act),
            (c_blk, c_tile, c_exp, c_off, c_first, c_last, c_act), (t_exp, t_act))


def _moe_layer(xall, g, wr, w1, w3, w2, slot, fg, xs_init, *, tb, tm, tf, final_norm, tag):
    n = xall.shape[0]
    n_rows = 2 * n + N_EXPERTS * tm
    if xs_init is None:
        xs_init = jnp.zeros((n_rows, D_MODEL), BF16)
    hn, rank_t, rank_c, comb, cnt = _route_call(xall, g, wr, tb=tb, name=f"moe_route_{tag}")
    cnt = cnt[:, 0, 0:N_EXPERTS].astype(jnp.int32)
    g_pairs, c_pairs, (t_exp, t_act) = _pair_lists(cnt, tm=tm, n_rows=n_rows)
    xs = _gather_call(g_pairs, rank_t, hn, xs_init, tb=tb, name=f"moe_gather_{tag}")
    ys = _experts_call(t_exp, t_act, xs, w1, w3, w2, slot, tm=tm, tf=tf, name=f"moe_experts_{tag}")
    out = _combine_call(c_pairs, xall, rank_c, comb, ys, fg, tb=tb, final_norm=final_norm,
                        name=f"moe_combine_{tag}")
    return out, xs


def _pad_heads(a, axis):
    axis = axis % a.ndim
    shp = a.shape
    a = a.reshape(shp[:axis] + (NH, DH) + shp[axis + 1:])
    padw = [(0, 0)] * a.ndim
    padw[axis + 1] = (0, HP - DH)
    a = jnp.pad(a, padw)
    return a.reshape(shp[:axis] + (HW,) + shp[axis + 1:])


def _unpad_heads(a, axis):
    axis = axis % a.ndim
    shp = a.shape
    a = a.reshape(shp[:axis] + (NH, HP) + shp[axis + 1:])
    a = lax.slice_in_dim(a, 0, DH, axis=axis + 1)
    return a.reshape(shp[:axis] + (NH * DH,) + shp[axis + 1:])


_M_W = NH * DH
SRC_MQ, SRC_MK, SRC_MV, SRC_MO = 0, _M_W, 2 * _M_W, 3 * _M_W
SRC_MI = 4 * _M_W
SRC_MF = SRC_MI + NH
SRC_PU = SRC_MF + NH
SRC_HQ = SRC_PU + P_W
SRC_HF = SRC_HQ + HW
SRC_HI = SRC_HF + HW
SRC_HG = SRC_HI + _M_W
D_IN = SRC_HG + _M_W


def _w_in_relayout_kernel(w_ref, o32_ref, o16_ref):
    rows = w_ref.shape[0]

    def put(dst, val):
        o32_ref[:, dst:dst + val.shape[1]] = val
        o16_ref[:, dst:dst + val.shape[1]] = val.astype(BF16)

    for src, dst in ((SRC_MQ, OFF_MQ), (SRC_MK, OFF_MK), (SRC_MV, OFF_MV), (SRC_MO, OFF_MO),
                     (SRC_HI, OFF_HI), (SRC_HG, OFF_HG)):
        for h in range(NH):
            put(dst + HP * h, w_ref[:, src + DH * h:src + DH * (h + 1)])
            put(dst + HP * h + DH, jnp.zeros((rows, HP - DH), F32))
    put(OFF_G, w_ref[:, SRC_MI:SRC_MI + NH])
    put(OFF_G + NH, w_ref[:, SRC_MF:SRC_MF + NH])
    put(OFF_G + 2 * NH, jnp.zeros((rows, HP - 2 * NH), F32))
    put(OFF_PU, w_ref[:, SRC_PU:SRC_PU + P_W])
    put(OFF_HQ, w_ref[:, SRC_HQ:SRC_HQ + HW])
    put(OFF_HF, w_ref[:, SRC_HF:SRC_HF + HW])


def _w_in_relayout(w_in, *, tr=256):
    d, k, n = w_in.shape
    assert n == D_IN and k % tr == 0
    return pl.pallas_call(
        _w_in_relayout_kernel,
        grid=(d, k // tr),
        in_specs=[pl.BlockSpec((None, tr, n), lambda l, i: (l, i, 0))],
        out_specs=[pl.BlockSpec((None, tr, D_INP), lambda l, i: (l, i, 0)),
                   pl.BlockSpec((None, tr, D_INP), lambda l, i: (l, i, 0))],
        out_shape=[jax.ShapeDtypeStruct((d, k, D_INP), F32), jax.ShapeDtypeStruct((d, k, D_INP), BF16)],
        compiler_params=pltpu.CompilerParams(dimension_semantics=("arbitrary", "arbitrary"),
                                             vmem_limit_bytes=VMEM_LIMIT),
        name="w_in_relayout",
    )(w_in)


def _prep_weights(norm1_g, w_in, b_igate, b_fgate, conv_w, mlstm_norm_g, pool_w, pool_scale, lb_logits,
                  hgrn_norm_g, w_out):
    m_w = _M_W
    w_in_p, w_in_b = _w_in_relayout(w_in)
    w_out_p = jnp.concatenate([_pad_heads(w_out[:, 0:m_w], 1), w_out[:, m_w:m_w + P_W],
                               _pad_heads(w_out[:, m_w + P_W:], 1)], axis=1)
    gbias = jnp.pad(jnp.concatenate([b_igate, b_fgate], axis=-1), ((0, 0), (0, HP - 2 * NH)))[:, None, :]
    conv_p = jnp.concatenate([_pad_heads(conv_w[..., 0:m_w], -1), _pad_heads(conv_w[..., m_w:], -1)], axis=-1)
    eye = jnp.eye(len(POOL_WINDOWS), dtype=F32)
    pool_bd = jnp.einsum('lgce,gh->lgche', pool_w, eye).reshape(DEPTH, P_W, P_W)
    return {
        "g1": norm1_g[:, None, :], "w_in": w_in_b, "w_in_f32": w_in_p, "gbias": gbias,
        "conv_w": conv_p, "mnorm": _pad_heads(mlstm_norm_g, -1)[:, None, :],
        "pool_w": pool_bd.astype(BF16), "pool_w_f32": pool_bd,
        "pool_scale": pool_scale[:, None, :], "lb_logits": lb_logits,
        "hnorm": _pad_heads(hgrn_norm_g, -1)[:, None, :], "w_out": w_out_p.astype(BF16), "w_out_f32": w_out_p,
    }


def _states_to_kernel(C, n, m, conv, pool, S):
    nb = C.shape[0]
    caug = jnp.concatenate([C, n[..., None]], axis=-1)
    caug = jnp.pad(caug, ((0, 0), (0, 0), (0, HP - DH), (0, HP - DH - 1)))
    mk = jnp.pad(jnp.broadcast_to(m[:, :, None], (nb, NH, HP)), ((0, 0), (0, 8 - NH), (0, 0)))
    m_w = NH * DH
    convk = jnp.concatenate([_pad_heads(conv[..., 0:m_w], -1), _pad_heads(conv[..., m_w:], -1)], axis=-1)
    convk = jnp.pad(convk, ((0, 0), (CONV_HDR - 3, 0), (0, 0)))
    poolk = jnp.pad(pool, ((0, 0), (1, 0), (0, 0)))
    sk = jnp.pad(jnp.swapaxes(S, -1, -2), ((0, 0), (0, 0), (0, HP - DH), (0, 0)))
    return caug, mk, convk, poolk, sk


def _states_from_kernel(st):
    caug, mk, convk, poolk, sk = st
    C = caug[:, :, 0:DH, 0:DH]
    n = caug[:, :, 0:DH, DH]
    m = mk[:, 0:NH, 0]
    conv = convk[:, CONV_HDR - 3:, :]
    conv = jnp.concatenate([_unpad_heads(conv[..., 0:HW], -1), _unpad_heads(conv[..., HW:], -1)], axis=-1)
    pool = poolk[:, 1:, :]
    S = jnp.swapaxes(sk[:, :, 0:DH, :], -1, -2)
    return C, n, m, conv, pool, S


def _pick_tile(n, candidates):
    for c in candidates:
        if n % c == 0:
            return c
    raise ValueError(f"no row tile for {n}")


def kernel(x_prompt, x_sample, state_mlstm_C, state_mlstm_n, state_mlstm_m, state_mlstm_conv, state_pool,
           state_hgrn, meta_tokens, norm1_g, norm2_g, final_g, w_in, b_igate, b_fgate, conv_w, mlstm_norm_g,
           pool_w, pool_scale, lb_logits, hgrn_norm_g, w_out, ffn_w1, ffn_w3, ffn_w2, router_w, moe_w1,
           moe_w3, moe_w2):
    B, T, _ = x_prompt.shape
    SB, ST, _ = x_sample.shape
    w = _prep_weights(norm1_g, w_in, b_igate, b_fgate, conv_w, mlstm_norm_g, pool_w, pool_scale, lb_logits,
                      hgrn_norm_g, w_out)
    router_p = jnp.pad(router_w, ((0, 0), (0, 0), (0, HP - N_EXPERTS)))

    n_main = B * T
    off_s = n_main
    off_m = off_s + SB * ST
    assert SB * ST == TAIL and n_main % TAIL == 0
    n_tok = off_m + META_TOKENS
    tile = 1280 if n_main >= 16384 else 256
    expert_tile = 1024 if n_main >= 16384 else 256
    n_pad = -(-n_tok // tile) * tile
    xall = jnp.concatenate([x_prompt.reshape(n_main, D_MODEL), x_sample.reshape(SB * ST, D_MODEL),
                            meta_tokens.astype(F32), jnp.zeros((n_pad - n_tok, D_MODEL), F32)], axis=0)
    tv_main = _pick_tile(T, (256, 128))

    zero_states = (jnp.zeros((1, NH, HP, HP), F32), jnp.zeros((1, 8, HP), F32),
                   jnp.zeros((1, CONV_HDR, 2 * HW), F32), jnp.zeros((1, 16, P_W), F32),
                   jnp.zeros((1, NH, HP, HP), F32))
    p_states, s_states = [], []
    xs_buf = None
    st_in = jax.vmap(_states_to_kernel)(state_mlstm_C, state_mlstm_n, state_mlstm_m, state_mlstm_conv,
                                        state_pool, state_hgrn)
    tail_blocks = tuple((b * T + T - TAIL) // TAIL for b in range(B)) + (off_s // TAIL,)
    for l in range(DEPTH):
        precise = l < PRECISE_LAYERS
        xall, st_meta = _mixer_call(xall, zero_states, True, w, l, row_off=off_m, seq_stride=META_TOKENS, nb=1,
                                    seq=META_TOKENS, tv=META_TOKENS, pos0=0, precise=False,
                                    name=f"mixer_meta_{l}")
        if precise:
            xall, st_p = _mixer_call(xall, st_meta, True, w, l, row_off=0, seq_stride=T, nb=B, seq=T - TAIL,
                                     tv=tv_main, pos0=META_TOKENS, precise=False, name=f"mixer_prompt_{l}")
            xall, st_p = _mixer_call(xall, st_p, False, w, l, row_off=T - TAIL, seq_stride=T, nb=B, seq=TAIL,
                                     tv=TAIL, pos0=META_TOKENS + T - TAIL, precise=True,
                                     name=f"mixer_prompt_tail_{l}")
        else:
            xall, st_p = _mixer_call(xall, st_meta, True, w, l, row_off=0, seq_stride=T, nb=B, seq=T,
                                     tv=tv_main, pos0=META_TOKENS, precise=False, name=f"mixer_prompt_{l}")
        xall, st_s = _mixer_call(xall, st_in, False, w, l, row_off=off_s, seq_stride=ST, nb=SB, seq=ST, tv=ST,
                                 pos0=META_TOKENS + PAST_LEN, precise=True, name=f"mixer_sample_{l}",
                                 stacked_states=True)
        p_states.append(st_p)
        s_states.append(st_s)
        i = l // 2
        if l % 2 == 0:
            f32_blocks = tail_blocks if precise else tail_blocks[-1:]
            rows = _ffn_rows_f32_call(xall, f32_blocks, norm2_g[l][None, :], ffn_w1, ffn_w3, ffn_w2, i,
                                      tm=TAIL, tf=512, name=f"ffn_rows_f32_{l}")
            xall = _ffn_call(xall, norm2_g[l][None, :], ffn_w1, ffn_w3, ffn_w2, i,
                             tm=tile, tf=512, name=f"ffn_{l}")
            for j, blk in enumerate(f32_blocks):
                xall = lax.dynamic_update_slice(xall, rows[j * TAIL:(j + 1) * TAIL], (blk * TAIL, 0))
        else:
            xall, xs_buf = _moe_layer(xall, norm2_g[l][None, :], router_p[i], moe_w1, moe_w3, moe_w2, i,
                                      final_g[None, :], xs_buf, tb=tile, tm=expert_tile, tf=512,
                                      final_norm=(l == DEPTH - 1), tag=str(l))
    y_prompt = xall[0:n_main].reshape(B, T, D_MODEL)
    y_sample = xall[off_s:off_m].reshape(SB, ST, D_MODEL)
    p_out = jax.vmap(_states_from_kernel)(tuple(jnp.stack([s[j] for s in p_states], axis=0) for j in range(5)))
    s_out = jax.vmap(_states_from_kernel)(tuple(jnp.stack([s[j] for s in s_states], axis=0) for j in range(5)))
    return (y_prompt, y_sample) + tuple(p_out) + tuple(s_out)
```
